```python
import jax, jax.numpy as jnp
from jax import lax
import numpy as np

D_MODEL = 1024
BATCH = 8
SEQ = 2048
DEPTH = 2

N_MIXERS = 2
N_A_LAYERS = (DEPTH + 1) // 2
N_B_LAYERS = DEPTH // 2

ATT_HEADS = 16
ATT_KV_HEADS = 4
HEAD_DIM = D_MODEL // ATT_HEADS
GROUP = ATT_HEADS // ATT_KV_HEADS
IDX_HEADS = 8
IDX_DIM = 64
TOPK_MAX = 256
Q_BLOCK = 128
ROPE_THETA = 10000.0
Q_COLS = ATT_HEADS * HEAD_DIM
KV_COLS = ATT_KV_HEADS * HEAD_DIM
IQ_COLS = IDX_HEADS * IDX_DIM
IK_COLS = IDX_DIM
IW_COLS = IDX_HEADS
IN_COLS = Q_COLS + 2 * KV_COLS + IQ_COLS + IK_COLS + IW_COLS

RWKV_HEAD = 64
RWKV_HEADS = D_MODEL // RWKV_HEAD
DECAY_LORA = 64
ICL_LORA = 64
GATE_LORA = 160
LNX_EPS = 64e-5

D_FF = 4 * D_MODEL
NORM_EPS = 1e-6

kernel_name = "dsa_rwkv7_interleaved_hybrid"


def rms_norm(x, g):
    xf = x.astype(jnp.float32)
    y = xf * lax.rsqrt(jnp.mean(xf * xf, axis=-1, keepdims=True) + NORM_EPS)
    return (y * g.astype(jnp.float32)).astype(x.dtype)


def rope_tables(length, dim):
    inv = 1.0 / (ROPE_THETA ** (jnp.arange(0, dim, 2, dtype=jnp.float32) / dim))
    ang = jnp.arange(length, dtype=jnp.float32)[:, None] * inv[None, :]
    return jnp.cos(ang), jnp.sin(ang)


def apply_rope(x, cos, sin):
    shape = (1, cos.shape[0]) + (1,) * (x.ndim - 3) + (cos.shape[1],)
    c = cos.reshape(shape)
    s = sin.reshape(shape)
    xf = x.astype(jnp.float32)
    x1, x2 = jnp.split(xf, 2, axis=-1)
    return jnp.concatenate([x1 * c - x2 * s, x2 * c + x1 * s], axis=-1).astype(x.dtype)


def dsa_mixer(h, w_in, w_o):
    B, L, _ = h.shape
    top_k = min(TOPK_MAX, L // 4)
    qb_len = min(Q_BLOCK, L)
    n_blk = L // qb_len
    proj = h @ w_in
    o0 = Q_COLS
    o1 = o0 + KV_COLS
    o2 = o1 + KV_COLS
    o3 = o2 + IQ_COLS
    o4 = o3 + IK_COLS
    q = proj[..., :o0].reshape(B, L, ATT_HEADS, HEAD_DIM)
    k = proj[..., o0:o1].reshape(B, L, ATT_KV_HEADS, HEAD_DIM)
    v = proj[..., o1:o2].reshape(B, L, ATT_KV_HEADS, HEAD_DIM)
    qi = proj[..., o2:o3].reshape(B, L, IDX_HEADS, IDX_DIM)
    ki = proj[..., o3:o4]
    wi = proj[..., o4:] * (IDX_HEADS ** -0.5 * IDX_DIM ** -0.5)
    cos, sin = rope_tables(L, HEAD_DIM)
    q = apply_rope(q, cos, sin)
    k = apply_rope(k, cos, sin)
    cos_i, sin_i = rope_tables(L, IDX_DIM)
    qi = apply_rope(qi, cos_i, sin_i)
    ki = apply_rope(ki, cos_i, sin_i)

    def to_blocks(a):
        return jnp.moveaxis(a.reshape((B, n_blk, qb_len) + a.shape[2:]), 1, 0)

    s_pos = jnp.arange(L, dtype=jnp.int32)

    def block_fn(args):
        q_blk, qi_blk, wi_blk, start = args
        t_pos = start + jnp.arange(qb_len, dtype=jnp.int32)
        rel = jax.nn.relu(jnp.einsum('bthd,bsd->bths', qi_blk, ki).astype(jnp.float32))
        score = jnp.einsum('bths,bth->bts', rel, wi_blk.astype(jnp.float32))
        causal = s_pos[None, :] <= t_pos[:, None]
        score = jnp.where(causal[None], score, -jnp.inf)
        _, sel = lax.top_k(score, top_k)
        valid = sel <= t_pos[None, :, None]
        k_sel = jax.vmap(lambda kb, ib: kb[ib])(k, sel)
        v_sel = jax.vmap(lambda vb, ib: vb[ib])(v, sel)
        qg = q_blk.reshape(B, qb_len, ATT_KV_HEADS, GROUP, HEAD_DIM)
        logits = jnp.einsum('btngd,btknd->btngk', qg, k_sel).astype(jnp.float32) * (HEAD_DIM ** -0.5)
        logits = jnp.where(valid[:, :, None, None, :], logits, -jnp.inf)
        p = jax.nn.softmax(logits, axis=-1).astype(v_sel.dtype)
        o = jnp.einsum('btngk,btknd->btngd', p, v_sel)
        return o.reshape(B, qb_len, ATT_HEADS * HEAD_DIM)

    starts = jnp.arange(n_blk, dtype=jnp.int32) * qb_len
    o = lax.map(block_fn, (to_blocks(q), to_blocks(qi), to_blocks(wi), starts))
    o = jnp.moveaxis(o, 0, 1).reshape(B, L, ATT_HEADS * HEAD_DIM)
    return o @ w_o


def token_shift(x):
    return jnp.pad(x, ((0, 0), (1, 0), (0, 0)))[:, :-1]


def rwkv7_mixer(h, mu, w_rkv, w0, w1, w2, a0, a1, a2, g1, g2, k_k, k_a, r_k, lnx_w, lnx_b, w_o):
    B, L, D = h.shape
    H, N = RWKV_HEADS, RWKV_HEAD
    f32 = jnp.float32
    xx = token_shift(h) - h
    xs = h[None] + xx[None] * mu[:, None, None, :]
    r, k, v = jnp.einsum('cbld,cde->cble', xs[:3], w_rkv)
    xw, xa, xg = xs[3], xs[4], xs[5]
    w_log = -jax.nn.softplus(-(w0 + jnp.tanh(xw @ w1) @ w2).astype(f32)) - 0.5
    decay = jnp.exp(-jnp.exp(w_log))
    a = jax.nn.sigmoid((a0 + (xa @ a1) @ a2).astype(f32))
    g = jax.nn.sigmoid(xg @ g1) @ g2
    kf = k.astype(f32)
    kk = (kf * k_k.astype(f32)).reshape(B, L, H, N)
    kk = kk * lax.rsqrt(jnp.maximum(jnp.sum(kk * kk, axis=-1, keepdims=True), 1e-24))
    kf = kf * (1.0 + (a - 1.0) * k_a.astype(f32))

    def heads(z):
        return z.reshape(B, L, H, N)

    rh, wh, kh, vh, ah = heads(r.astype(f32)), heads(decay), heads(kf), heads(v.astype(f32)), heads(a)
    bh = kk * ah

    def step(state, inp):
        r_t, w_t, k_t, v_t, kk_t, b_t = inp
        sa = jnp.einsum('bhvk,bhk->bhv', state, -kk_t)
        state = (state * w_t[:, :, None, :] + sa[..., None] * b_t[:, :, None, :]
                 + v_t[..., None] * k_t[:, :, None, :])
        y = jnp.einsum('bhvk,bhk->bhv', state, r_t)
        return state, y

    def tm(z):
        return jnp.moveaxis(z, 1, 0)

    state0 = jnp.zeros((B, H, N, N), f32)
    _, y = lax.scan(step, state0, (tm(rh), tm(wh), tm(kh), tm(vh), tm(kk), tm(bh)))
    y = jnp.moveaxis(y, 0, 1)
    mean = jnp.mean(y, axis=-1, keepdims=True)
    var = jnp.mean(jnp.square(y - mean), axis=-1, keepdims=True)
    y = ((y - mean) * lax.rsqrt(var + LNX_EPS) * lnx_w.astype(f32).reshape(H, N)
         + lnx_b.astype(f32).reshape(H, N))
    bonus = jnp.sum(rh * kh * r_k.astype(f32), axis=-1, keepdims=True) * vh
    out = (y + bonus).reshape(B, L, D).astype(h.dtype) * g
    return out @ w_o


def sqrelu_mlp(h, w_up, w_down):
    u = jax.nn.relu(h @ w_up)
    return (u * u) @ w_down


def setup_inputs(seed: int = 0) -> dict:
    key = jax.random.key(seed)
    ks = jax.random.split(key, 32)
    D = D_MODEL

    def nrm(k, shape, scale):
        return jax.random.normal(k, shape, jnp.float32) * scale

    nA, nB = N_A_LAYERS, N_B_LAYERS
    return {
        "x": nrm(ks[0], (BATCH, SEQ, D), 1.0),
        "mixer_norm": 1.0 + nrm(ks[1], (DEPTH, D), 0.02),
        "mlp_norm": 1.0 + nrm(ks[2], (DEPTH, D), 0.02),
        "mlp_w_up": nrm(ks[3], (DEPTH, D, D_FF), D ** -0.5),
        "mlp_w_down": nrm(ks[4], (DEPTH, D_FF, D), D_FF ** -0.5),
        "final_norm": 1.0 + nrm(ks[5], (D,), 0.02),
        "dsa_w_in": nrm(ks[6], (nA, D, IN_COLS), D ** -0.5),
        "dsa_w_o": nrm(ks[7], (nA, Q_COLS, D), Q_COLS ** -0.5),
        "rwkv_mu": jax.random.uniform(ks[8], (nB, 6, D), jnp.float32),
        "rwkv_w_rkv": nrm(ks[9], (nB, 3, D, D), D ** -0.5),
        "rwkv_w0": jnp.linspace(-6.0, -1.0, D, dtype=jnp.float32)[None, :] + nrm(ks[10], (nB, D), 0.1),
        "rwkv_w1": nrm(ks[11], (nB, D, DECAY_LORA), D ** -0.5),
        "rwkv_w2": nrm(ks[12], (nB, DECAY_LORA, D), 0.1 * DECAY_LORA ** -0.5),
        "rwkv_a0": nrm(ks[13], (nB, D), 0.1),
        "rwkv_a1": nrm(ks[14], (nB, D, ICL_LORA), D ** -0.5),
        "rwkv_a2": nrm(ks[15], (nB, ICL_LORA, D), 0.1 * ICL_LORA ** -0.5),
        "rwkv_g1": nrm(ks[16], (nB, D, GATE_LORA), D ** -0.5),
        "rwkv_g2": nrm(ks[17], (nB, GATE_LORA, D), GATE_LORA ** -0.5),
        "rwkv_k_k": 0.85 + nrm(ks[18], (nB, D), 0.02),
        "rwkv_k_a": 1.0 + nrm(ks[19], (nB, D), 0.02),
        "rwkv_r_k": -0.04 + nrm(ks[20], (nB, RWKV_HEADS, RWKV_HEAD), 0.02),
        "rwkv_lnx_w": 1.0 + nrm(ks[21], (nB, D), 0.02),
        "rwkv_lnx_b": nrm(ks[22], (nB, D), 0.02),
        "rwkv_w_o": nrm(ks[23], (nB, D, D), D ** -0.5),
    }


def reference(x, mixer_norm, mlp_norm, mlp_w_up, mlp_w_down, final_norm, dsa_w_in, dsa_w_o,
              rwkv_mu, rwkv_w_rkv, rwkv_w0, rwkv_w1, rwkv_w2, rwkv_a0, rwkv_a1, rwkv_a2,
              rwkv_g1, rwkv_g2, rwkv_k_k, rwkv_k_a, rwkv_r_k, rwkv_lnx_w, rwkv_lnx_b, rwkv_w_o):
    for i in range(DEPTH):
        h = rms_norm(x, mixer_norm[i])
        j = i // N_MIXERS
        if i % N_MIXERS == 0:
            x = x + dsa_mixer(h, dsa_w_in[j], dsa_w_o[j])
        else:
            x = x + rwkv7_mixer(h, rwkv_mu[j], rwkv_w_rkv[j], rwkv_w0[j], rwkv_w1[j], rwkv_w2[j],
                                rwkv_a0[j], rwkv_a1[j], rwkv_a2[j], rwkv_g1[j], rwkv_g2[j],
                                rwkv_k_k[j], rwkv_k_a[j], rwkv_r_k[j], rwkv_lnx_w[j],
                                rwkv_lnx_b[j], rwkv_w_o[j])
        x = x + sqrelu_mlp(rms_norm(x, mlp_norm[i]), mlp_w_up[i], mlp_w_down[i])
    return rms_norm(x, final_norm)
```

```python
import functools

import jax
import jax.numpy as jnp
from jax import lax
from jax.experimental import pallas as pl
from jax.experimental.pallas import tpu as pltpu

f32 = jnp.float32
bf16 = jnp.bfloat16

D_MODEL = 1024
D_FF = 4 * D_MODEL
NORM_EPS = 1e-6

ATT_HEADS = 16
ATT_KV_HEADS = 4
HEAD_DIM = 64
IDX_HEADS = 8
TOPK_MAX = 256
ROPE_THETA = 10000.0

RWKV_HEAD = 64
RWKV_HEADS = 16
LNX_EPS = 64e-5
CHUNK = 64

LANES = 128
VMEM_LIMIT = 56 * 1024 * 1024

NT = (((1,), (1,)), ((), ()))
TN = (((0,), (0,)), ((), ()))


def _cparams(*sem):
    return pltpu.CompilerParams(dimension_semantics=sem, vmem_limit_bytes=VMEM_LIMIT)


def _rms(x, g):
    return x * lax.rsqrt(jnp.mean(x * x, axis=-1, keepdims=True) + NORM_EPS) * g


def _bdot(a, b, dims=None):
    a = a.astype(bf16)
    b = b.astype(bf16)
    if dims is None:
        return jnp.dot(a, b, preferred_element_type=f32)
    return lax.dot_general(a, b, dims, preferred_element_type=f32)


def _split_dot(a, b, parts):
    out = None
    rem = a
    for p in range(parts):
        piece = rem.astype(bf16)
        term = jnp.dot(piece, b, preferred_element_type=f32)
        out = term if out is None else out + term
        if p + 1 < parts:
            rem = rem - piece.astype(f32)
    return out


def _const_spec(shape):
    return pl.BlockSpec(shape, lambda *_: (0,) * len(shape))


MLP_TM = 512
MLP_FCH = 512


def _mlp_kernel(x_ref, g_ref, wu_ref, wd_ref, fg_ref, o_ref, *, final):
    x = x_ref[...]
    xn = _rms(x, g_ref[...]).astype(bf16)
    o_ref[...] = x
    for f in range(0, D_FF, MLP_FCH):
        u = jnp.dot(xn, wu_ref[:, f:f + MLP_FCH], preferred_element_type=f32)
        u = jnp.maximum(u, 0.0)
        o_ref[...] += jnp.dot((u * u).astype(bf16), wd_ref[f:f + MLP_FCH, :],
                              preferred_element_type=f32)
    if final:
        o_ref[...] = _rms(o_ref[...], fg_ref[...])


def _mlp(x2, g, wu, wd, fg, final):
    m = x2.shape[0]
    return pl.pallas_call(
        functools.partial(_mlp_kernel, final=final),
        grid=(m // MLP_TM,),
        in_specs=[pl.BlockSpec((MLP_TM, D_MODEL), lambda i: (i, 0)),
                  _const_spec((1, D_MODEL)),
                  _const_spec((D_MODEL, D_FF)),
                  _const_spec((D_FF, D_MODEL)),
                  _const_spec((1, D_MODEL))],
        out_specs=pl.BlockSpec((MLP_TM, D_MODEL), lambda i: (i, 0)),
        out_shape=jax.ShapeDtypeStruct((m, D_MODEL), f32),
        compiler_params=_cparams("parallel"),
        name="mlp",
    )(x2, g, wu, wd, fg)


PROJ_TM = 512


def _resproj_kernel(a_ref, w_ref, r_ref, o_ref):
    o_ref[...] = r_ref[...] + jnp.dot(a_ref[...], w_ref[...], preferred_element_type=f32)


def _resproj(a, w, res):
    m, k = a.shape
    n = w.shape[1]
    return pl.pallas_call(
        _resproj_kernel,
        grid=(m // PROJ_TM,),
        in_specs=[pl.BlockSpec((PROJ_TM, k), lambda i: (i, 0)),
                  _const_spec((k, n)),
                  pl.BlockSpec((PROJ_TM, n), lambda i: (i, 0))],
        out_specs=pl.BlockSpec((PROJ_TM, n), lambda i: (i, 0)),
        out_shape=jax.ShapeDtypeStruct((m, n), f32),
        compiler_params=_cparams("parallel"),
        name="resproj",
    )(a, w, res)


DSA_TM = 256


def _rope(x, cos, sin_signed, first_half):
    fwd = pltpu.roll(x, 32, 1)
    bwd = pltpu.roll(x, 96, 1)
    return x * cos + jnp.where(first_half, bwd, fwd) * sin_signed


def _dsa_proj_kernel(x_ref, g_ref, wq_ref, wk_ref, wv_ref, wqi_ref, wki_ref, wwi_ref,
                     cos_ref, sin_ref,
                     q_ref, ke_ref, ko_ref, ve_ref, vo_ref, qi_ref, kie_ref, kio_ref, wi_ref):
    h = _rms(x_ref[...], g_ref[...]).astype(bf16)
    cos = cos_ref[...]
    sin = sin_ref[...]
    lane = lax.broadcasted_iota(jnp.int32, (1, LANES), 1)
    first_half = (lane % HEAD_DIM) < (HEAD_DIM // 2)
    low = lane < HEAD_DIM

    q = jnp.dot(h, wq_ref[...], preferred_element_type=f32)
    for c in range(0, ATT_HEADS * HEAD_DIM, LANES):
        qc = _rope(q[:, c:c + LANES], cos, sin, first_half) * (HEAD_DIM ** -0.5)
        q_ref[:, c:c + LANES] = qc.astype(bf16)

    k2 = jnp.dot(h, wk_ref[...], preferred_element_type=f32)
    v2 = jnp.dot(h, wv_ref[...], preferred_element_type=f32)
    for c in range(0, ATT_KV_HEADS * LANES, LANES):
        kc = _rope(k2[:, c:c + LANES], cos, sin, first_half)
        ke_ref[:, c:c + LANES] = jnp.where(low, kc, 0.0).astype(bf16)
        ko_ref[:, c:c + LANES] = jnp.where(low, 0.0, kc).astype(bf16)
        vc = v2[:, c:c + LANES]
        ve_ref[:, c:c + LANES] = jnp.where(low, vc, 0.0).astype(bf16)
        vo_ref[:, c:c + LANES] = jnp.where(low, 0.0, vc).astype(bf16)

    qi = jnp.dot(h, wqi_ref[...], preferred_element_type=f32)
    for c in range(0, IDX_HEADS * HEAD_DIM, LANES):
        qi_ref[:, c:c + LANES] = _rope(qi[:, c:c + LANES], cos, sin, first_half).astype(bf16)

    ki2 = _rope(jnp.dot(h, wki_ref[...], preferred_element_type=f32), cos, sin, first_half)
    kie_ref[...] = jnp.where(low, ki2, 0.0).astype(bf16)
    kio_ref[...] = jnp.where(low, 0.0, ki2).astype(bf16)

    wi_ref[...] = (jnp.dot(h, wwi_ref[...], preferred_element_type=f32)
                   * (IDX_HEADS ** -0.5 * HEAD_DIM ** -0.5))


def _dsa_proj(x2, g, wq, wk2, wv2, wqi, wki2, wwi, cos, sin, seq):
    m = x2.shape[0]
    nblk = seq // DSA_TM
    row = lambda n: pl.BlockSpec((DSA_TM, n), lambda i: (i, 0))
    tab = pl.BlockSpec((DSA_TM, LANES), lambda i: (i % nblk, 0))
    kvw = ATT_KV_HEADS * LANES
    outs = [(ATT_HEADS * HEAD_DIM, bf16), (kvw, bf16), (kvw, bf16), (kvw, bf16), (kvw, bf16),
            (IDX_HEADS * HEAD_DIM, bf16), (LANES, bf16), (LANES, bf16), (LANES, f32)]
    return pl.pallas_call(
        _dsa_proj_kernel,
        grid=(m // DSA_TM,),
        in_specs=[row(D_MODEL), _const_spec((1, D_MODEL)),
                  _const_spec(wq.shape), _const_spec(wk2.shape), _const_spec(wv2.shape),
                  _const_spec(wqi.shape), _const_spec(wki2.shape), _const_spec(wwi.shape),
                  tab, tab],
        out_specs=[row(n) for n, _ in outs],
        out_shape=[jax.ShapeDtypeStruct((m, n), dt) for n, dt in outs],
        compiler_params=_cparams("parallel"),
        name="dsa_proj",
    )(x2, g, wq, wk2, wv2, wqi, wki2, wwi, cos, sin)


DSA_TQ = 128
INT_MIN = -2 ** 31


def _count(mask):
    return jnp.sum(mask.astype(jnp.int32), axis=1, keepdims=True)


def _dsa_attn_kernel(q_ref, qi_ref, wi_ref, kie_ref, kio_ref, ke_ref, ko_ref, ve_ref, vo_ref,
                     o_ref, *, seq, top_k):
    t0 = pl.program_id(1) * DSA_TQ
    row = t0 + lax.broadcasted_iota(jnp.int32, (DSA_TQ, 1), 0)
    col = lax.broadcasted_iota(jnp.int32, (1, seq), 1)
    causal = col <= row

    wi = wi_ref[...]
    score = jnp.zeros((DSA_TQ, seq), f32)
    for p in range(IDX_HEADS // 2):
        qip = qi_ref[:, p * LANES:(p + 1) * LANES]
        for half, ki_ref in enumerate((kie_ref, kio_ref)):
            hd = 2 * p + half
            rel = lax.dot_general(qip, ki_ref[...], NT, preferred_element_type=f32)
            score = score + wi[:, hd:hd + 1] * jnp.maximum(rel, 0.0)
    score = jnp.where(causal, score, -jnp.inf)

    bits = lax.bitcast_convert_type(score, jnp.int32)
    key = bits ^ ((bits >> 31) & jnp.int32(0x7FFFFFFF))

    thr = jnp.where(_count(key >= 0) >= top_k, jnp.int32(0), jnp.int32(INT_MIN))

    def value_step(i, thr):
        cand = thr | (jnp.int32(1) << (30 - i))
        return jnp.where(_count(key >= cand) >= top_k, cand, thr)

    thr = lax.fori_loop(0, 31, value_step, thr)

    above = key > thr
    tied = key == thr
    need = top_k - _count(above)

    def index_step(i, pos):
        cand = pos | (jnp.int32(1) << (seq.bit_length() - 2 - i))
        return jnp.where(_count(tied & (col < cand)) < need, cand, pos)

    pos = lax.fori_loop(0, seq.bit_length() - 1, index_step, jnp.zeros((DSA_TQ, 1), jnp.int32))
    keep = (above | (tied & (col <= pos))) & causal

    for pair in range(ATT_HEADS // 2):
        g = pair // 2
        qp = q_ref[:, pair * LANES:(pair + 1) * LANES]
        acc = None
        for k_ref, v_ref in ((ke_ref, ve_ref), (ko_ref, vo_ref)):
            logits = lax.dot_general(qp, k_ref[:, g * LANES:(g + 1) * LANES], NT,
                                     preferred_element_type=f32)
            logits = jnp.where(keep, logits, -jnp.inf)
            mx = jnp.max(logits, axis=1, keepdims=True)
            p = jnp.exp(logits - mx)
            inv = 1.0 / jnp.sum(p, axis=1, keepdims=True)
            part = jnp.dot(p.astype(bf16), v_ref[:, g * LANES:(g + 1) * LANES],
                           preferred_element_type=f32) * inv
            acc = part if acc is None else acc + part
        o_ref[:, pair * LANES:(pair + 1) * LANES] = acc.astype(bf16)


def _dsa_attn(q, qi, wi, kie, kio, ke, ko, ve, vo, batch, seq):
    nq = seq // DSA_TQ
    top_k = min(TOPK_MAX, seq // 4)
    qrow = lambda n: pl.BlockSpec((DSA_TQ, n), lambda b, i: (b * nq + i, 0))
    full = lambda n: pl.BlockSpec((seq, n), lambda b, i: (b, 0))
    kvw = ATT_KV_HEADS * LANES
    return pl.pallas_call(
        functools.partial(_dsa_attn_kernel, seq=seq, top_k=top_k),
        grid=(batch, nq),
        in_specs=[qrow(ATT_HEADS * HEAD_DIM), qrow(IDX_HEADS * HEAD_DIM), qrow(LANES),
                  full(LANES), full(LANES), full(kvw), full(kvw), full(kvw), full(kvw)],
        out_specs=qrow(ATT_HEADS * HEAD_DIM),
        out_shape=jax.ShapeDtypeStruct((batch * seq, ATT_HEADS * HEAD_DIM), bf16),
        compiler_params=_cparams("parallel", "arbitrary"),
        name="dsa_attn",
    )(q, qi, wi, kie, kio, ke, ko, ve, vo)


RW_TM = 256
LORA_PAD = 128
GATE_PAD = 256


def _rwkv_pre_kernel(x_ref, xp_ref, g_ref, mu_ref, wr_ref, wk_ref, wv_ref,
                     w1_ref, w2_ref, a1_ref, a2_ref, g1_ref, g2_ref,
                     w0_ref, a0_ref, kk_ref, ka_ref, rk_ref, tri_ref, hsum_ref,
                     at_ref, bt_ref, kt_ref, rt_ref, v_ref, gc_ref, bonus_ref, gate_ref,
                     lc_scr, *, seq):
    i = pl.program_id(0)
    g = g_ref[...]
    h = _rms(x_ref[...], g)
    hp = _rms(xp_ref[...], g)[7:8, :]
    hp = jnp.where((i * RW_TM) % seq == 0, 0.0, hp)
    rowi = lax.broadcasted_iota(jnp.int32, (RW_TM, 1), 0)
    hs = jnp.where(rowi == 0, hp, pltpu.roll(h, 1, 0))
    xx = hs - h
    mu = mu_ref[...]

    def mix(c):
        return (h + xx * mu[c:c + 1, :]).astype(bf16)

    r = jnp.dot(mix(0), wr_ref[...], preferred_element_type=f32)
    k = jnp.dot(mix(1), wk_ref[...], preferred_element_type=f32)
    v = jnp.dot(mix(2), wv_ref[...], preferred_element_type=f32)

    wl = w0_ref[...] + _bdot(jnp.tanh(jnp.dot(mix(3), w1_ref[...], preferred_element_type=f32)),
                             w2_ref[...])
    nwl = -wl
    softplus = jnp.maximum(nwl, 0.0) + jnp.log1p(jnp.exp(-jnp.abs(nwl)))
    ld = -jnp.exp(-softplus - 0.5)
    a = jax.nn.sigmoid(a0_ref[...] + _bdot(jnp.dot(mix(4), a1_ref[...],
                                                   preferred_element_type=f32), a2_ref[...]))
    gate_ref[...] = _bdot(jax.nn.sigmoid(jnp.dot(mix(5), g1_ref[...],
                                                 preferred_element_type=f32)), g2_ref[...])

    hsum = hsum_ref[...]
    z = k * kk_ref[...]
    kk = z * lax.rsqrt(jnp.maximum(_split_dot(z * z, hsum, 2), 1e-24))
    k2 = k * (1.0 + (a - 1.0) * ka_ref[...])
    bonus_ref[...] = _split_dot(r * k2 * rk_ref[...], hsum, 2) * v

    lc = _split_dot_left(tri_ref[...], ld, 3)
    lc_scr[...] = lc
    for c in range(RW_TM // CHUNK):
        last = lc_scr[c * CHUNK + CHUNK - 1:c * CHUNK + CHUNK, :]
        gc_ref[c * 8:(c + 1) * 8, :] = jnp.broadcast_to(jnp.exp(last), (8, D_MODEL))
    einv = jnp.exp(-lc)
    at_ref[...] = -kk * jnp.exp(lc - ld)
    bt_ref[...] = kk * a * einv
    kt_ref[...] = k2 * einv
    rt_ref[...] = r * jnp.exp(lc)
    v_ref[...] = v


def _split_dot_left(a, b, parts):
    out = None
    rem = b
    for p in range(parts):
        piece = rem.astype(bf16)
        term = jnp.dot(a, piece, preferred_element_type=f32)
        out = term if out is None else out + term
        if p + 1 < parts:
            rem = rem - piece.astype(f32)
    return out


def _rwkv_pre(x2, g, mu8, wr, wk, wv, w1, w2, a1, a2, g1, g2, w0, a0, k_k, k_a, r_k, tri, hsum, seq):
    m = x2.shape[0]
    row = pl.BlockSpec((RW_TM, D_MODEL), lambda i: (i, 0))
    prev = pl.BlockSpec((8, D_MODEL), lambda i: (jnp.maximum(i * (RW_TM // 8) - 1, 0), 0))
    vec = _const_spec((1, D_MODEL))
    gcrow = pl.BlockSpec((RW_TM // CHUNK * 8, D_MODEL), lambda i: (i, 0))
    act = jax.ShapeDtypeStruct((m, D_MODEL), f32)
    return pl.pallas_call(
        functools.partial(_rwkv_pre_kernel, seq=seq),
        grid=(m // RW_TM,),
        in_specs=[row, prev, vec, _const_spec((8, D_MODEL)),
                  _const_spec(wr.shape), _const_spec(wk.shape), _const_spec(wv.shape),
                  _const_spec(w1.shape), _const_spec(w2.shape),
                  _const_spec(a1.shape), _const_spec(a2.shape),
                  _const_spec(g1.shape), _const_spec(g2.shape),
                  vec, vec, vec, vec, vec,
                  _const_spec(tri.shape), _const_spec(hsum.shape)],
        out_specs=[row, row, row, row, row, gcrow, row, row],
        out_shape=[act, act, act, act, act,
                   jax.ShapeDtypeStruct((m // CHUNK * 8, D_MODEL), f32), act, act],
        scratch_shapes=[pltpu.VMEM((RW_TM, D_MODEL), f32)],
        compiler_params=_cparams("parallel"),
        name="rwkv_pre",
    )(x2, x2, g, mu8, wr, wk, wv, w1, w2, a1, a2, g1, g2, w0, a0, k_k, k_a, r_k, tri, hsum)


RA_ROWS = 256


def _inv_unit_lower(n):
    eye = (lax.broadcasted_iota(jnp.int32, n.shape, 0)
           == lax.broadcasted_iota(jnp.int32, n.shape, 1)).astype(f32)
    x = eye + n
    p = n
    steps = (CHUNK - 1).bit_length() - 1
    for _ in range(steps):
        p = _bdot(p, p)
        x = x + _bdot(x, p)
    return x


def _rwkv_chunk_kernel(at_ref, bt_ref, kt_ref, rt_ref, v_ref, gc_ref,
                       q_ref, yl_ref, m_ref, g_ref):
    ri = lax.broadcasted_iota(jnp.int32, (CHUNK, CHUNK), 0)
    ci = lax.broadcasted_iota(jnp.int32, (CHUNK, CHUNK), 1)
    strict = ci < ri
    incl = ci <= ri
    eye = ci == ri
    for c in range(RA_ROWS // CHUNK):
        rows = slice(c * CHUNK, (c + 1) * CHUNK)
        for hd in range(LANES // RWKV_HEAD):
            cols = slice(hd * RWKV_HEAD, (hd + 1) * RWKV_HEAD)
            at = at_ref[rows, cols]
            bt = bt_ref[rows, cols]
            kt = kt_ref[rows, cols]
            rt = rt_ref[rows, cols]
            v = v_ref[rows, cols]
            gc = gc_ref[c * 8:c * 8 + 1, cols]
            aab = jnp.where(strict, _bdot(at, bt, NT), 0.0)
            aak = jnp.where(strict, _bdot(at, kt, NT), 0.0)
            arb = jnp.where(incl, _bdot(rt, bt, NT), 0.0)
            ark = jnp.where(incl, _bdot(rt, kt, NT), 0.0)
            t = _inv_unit_lower(aab)
            w_hat = _bdot(t, at)
            u_loc = _bdot(t, _bdot(aak, v))
            q_ref[rows, cols] = rt + _bdot(arb, w_hat)
            yl_ref[rows, cols] = _bdot(arb, u_loc) + _bdot(ark, v)
            bh = bt * gc
            kh = kt * gc
            m_ref[rows, cols] = jnp.where(eye, gc, 0.0) + _bdot(w_hat, bh, TN)
            g_ref[rows, cols] = _bdot(u_loc, bh, TN) + _bdot(v, kh, TN)


def _rwkv_chunk(at, bt, kt, rt, v, gc):
    m = at.shape[0]
    blk = pl.BlockSpec((RA_ROWS, LANES), lambda i, p: (i, p))
    gblk = pl.BlockSpec((RA_ROWS // CHUNK * 8, LANES), lambda i, p: (i, p))
    act = jax.ShapeDtypeStruct((m, D_MODEL), f32)
    return pl.pallas_call(
        _rwkv_chunk_kernel,
        grid=(m // RA_ROWS, D_MODEL // LANES),
        in_specs=[blk, blk, blk, blk, blk, gblk],
        out_specs=[blk, blk, blk, blk],
        out_shape=[act, act, act, act],
        compiler_params=_cparams("parallel", "parallel"),
        name="rwkv_chunk",
    )(at, bt, kt, rt, v, gc)


def _rwkv_scan_kernel(q_ref, yl_ref, m_ref, g_ref, y_ref, *, seq):
    heads = LANES // RWKV_HEAD

    def body(c, states):
        r0 = pl.multiple_of(c * CHUNK, CHUNK)
        rows = pl.ds(r0, CHUNK)
        q = q_ref[rows, :]
        yl = yl_ref[rows, :]
        mm = m_ref[rows, :]
        gg = g_ref[rows, :]
        new_states = []
        ys = []
        for hd in range(heads):
            cols = slice(hd * RWKV_HEAD, (hd + 1) * RWKV_HEAD)
            s = states[hd]
            ys.append(yl[:, cols] + _bdot(q[:, cols], s, NT))
            s_hi = s.astype(bf16)
            s_lo = (s - s_hi.astype(f32)).astype(bf16)
            mc = mm[:, cols]
            m_hi = mc.astype(bf16)
            m_lo = (mc - m_hi.astype(f32)).astype(bf16)
            sm = (jnp.dot(s_hi, m_hi, preferred_element_type=f32)
                  + jnp.dot(s_lo, m_hi, preferred_element_type=f32)
                  + jnp.dot(s_hi, m_lo, preferred_element_type=f32))
            new_states.append(sm + gg[:, cols])
        y_ref[rows, :] = jnp.concatenate(ys, axis=1)
        return tuple(new_states)

    init = tuple(jnp.zeros((RWKV_HEAD, RWKV_HEAD), f32) for _ in range(heads))
    lax.fori_loop(0, seq // CHUNK, body, init)


def _rwkv_scan(q, yl, mm, gg, batch, seq):
    blk = pl.BlockSpec((seq, LANES), lambda b, p: (b, p))
    return pl.pallas_call(
        functools.partial(_rwkv_scan_kernel, seq=seq),
        grid=(batch, D_MODEL // LANES),
        in_specs=[blk, blk, blk, blk],
        out_specs=blk,
        out_shape=jax.ShapeDtypeStruct((batch * seq, D_MODEL), f32),
        compiler_params=_cparams("parallel", "parallel"),
        name="rwkv_scan",
    )(q, yl, mm, gg)


def _rwkv_post_kernel(y_ref, bonus_ref, gate_ref, x_ref, lw_ref, lb_ref, hsum_ref, wo_ref, o_ref):
    y = y_ref[...]
    hsum = hsum_ref[...]
    mean = _split_dot(y, hsum, 2) * (1.0 / RWKV_HEAD)
    yc = y - mean
    var = _split_dot(yc * yc, hsum, 2) * (1.0 / RWKV_HEAD)
    yn = yc * lax.rsqrt(var + LNX_EPS) * lw_ref[...] + lb_ref[...]
    out = (yn + bonus_ref[...]) * gate_ref[...]
    o_ref[...] = x_ref[...] + jnp.dot(out.astype(bf16), wo_ref[...], preferred_element_type=f32)


def _rwkv_post(y, bonus, gate, x2, lw, lb, hsum, wo):
    m = y.shape[0]
    row = pl.BlockSpec((RW_TM, D_MODEL), lambda i: (i, 0))
    vec = _const_spec((1, D_MODEL))
    return pl.pallas_call(
        _rwkv_post_kernel,
        grid=(m // RW_TM,),
        in_specs=[row, row, row, row, vec, vec, _const_spec(hsum.shape), _const_spec(wo.shape)],
        out_specs=row,
        out_shape=jax.ShapeDtypeStruct((m, D_MODEL), f32),
        compiler_params=_cparams("parallel"),
        name="rwkv_post",
    )(y, bonus, gate, x2, lw, lb, hsum, wo)


def _dup_heads(w, n_heads):
    w = w.reshape(w.shape[0], n_heads, 1, HEAD_DIM)
    return jnp.broadcast_to(w, (w.shape[0], n_heads, 2, HEAD_DIM)).reshape(w.shape[0], n_heads * LANES)


def _pad_cols(w, n):
    return jnp.pad(w, ((0, 0), (0, n - w.shape[1])))


def _pad_rows(w, n):
    return jnp.pad(w, ((0, n - w.shape[0]), (0, 0)))


def _rope_tables(seq):
    inv = 1.0 / (ROPE_THETA ** (jnp.arange(0, HEAD_DIM, 2, dtype=f32) / HEAD_DIM))
    ang = jnp.arange(seq, dtype=f32)[:, None] * inv[None, :]
    cos, sin = jnp.cos(ang), jnp.sin(ang)
    return jnp.tile(jnp.concatenate([cos, cos], 1), (1, 2)), jnp.tile(jnp.concatenate([-sin, sin], 1), (1, 2))


def _dsa_layer(x2, g, w_in, w_o, batch, seq):
    o0 = ATT_HEADS * HEAD_DIM
    o1 = o0 + ATT_KV_HEADS * HEAD_DIM
    o2 = o1 + ATT_KV_HEADS * HEAD_DIM
    o3 = o2 + IDX_HEADS * HEAD_DIM
    o4 = o3 + HEAD_DIM
    w_in = w_in.astype(bf16)
    wq = w_in[:, :o0]
    wk2 = _dup_heads(w_in[:, o0:o1], ATT_KV_HEADS)
    wv2 = _dup_heads(w_in[:, o1:o2], ATT_KV_HEADS)
    wqi = w_in[:, o2:o3]
    wki2 = _dup_heads(w_in[:, o3:o4], 1)
    wwi = _pad_cols(w_in[:, o4:], LANES)
    cos, sin = _rope_tables(seq)
    q, ke, ko, ve, vo, qi, kie, kio, wi = _dsa_proj(x2, g, wq, wk2, wv2, wqi, wki2, wwi, cos, sin, seq)
    o = _dsa_attn(q, qi, wi, kie, kio, ke, ko, ve, vo, batch, seq)
    return _resproj(o, w_o.astype(bf16), x2)


def _rwkv_layer(x2, g, mu, w_rkv, w0, w1, w2, a0, a1, a2, g1, g2, k_k, k_a, r_k, lnx_w, lnx_b, w_o,
                batch, seq):
    vec = lambda p: p.reshape(1, D_MODEL)
    w_rkv = w_rkv.astype(bf16)
    ri = jnp.arange(RW_TM)
    tri = ((ri[:, None] >= ri[None, :]) & (ri[:, None] // CHUNK == ri[None, :] // CHUNK)).astype(bf16)
    di = jnp.arange(D_MODEL)
    hsum = (di[:, None] // RWKV_HEAD == di[None, :] // RWKV_HEAD).astype(bf16)
    at, bt, kt, rt, v, gc, bonus, gate = _rwkv_pre(
        x2, g, _pad_rows(mu, 8), w_rkv[0], w_rkv[1], w_rkv[2],
        _pad_cols(w1, LORA_PAD).astype(bf16), _pad_rows(w2, LORA_PAD).astype(bf16),
        _pad_cols(a1, LORA_PAD).astype(bf16), _pad_rows(a2, LORA_PAD).astype(bf16),
        _pad_cols(g1, GATE_PAD).astype(bf16), _pad_rows(g2, GATE_PAD).astype(bf16),
        vec(w0), vec(a0), vec(k_k), vec(k_a), vec(r_k), tri, hsum, seq)
    q, yl, mm, gg = _rwkv_chunk(at, bt, kt, rt, v, gc)
    y = _rwkv_scan(q, yl, mm, gg, batch, seq)
    return _rwkv_post(y, bonus, gate, x2, vec(lnx_w), vec(lnx_b), hsum, w_o.astype(bf16))


def kernel(x, mixer_norm, mlp_norm, mlp_w_up, mlp_w_down, final_norm, dsa_w_in, dsa_w_o, rwkv_mu, rwkv_w_rkv, rwkv_w0, rwkv_w1, rwkv_w2, rwkv_a0, rwkv_a1, rwkv_a2, rwkv_g1, rwkv_g2, rwkv_k_k, rwkv_k_a, rwkv_r_k, rwkv_lnx_w, rwkv_lnx_b, rwkv_w_o):
    batch, seq, _ = x.shape
    depth = mixer_norm.shape[0]
    x2 = x.reshape(batch * seq, D_MODEL)
    fg = final_norm.reshape(1, D_MODEL)
    for i in range(depth):
        g = mixer_norm[i].reshape(1, D_MODEL)
        j = i // 2
        if i % 2 == 0:
            x2 = _dsa_layer(x2, g, dsa_w_in[j], dsa_w_o[j], batch, seq)
        else:
            x2 = _rwkv_layer(x2, g, rwkv_mu[j], rwkv_w_rkv[j], rwkv_w0[j], rwkv_w1[j], rwkv_w2[j],
                             rwkv_a0[j], rwkv_a1[j], rwkv_a2[j], rwkv_g1[j], rwkv_g2[j],
                             rwkv_k_k[j], rwkv_k_a[j], rwkv_r_k[j], rwkv_lnx_w[j], rwkv_lnx_b[j],
                             rwkv_w_o[j], batch, seq)
        x2 = _mlp(x2, mlp_norm[i].reshape(1, D_MODEL), mlp_w_up[i].astype(bf16),
                  mlp_w_down[i].astype(bf16), fg, final=(i == depth - 1))
    return x2.reshape(batch, seq, D_MODEL)
```

```python
import functools

import jax
import jax.numpy as jnp
from jax import lax
from jax.experimental import pallas as pl
from jax.experimental.pallas import tpu as pltpu

f32 = jnp.float32
bf16 = jnp.bfloat16

D_MODEL = 1024
D_FF = 4 * D_MODEL
NORM_EPS = 1e-6

ATT_HEADS = 16
ATT_KV_HEADS = 4
HEAD_DIM = 64
IDX_HEADS = 8
TOPK_MAX = 256
ROPE_THETA = 10000.0

RWKV_HEAD = 64
RWKV_HEADS = 16
LNX_EPS = 64e-5
CHUNK = 64

LANES = 128
VMEM_LIMIT = 56 * 1024 * 1024

NT = (((1,), (1,)), ((), ()))
TN = (((0,), (0,)), ((), ()))


def _cparams(*sem):
    return pltpu.CompilerParams(dimension_semantics=sem, vmem_limit_bytes=VMEM_LIMIT)


def _rms(x, g):
    return x * lax.rsqrt(jnp.mean(x * x, axis=-1, keepdims=True) + NORM_EPS) * g


def _bdot(a, b, dims=None):
    a = a.astype(bf16)
    b = b.astype(bf16)
    if dims is None:
        return jnp.dot(a, b, preferred_element_type=f32)
    return lax.dot_general(a, b, dims, preferred_element_type=f32)


def _split_dot(a, b, parts):
    out = None
    rem = a
    for p in range(parts):
        piece = rem.astype(bf16)
        term = jnp.dot(piece, b, preferred_element_type=f32)
        out = term if out is None else out + term
        if p + 1 < parts:
            rem = rem - piece.astype(f32)
    return out


def _const_spec(shape):
    return pl.BlockSpec(shape, lambda *_: (0,) * len(shape))


MLP_TM = 512
MLP_FCH = 512


def _mlp_kernel(x_ref, g_ref, wu_ref, wd_ref, fg_ref, o_ref, *, final):
    x = x_ref[...]
    xn = _rms(x, g_ref[...]).astype(bf16)
    o_ref[...] = x
    for f in range(0, D_FF, MLP_FCH):
        u = jnp.dot(xn, wu_ref[:, f:f + MLP_FCH], preferred_element_type=f32)
        u = jnp.maximum(u, 0.0)
        o_ref[...] += jnp.dot((u * u).astype(bf16), wd_ref[f:f + MLP_FCH, :],
                              preferred_element_type=f32)
    if final:
        o_ref[...] = _rms(o_ref[...], fg_ref[...])


def _mlp(x2, g, wu, wd, fg, final):
    m = x2.shape[0]
    return pl.pallas_call(
        functools.partial(_mlp_kernel, final=final),
        grid=(m // MLP_TM,),
        in_specs=[pl.BlockSpec((MLP_TM, D_MODEL), lambda i: (i, 0)),
                  _const_spec((1, D_MODEL)),
                  _const_spec((D_MODEL, D_FF)),
                  _const_spec((D_FF, D_MODEL)),
                  _const_spec((1, D_MODEL))],
        out_specs=pl.BlockSpec((MLP_TM, D_MODEL), lambda i: (i, 0)),
        out_shape=jax.ShapeDtypeStruct((m, D_MODEL), f32),
        compiler_params=_cparams("parallel"),
        name="mlp",
    )(x2, g, wu, wd, fg)


PROJ_TM = 512


def _resproj_kernel(a_ref, w_ref, r_ref, o_ref):
    o_ref[...] = r_ref[...] + jnp.dot(a_ref[...], w_ref[...], preferred_element_type=f32)


def _resproj(a, w, res):
    m, k = a.shape
    n = w.shape[1]
    return pl.pallas_call(
        _resproj_kernel,
        grid=(m // PROJ_TM,),
        in_specs=[pl.BlockSpec((PROJ_TM, k), lambda i: (i, 0)),
                  _const_spec((k, n)),
                  pl.BlockSpec((PROJ_TM, n), lambda i: (i, 0))],
        out_specs=pl.BlockSpec((PROJ_TM, n), lambda i: (i, 0)),
        out_shape=jax.ShapeDtypeStruct((m, n), f32),
        compiler_params=_cparams("parallel"),
        name="resproj",
    )(a, w, res)


DSA_TM = 256


def _rope(x, cos, sin_signed, first_half):
    fwd = pltpu.roll(x, 32, 1)
    bwd = pltpu.roll(x, 96, 1)
    return x * cos + jnp.where(first_half, bwd, fwd) * sin_signed


def _dsa_proj_kernel(x_ref, g_ref, wq_ref, wk_ref, wv_ref, wqi_ref, wki_ref, wwi_ref,
                     cos_ref, sin_ref,
                     q_ref, ke_ref, ko_ref, ve_ref, vo_ref, qi_ref, kie_ref, kio_ref, wi_ref):
    h = _rms(x_ref[...], g_ref[...]).astype(bf16)
    cos = cos_ref[...]
    sin = sin_ref[...]
    lane = lax.broadcasted_iota(jnp.int32, (1, LANES), 1)
    first_half = (lane % HEAD_DIM) < (HEAD_DIM // 2)
    low = lane < HEAD_DIM

    q = jnp.dot(h, wq_ref[...], preferred_element_type=f32)
    for c in range(0, ATT_HEADS * HEAD_DIM, LANES):
        qc = _rope(q[:, c:c + LANES], cos, sin, first_half) * (HEAD_DIM ** -0.5)
        q_ref[:, c:c + LANES] = qc.astype(bf16)

    k2 = jnp.dot(h, wk_ref[...], preferred_element_type=f32)
    v2 = jnp.dot(h, wv_ref[...], preferred_element_type=f32)
    for c in range(0, ATT_KV_HEADS * LANES, LANES):
        kc = _rope(k2[:, c:c + LANES], cos, sin, first_half)
        ke_ref[:, c:c + LANES] = jnp.where(low, kc, 0.0).astype(bf16)
        ko_ref[:, c:c + LANES] = jnp.where(low, 0.0, kc).astype(bf16)
        vc = v2[:, c:c + LANES]
        ve_ref[:, c:c + LANES] = jnp.where(low, vc, 0.0).astype(bf16)
        vo_ref[:, c:c + LANES] = jnp.where(low, 0.0, vc).astype(bf16)

    qi = jnp.dot(h, wqi_ref[...], preferred_element_type=f32)
    for c in range(0, IDX_HEADS * HEAD_DIM, LANES):
        qi_ref[:, c:c + LANES] = _rope(qi[:, c:c + LANES], cos, sin, first_half).astype(bf16)

    ki2 = _rope(jnp.dot(h, wki_ref[...], preferred_element_type=f32), cos, sin, first_half)
    kie_ref[...] = jnp.where(low, ki2, 0.0).astype(bf16)
    kio_ref[...] = jnp.where(low, 0.0, ki2).astype(bf16)

    wi_ref[...] = (jnp.dot(h, wwi_ref[...], preferred_element_type=f32)
                   * (IDX_HEADS ** -0.5 * HEAD_DIM ** -0.5))


def _dsa_proj(x2, g, wq, wk2, wv2, wqi, wki2, wwi, cos, sin, seq):
    m = x2.shape[0]
    nblk = seq // DSA_TM
    row = lambda n: pl.BlockSpec((DSA_TM, n), lambda i: (i, 0))
    tab = pl.BlockSpec((DSA_TM, LANES), lambda i: (i % nblk, 0))
    kvw = ATT_KV_HEADS * LANES
    outs = [(ATT_HEADS * HEAD_DIM, bf16), (kvw, bf16), (kvw, bf16), (kvw, bf16), (kvw, bf16),
            (IDX_HEADS * HEAD_DIM, bf16), (LANES, bf16), (LANES, bf16), (LANES, f32)]
    return pl.pallas_call(
        _dsa_proj_kernel,
        grid=(m // DSA_TM,),
        in_specs=[row(D_MODEL), _const_spec((1, D_MODEL)),
                  _const_spec(wq.shape), _const_spec(wk2.shape), _const_spec(wv2.shape),
                  _const_spec(wqi.shape), _const_spec(wki2.shape), _const_spec(wwi.shape),
                  tab, tab],
        out_specs=[row(n) for n, _ in outs],
        out_shape=[jax.ShapeDtypeStruct((m, n), dt) for n, dt in outs],
        compiler_params=_cparams("parallel"),
        name="dsa_proj",
    )(x2, g, wq, wk2, wv2, wqi, wki2, wwi, cos, sin)


DSA_TQ = 128
INT_MIN = -2 ** 31


def _count(mask):
    return jnp.sum(mask.astype(jnp.int32), axis=1, keepdims=True)


def _dsa_attn_kernel(q_ref, qi_ref, wi_ref, kie_ref, kio_ref, ke_ref, ko_ref, ve_ref, vo_ref,
                     o_ref, *, seq, top_k):
    t0 = pl.program_id(1) * DSA_TQ
    row = t0 + lax.broadcasted_iota(jnp.int32, (DSA_TQ, 1), 0)
    col = lax.broadcasted_iota(jnp.int32, (1, seq), 1)
    causal = col <= row

    wi = wi_ref[...]
    score = jnp.zeros((DSA_TQ, seq), f32)
    for p in range(IDX_HEADS // 2):
        qip = qi_ref[:, p * LANES:(p + 1) * LANES]
        for half, ki_ref in enumerate((kie_ref, kio_ref)):
            hd = 2 * p + half
            rel = lax.dot_general(qip, ki_ref[...], NT, preferred_element_type=f32)
            score = score + wi[:, hd:hd + 1] * jnp.maximum(rel, 0.0)
    score = jnp.where(causal, score, -jnp.inf)

    bits = lax.bitcast_convert_type(score, jnp.int32)
    key = bits ^ ((bits >> 31) & jnp.int32(0x7FFFFFFF))

    thr = jnp.where(_count(key >= 0) >= top_k, jnp.int32(0), jnp.int32(INT_MIN))

    def value_step(i, thr):
        cand = thr | (jnp.int32(1) << (30 - i))
        return jnp.where(_count(key >= cand) >= top_k, cand, thr)

    thr = lax.fori_loop(0, 31, value_step, thr)

    above = key > thr
    tied = key == thr
    need = top_k - _count(above)

    def index_step(i, pos):
        cand = pos | (jnp.int32(1) << (seq.bit_length() - 2 - i))
        return jnp.where(_count(tied & (col < cand)) < need, cand, pos)

    pos = lax.fori_loop(0, seq.bit_length() - 1, index_step, jnp.zeros((DSA_TQ, 1), jnp.int32))
    keep = (above | (tied & (col <= pos))) & causal

    for pair in range(ATT_HEADS // 2):
        g = pair // 2
        qp = q_ref[:, pair * LANES:(pair + 1) * LANES]
        acc = None
        for k_ref, v_ref in ((ke_ref, ve_ref), (ko_ref, vo_ref)):
            logits = lax.dot_general(qp, k_ref[:, g * LANES:(g + 1) * LANES], NT,
                                     preferred_element_type=f32)
            logits = jnp.where(keep, logits, -jnp.inf)
            mx = jnp.max(logits, axis=1, keepdims=True)
            p = jnp.exp(logits - mx)
            inv = 1.0 / jnp.sum(p, axis=1, keepdims=True)
            part = jnp.dot(p.astype(bf16), v_ref[:, g * LANES:(g + 1) * LANES],
                           preferred_element_type=f32) * inv
            acc = part if acc is None else acc + part
        o_ref[:, pair * LANES:(pair + 1) * LANES] = acc.astype(bf16)


def _dsa_attn(q, qi, wi, kie, kio, ke, ko, ve, vo, batch, seq):
    nq = seq // DSA_TQ
    top_k = min(TOPK_MAX, seq // 4)
    qrow = lambda n: pl.BlockSpec((DSA_TQ, n), lambda b, i: (b * nq + i, 0))
    full = lambda n: pl.BlockSpec((seq, n), lambda b, i: (b, 0))
    kvw = ATT_KV_HEADS * LANES
    return pl.pallas_call(
        functools.partial(_dsa_attn_kernel, seq=seq, top_k=top_k),
        grid=(batch, nq),
        in_specs=[qrow(ATT_HEADS * HEAD_DIM), qrow(IDX_HEADS * HEAD_DIM), qrow(LANES),
                  full(LANES), full(LANES), full(kvw), full(kvw), full(kvw), full(kvw)],
        out_specs=qrow(ATT_HEADS * HEAD_DIM),
        out_shape=jax.ShapeDtypeStruct((batch * seq, ATT_HEADS * HEAD_DIM), bf16),
        compiler_params=_cparams("parallel", "arbitrary"),
        name="dsa_attn",
    )(q, qi, wi, kie, kio, ke, ko, ve, vo)


RW_TM = 256
LORA_PAD = 128
GATE_PAD = 256


def _rwkv_pre_kernel(x_ref, xp_ref, g_ref, mu_ref, wr_ref, wk_ref, wv_ref,
                     w1_ref, w2_ref, a1_ref, a2_ref, g1_ref, g2_ref,
                     w0_ref, a0_ref, kk_ref, ka_ref, rk_ref, tri_ref, hsum_ref,
                     at_ref, bt_ref, kt_ref, rt_ref, v_ref, gc_ref, bonus_ref, gate_ref,
                     lc_scr, *, seq):
    i = pl.program_id(0)
    g = g_ref[...]
    h = _rms(x_ref[...], g)
    hp = _rms(xp_ref[...], g)[7:8, :]
    hp = jnp.where((i * RW_TM) % seq == 0, 0.0, hp)
    rowi = lax.broadcasted_iota(jnp.int32, (RW_TM, 1), 0)
    hs = jnp.where(rowi == 0, hp, pltpu.roll(h, 1, 0))
    xx = hs - h
    mu = mu_ref[...]

    def mix(c):
        return (h + xx * mu[c:c + 1, :]).astype(bf16)

    r = jnp.dot(mix(0), wr_ref[...], preferred_element_type=f32)
    k = jnp.dot(mix(1), wk_ref[...], preferred_element_type=f32)
    v = jnp.dot(mix(2), wv_ref[...], preferred_element_type=f32)

    wl = w0_ref[...] + _bdot(jnp.tanh(jnp.dot(mix(3), w1_ref[...], preferred_element_type=f32)),
                             w2_ref[...])
    nwl = -wl
    softplus = jnp.maximum(nwl, 0.0) + jnp.log1p(jnp.exp(-jnp.abs(nwl)))
    ld = -jnp.exp(-softplus - 0.5)
    a = jax.nn.sigmoid(a0_ref[...] + _bdot(jnp.dot(mix(4), a1_ref[...],
                                                   preferred_element_type=f32), a2_ref[...]))
    gate_ref[...] = _bdot(jax.nn.sigmoid(jnp.dot(mix(5), g1_ref[...],
                                                 preferred_element_type=f32)), g2_ref[...])

    hsum = hsum_ref[...]
    z = k * kk_ref[...]
    kk = z * lax.rsqrt(jnp.maximum(_split_dot(z * z, hsum, 2), 1e-24))
    k2 = k * (1.0 + (a - 1.0) * ka_ref[...])
    bonus_ref[...] = _split_dot(r * k2 * rk_ref[...], hsum, 2) * v

    lc = _split_dot_left(tri_ref[...], ld, 3)
    lc_scr[...] = lc
    for c in range(RW_TM // CHUNK):
        last = lc_scr[c * CHUNK + CHUNK - 1:c * CHUNK + CHUNK, :]
        gc_ref[c * 8:(c + 1) * 8, :] = jnp.broadcast_to(jnp.exp(last), (8, D_MODEL))
    einv = jnp.exp(-lc)
    at_ref[...] = -kk * jnp.exp(lc - ld)
    bt_ref[...] = kk * a * einv
    kt_ref[...] = k2 * einv
    rt_ref[...] = r * jnp.exp(lc)
    v_ref[...] = v


def _split_dot_left(a, b, parts):
    out = None
    rem = b
    for p in range(parts):
        piece = rem.astype(bf16)
        term = jnp.dot(a, piece, preferred_element_type=f32)
        out = term if out is None else out + term
        if p + 1 < parts:
            rem = rem - piece.astype(f32)
    return out


def _rwkv_pre(x2, g, mu8, wr, wk, wv, w1, w2, a1, a2, g1, g2, w0, a0, k_k, k_a, r_k, tri, hsum, seq):
    m = x2.shape[0]
    row = pl.BlockSpec((RW_TM, D_MODEL), lambda i: (i, 0))
    prev = pl.BlockSpec((8, D_MODEL), lambda i: (jnp.maximum(i * (RW_TM // 8) - 1, 0), 0))
    vec = _const_spec((1, D_MODEL))
    gcrow = pl.BlockSpec((RW_TM // CHUNK * 8, D_MODEL), lambda i: (i, 0))
    act = jax.ShapeDtypeStruct((m, D_MODEL), f32)
    return pl.pallas_call(
        functools.partial(_rwkv_pre_kernel, seq=seq),
        grid=(m // RW_TM,),
        in_specs=[row, prev, vec, _const_spec((8, D_MODEL)),
                  _const_spec(wr.shape), _const_spec(wk.shape), _const_spec(wv.shape),
                  _const_spec(w1.shape), _const_spec(w2.shape),
                  _const_spec(a1.shape), _const_spec(a2.shape),
                  _const_spec(g1.shape), _const_spec(g2.shape),
                  vec, vec, vec, vec, vec,
                  _const_spec(tri.shape), _const_spec(hsum.shape)],
        out_specs=[row, row, row, row, row, gcrow, row, row],
        out_shape=[act, act, act, act, act,
                   jax.ShapeDtypeStruct((m // CHUNK * 8, D_MODEL), f32), act, act],
        scratch_shapes=[pltpu.VMEM((RW_TM, D_MODEL), f32)],
        compiler_params=_cparams("parallel"),
        name="rwkv_pre",
    )(x2, x2, g, mu8, wr, wk, wv, w1, w2, a1, a2, g1, g2, w0, a0, k_k, k_a, r_k, tri, hsum)


RA_ROWS = 256


def _inv_unit_lower(n):
    eye = (lax.broadcasted_iota(jnp.int32, n.shape, 0)
           == lax.broadcasted_iota(jnp.int32, n.shape, 1)).astype(f32)
    x = eye + n
    p = n
    steps = (CHUNK - 1).bit_length() - 1
    for _ in range(steps):
        p = _bdot(p, p)
        x = x + _bdot(x, p)
    return x


def _rwkv_chunk_kernel(at_ref, bt_ref, kt_ref, rt_ref, v_ref, gc_ref,
                       q_ref, yl_ref, m_ref, g_ref):
    n = RA_ROWS
    ri = lax.broadcasted_iota(jnp.int32, (n, n), 0)
    ci = lax.broadcasted_iota(jnp.int32, (n, n), 1)
    same = (ri // CHUNK) == (ci // CHUNK)
    strict = same & (ci < ri)
    incl = same & (ci <= ri)
    lane = lax.broadcasted_iota(jnp.int32, (1, LANES), 1)
    low = lane < RWKV_HEAD

    at = at_ref[...]
    bt = bt_ref[...]
    kt = kt_ref[...]
    rt = rt_ref[...]
    v = v_ref[...]
    ar = jnp.concatenate([at, rt], axis=0)
    bk = jnp.concatenate([bt, kt], axis=0).astype(bf16)
    vb = v.astype(bf16)

    per_head = []
    for head_lanes in (low, jnp.logical_not(low)):
        g = lax.dot_general(jnp.where(head_lanes, ar, 0.0).astype(bf16), bk, NT,
                            preferred_element_type=f32)
        aab = jnp.where(strict, g[:n, :n], 0.0)
        aak = jnp.where(strict, g[:n, n:], 0.0)
        arb = jnp.where(incl, g[n:, :n], 0.0)
        ark = jnp.where(incl, g[n:, n:], 0.0)
        t = _inv_unit_lower(aab)
        tw = _bdot(t, jnp.concatenate([at, _bdot(aak, vb)], axis=1))
        qy = _bdot(arb, tw)
        per_head.append((tw[:, :LANES], tw[:, LANES:], rt + qy[:, :LANES],
                         qy[:, LANES:] + _bdot(ark, vb)))
    w_hat, u_loc, q, yl = (jnp.where(low, a, b) for a, b in zip(*per_head))
    q_ref[...] = q
    yl_ref[...] = yl

    rk = lax.broadcasted_iota(jnp.int32, (RWKV_HEAD, LANES), 0)
    diag = rk == (lax.broadcasted_iota(jnp.int32, (RWKV_HEAD, LANES), 1) % RWKV_HEAD)
    for c in range(n // CHUNK):
        rows = slice(c * CHUNK, (c + 1) * CHUNK)
        gc = gc_ref[c * 8:c * 8 + 1, :]
        bh = bt[rows] * gc
        kh = kt[rows] * gc
        pm = _bdot(w_hat[rows], bh, TN)
        pg = _bdot(jnp.concatenate([u_loc[rows], v[rows]], axis=0),
                   jnp.concatenate([bh, kh], axis=0), TN)
        m_ref[rows, :] = (jnp.where(low, pm[:RWKV_HEAD], pm[RWKV_HEAD:])
                          + jnp.where(diag, gc, 0.0))
        g_ref[rows, :] = jnp.where(low, pg[:RWKV_HEAD], pg[RWKV_HEAD:])


def _rwkv_chunk(at, bt, kt, rt, v, gc):
    m = at.shape[0]
    blk = pl.BlockSpec((RA_ROWS, LANES), lambda i, p: (i, p))
    gblk = pl.BlockSpec((RA_ROWS // CHUNK * 8, LANES), lambda i, p: (i, p))
    act = jax.ShapeDtypeStruct((m, D_MODEL), f32)
    return pl.pallas_call(
        _rwkv_chunk_kernel,
        grid=(m // RA_ROWS, D_MODEL // LANES),
        in_specs=[blk, blk, blk, blk, blk, gblk],
        out_specs=[blk, blk, blk, blk],
        out_shape=[act, act, act, act],
        compiler_params=_cparams("parallel", "parallel"),
        name="rwkv_chunk",
    )(at, bt, kt, rt, v, gc)


SCAN_LANES = 512


def _rwkv_scan_kernel(q_ref, yl_ref, m_ref, g_ref, y_ref, *, seq):
    pairs = SCAN_LANES // LANES
    low = lax.broadcasted_iota(jnp.int32, (1, LANES), 1) < RWKV_HEAD

    def blockdiag(x):
        return jnp.concatenate([jnp.where(low, x, 0.0), jnp.where(low, 0.0, x)], axis=0)

    def body(c, states):
        rows = pl.ds(pl.multiple_of(c * CHUNK, CHUNK), CHUNK)
        new_states = []
        for p in range(pairs):
            cols = slice(p * LANES, (p + 1) * LANES)
            s = states[p]
            y_ref[rows, cols] = yl_ref[rows, cols] + _bdot(q_ref[rows, cols], s, NT)
            mc = blockdiag(m_ref[rows, cols])
            s_hi = s.astype(bf16)
            s_lo = (s - s_hi.astype(f32)).astype(bf16)
            m_hi = mc.astype(bf16)
            m_lo = (mc - m_hi.astype(f32)).astype(bf16)
            sm = (jnp.dot(s_hi, m_hi, preferred_element_type=f32)
                  + jnp.dot(s_lo, m_hi, preferred_element_type=f32)
                  + jnp.dot(s_hi, m_lo, preferred_element_type=f32))
            new_states.append(sm + blockdiag(g_ref[rows, cols]))
        return tuple(new_states)

    init = tuple(jnp.zeros((LANES, LANES), f32) for _ in range(pairs))
    lax.fori_loop(0, seq // CHUNK, body, init)


def _rwkv_scan(q, yl, mm, gg, batch, seq):
    blk = pl.BlockSpec((seq, SCAN_LANES), lambda b, p: (b, p))
    return pl.pallas_call(
        functools.partial(_rwkv_scan_kernel, seq=seq),
        grid=(batch, D_MODEL // SCAN_LANES),
        in_specs=[blk, blk, blk, blk],
        out_specs=blk,
        out_shape=jax.ShapeDtypeStruct((batch * seq, D_MODEL), f32),
        compiler_params=_cparams("parallel", "parallel"),
        name="rwkv_scan",
    )(q, yl, mm, gg)


def _rwkv_post_kernel(y_ref, bonus_ref, gate_ref, x_ref, lw_ref, lb_ref, hsum_ref, wo_ref, o_ref):
    y = y_ref[...]
    hsum = hsum_ref[...]
    mean = _split_dot(y, hsum, 2) * (1.0 / RWKV_HEAD)
    yc = y - mean
    var = _split_dot(yc * yc, hsum, 2) * (1.0 / RWKV_HEAD)
    yn = yc * lax.rsqrt(var + LNX_EPS) * lw_ref[...] + lb_ref[...]
    out = (yn + bonus_ref[...]) * gate_ref[...]
    o_ref[...] = x_ref[...] + jnp.dot(out.astype(bf16), wo_ref[...], preferred_element_type=f32)


def _rwkv_post(y, bonus, gate, x2, lw, lb, hsum, wo):
    m = y.shape[0]
    row = pl.BlockSpec((RW_TM, D_MODEL), lambda i: (i, 0))
    vec = _const_spec((1, D_MODEL))
    return pl.pallas_call(
        _rwkv_post_kernel,
        grid=(m // RW_TM,),
        in_specs=[row, row, row, row, vec, vec, _const_spec(hsum.shape), _const_spec(wo.shape)],
        out_specs=row,
        out_shape=jax.ShapeDtypeStruct((m, D_MODEL), f32),
        compiler_params=_cparams("parallel"),
        name="rwkv_post",
    )(y, bonus, gate, x2, lw, lb, hsum, wo)


def _dup_heads(w, n_heads):
    w = w.reshape(w.shape[0], n_heads, 1, HEAD_DIM)
    return jnp.broadcast_to(w, (w.shape[0], n_heads, 2, HEAD_DIM)).reshape(w.shape[0], n_heads * LANES)


def _pad_cols(w, n):
    return jnp.pad(w, ((0, 0), (0, n - w.shape[1])))


def _pad_rows(w, n):
    return jnp.pad(w, ((0, n - w.shape[0]), (0, 0)))


def _rope_tables(seq):
    inv = 1.0 / (ROPE_THETA ** (jnp.arange(0, HEAD_DIM, 2, dtype=f32) / HEAD_DIM))
    ang = jnp.arange(seq, dtype=f32)[:, None] * inv[None, :]
    cos, sin = jnp.cos(ang), jnp.sin(ang)
    return jnp.tile(jnp.concatenate([cos, cos], 1), (1, 2)), jnp.tile(jnp.concatenate([-sin, sin], 1), (1, 2))


def _dsa_layer(x2, g, w_in, w_o, batch, seq):
    o0 = ATT_HEADS * HEAD_DIM
    o1 = o0 + ATT_KV_HEADS * HEAD_DIM
    o2 = o1 + ATT_KV_HEADS * HEAD_DIM
    o3 = o2 + IDX_HEADS * HEAD_DIM
    o4 = o3 + HEAD_DIM
    w_in = w_in.astype(bf16)
    wq = w_in[:, :o0]
    wk2 = _dup_heads(w_in[:, o0:o1], ATT_KV_HEADS)
    wv2 = _dup_heads(w_in[:, o1:o2], ATT_KV_HEADS)
    wqi = w_in[:, o2:o3]
    wki2 = _dup_heads(w_in[:, o3:o4], 1)
    wwi = _pad_cols(w_in[:, o4:], LANES)
    cos, sin = _rope_tables(seq)
    q, ke, ko, ve, vo, qi, kie, kio, wi = _dsa_proj(x2, g, wq, wk2, wv2, wqi, wki2, wwi, cos, sin, seq)
    o = _dsa_attn(q, qi, wi, kie, kio, ke, ko, ve, vo, batch, seq)
    return _resproj(o, w_o.astype(bf16), x2)


def _rwkv_layer(x2, g, mu, w_rkv, w0, w1, w2, a0, a1, a2, g1, g2, k_k, k_a, r_k, lnx_w, lnx_b, w_o,
                batch, seq):
    vec = lambda p: p.reshape(1, D_MODEL)
    w_rkv = w_rkv.astype(bf16)
    ri = jnp.arange(RW_TM)
    tri = ((ri[:, None] >= ri[None, :]) & (ri[:, None] // CHUNK == ri[None, :] // CHUNK)).astype(bf16)
    di = jnp.arange(D_MODEL)
    hsum = (di[:, None] // RWKV_HEAD == di[None, :] // RWKV_HEAD).astype(bf16)
    at, bt, kt, rt, v, gc, bonus, gate = _rwkv_pre(
        x2, g, _pad_rows(mu, 8), w_rkv[0], w_rkv[1], w_rkv[2],
        _pad_cols(w1, LORA_PAD).astype(bf16), _pad_rows(w2, LORA_PAD).astype(bf16),
        _pad_cols(a1, LORA_PAD).astype(bf16), _pad_rows(a2, LORA_PAD).astype(bf16),
        _pad_cols(g1, GATE_PAD).astype(bf16), _pad_rows(g2, GATE_PAD).astype(bf16),
        vec(w0), vec(a0), vec(k_k), vec(k_a), vec(r_k), tri, hsum, seq)
    q, yl, mm, gg = _rwkv_chunk(at, bt, kt, rt, v, gc)
    y = _rwkv_scan(q, yl, mm, gg, batch, seq)
    return _rwkv_post(y, bonus, gate, x2, vec(lnx_w), vec(lnx_b), hsum, w_o.astype(bf16))


def kernel(x, mixer_norm, mlp_norm, mlp_w_up, mlp_w_down, final_norm, dsa_w_in, dsa_w_o, rwkv_mu, rwkv_w_rkv, rwkv_w0, rwkv_w1, rwkv_w2, rwkv_a0, rwkv_a1, rwkv_a2, rwkv_g1, rwkv_g2, rwkv_k_k, rwkv_k_a, rwkv_r_k, rwkv_lnx_w, rwkv_lnx_b, rwkv_w_o):
    batch, seq, _ = x.shape
    depth = mixer_norm.shape[0]
    x2 = x.reshape(batch * seq, D_MODEL)
    fg = final_norm.reshape(1, D_MODEL)
    for i in range(depth):
        g = mixer_norm[i].reshape(1, D_MODEL)
        j = i // 2
        if i % 2 == 0:
            x2 = _dsa_layer(x2, g, dsa_w_in[j], dsa_w_o[j], batch, seq)
        else:
            x2 = _rwkv_layer(x2, g, rwkv_mu[j], rwkv_w_rkv[j], rwkv_w0[j], rwkv_w1[j], rwkv_w2[j],
                             rwkv_a0[j], rwkv_a1[j], rwkv_a2[j], rwkv_g1[j], rwkv_g2[j],
                             rwkv_k_k[j], rwkv_k_a[j], rwkv_r_k[j], rwkv_lnx_w[j], rwkv_lnx_b[j],
                             rwkv_w_o[j], batch, seq)
        x2 = _mlp(x2, mlp_norm[i].reshape(1, D_MODEL), mlp_w_up[i].astype(bf16),
                  mlp_w_down[i].astype(bf16), fg, final=(i == depth - 1))
    return x2.reshape(batch, seq, D_MODEL)
```

```python
import functools

import jax
import jax.numpy as jnp
from jax import lax
from jax.experimental import pallas as pl
from jax.experimental.pallas import tpu as pltpu

f32 = jnp.float32
bf16 = jnp.bfloat16

D_MODEL = 1024
D_FF = 4 * D_MODEL
NORM_EPS = 1e-6

ATT_HEADS = 16
ATT_KV_HEADS = 4
HEAD_DIM = 64
IDX_HEADS = 8
TOPK_MAX = 256
ROPE_THETA = 10000.0

RWKV_HEAD = 64
RWKV_HEADS = 16
LNX_EPS = 64e-5
CHUNK = 64

LANES = 128
VMEM_LIMIT = 56 * 1024 * 1024

NT = (((1,), (1,)), ((), ()))
TN = (((0,), (0,)), ((), ()))


def _cparams(*sem):
    return pltpu.CompilerParams(dimension_semantics=sem, vmem_limit_bytes=VMEM_LIMIT)


def _rms(x, g):
    return x * lax.rsqrt(jnp.mean(x * x, axis=-1, keepdims=True) + NORM_EPS) * g


def _bdot(a, b, dims=None):
    a = a.astype(bf16)
    b = b.astype(bf16)
    if dims is None:
        return jnp.dot(a, b, preferred_element_type=f32)
    return lax.dot_general(a, b, dims, preferred_element_type=f32)


def _split_dot(a, b, parts):
    out = None
    rem = a
    for p in range(parts):
        piece = rem.astype(bf16)
        term = jnp.dot(piece, b, preferred_element_type=f32)
        out = term if out is None else out + term
        if p + 1 < parts:
            rem = rem - piece.astype(f32)
    return out


def _const_spec(shape):
    return pl.BlockSpec(shape, lambda *_: (0,) * len(shape))


MLP_TM = 512
MLP_FCH = 512


def _mlp_kernel(x_ref, g_ref, wu_ref, wd_ref, fg_ref, o_ref, *, final):
    x = x_ref[...]
    xn = _rms(x, g_ref[...]).astype(bf16)
    o_ref[...] = x
    for f in range(0, D_FF, MLP_FCH):
        u = jnp.dot(xn, wu_ref[:, f:f + MLP_FCH], preferred_element_type=f32)
        u = jnp.maximum(u, 0.0)
        o_ref[...] += jnp.dot((u * u).astype(bf16), wd_ref[f:f + MLP_FCH, :],
                              preferred_element_type=f32)
    if final:
        o_ref[...] = _rms(o_ref[...], fg_ref[...])


def _mlp(x2, g, wu, wd, fg, final):
    m = x2.shape[0]
    return pl.pallas_call(
        functools.partial(_mlp_kernel, final=final),
        grid=(m // MLP_TM,),
        in_specs=[pl.BlockSpec((MLP_TM, D_MODEL), lambda i: (i, 0)),
                  _const_spec((1, D_MODEL)),
                  _const_spec((D_MODEL, D_FF)),
                  _const_spec((D_FF, D_MODEL)),
                  _const_spec((1, D_MODEL))],
        out_specs=pl.BlockSpec((MLP_TM, D_MODEL), lambda i: (i, 0)),
        out_shape=jax.ShapeDtypeStruct((m, D_MODEL), f32),
        compiler_params=_cparams("parallel"),
        name="mlp",
    )(x2, g, wu, wd, fg)


PROJ_TM = 512


def _resproj_kernel(a_ref, w_ref, r_ref, o_ref):
    o_ref[...] = r_ref[...] + jnp.dot(a_ref[...], w_ref[...], preferred_element_type=f32)


def _resproj(a, w, res):
    m, k = a.shape
    n = w.shape[1]
    return pl.pallas_call(
        _resproj_kernel,
        grid=(m // PROJ_TM,),
        in_specs=[pl.BlockSpec((PROJ_TM, k), lambda i: (i, 0)),
                  _const_spec((k, n)),
                  pl.BlockSpec((PROJ_TM, n), lambda i: (i, 0))],
        out_specs=pl.BlockSpec((PROJ_TM, n), lambda i: (i, 0)),
        out_shape=jax.ShapeDtypeStruct((m, n), f32),
        compiler_params=_cparams("parallel"),
        name="resproj",
    )(a, w, res)


DSA_TM = 256


def _rope(x, cos, sin_signed, first_half):
    fwd = pltpu.roll(x, 32, 1)
    bwd = pltpu.roll(x, 96, 1)
    return x * cos + jnp.where(first_half, bwd, fwd) * sin_signed


LOG2E = 1.4426950408889634


def _dsa_proj_kernel(x_ref, g_ref, wq_ref, wk_ref, wv_ref, wqi_ref, wki_ref, wwi_ref,
                     cos_ref, sin_ref,
                     q_ref, ke_ref, ko_ref, vt_ref, qi_ref, kie_ref, kio_ref, wit_ref):
    h = _rms(x_ref[...], g_ref[...]).astype(bf16)
    cos = cos_ref[...]
    sin = sin_ref[...]
    lane = lax.broadcasted_iota(jnp.int32, (1, LANES), 1)
    first_half = (lane % HEAD_DIM) < (HEAD_DIM // 2)
    low = lane < HEAD_DIM

    q = jnp.dot(h, wq_ref[...], preferred_element_type=f32)
    for c in range(0, ATT_HEADS * HEAD_DIM, LANES):
        qc = _rope(q[:, c:c + LANES], cos, sin, first_half) * (HEAD_DIM ** -0.5 * LOG2E)
        q_ref[:, c:c + LANES] = qc.astype(bf16)

    k2 = jnp.dot(h, wk_ref[...], preferred_element_type=f32)
    for c in range(0, ATT_KV_HEADS * LANES, LANES):
        kc = _rope(k2[:, c:c + LANES], cos, sin, first_half)
        ke_ref[0, :, c:c + LANES] = jnp.where(low, kc, 0.0).astype(bf16)
        ko_ref[0, :, c:c + LANES] = jnp.where(low, 0.0, kc).astype(bf16)

    vt_ref[0] = lax.dot_general(wv_ref[...], h, NT, preferred_element_type=f32).astype(bf16)

    qi = jnp.dot(h, wqi_ref[...], preferred_element_type=f32)
    for c in range(0, IDX_HEADS * HEAD_DIM, LANES):
        qi_ref[:, c:c + LANES] = _rope(qi[:, c:c + LANES], cos, sin, first_half).astype(bf16)

    ki2 = _rope(jnp.dot(h, wki_ref[...], preferred_element_type=f32), cos, sin, first_half)
    kie_ref[0] = jnp.where(low, ki2, 0.0).astype(bf16)
    kio_ref[0] = jnp.where(low, 0.0, ki2).astype(bf16)

    wit_ref[...] = (lax.dot_general(wwi_ref[...], h, NT, preferred_element_type=f32)
                    * (IDX_HEADS ** -0.5 * HEAD_DIM ** -0.5))


def _dsa_proj(x2, g, wq, wk2, wv, wqi, wki2, wwi, cos, sin, batch, seq):
    m = x2.shape[0]
    nblk = seq // DSA_TM
    row = lambda n: pl.BlockSpec((DSA_TM, n), lambda b, i: (b * nblk + i, 0))
    seqrow = lambda n: pl.BlockSpec((1, DSA_TM, n), lambda b, i: (b, i, 0))
    tab = pl.BlockSpec((DSA_TM, LANES), lambda b, i: (i, 0))
    kvw = ATT_KV_HEADS * LANES
    kvd = ATT_KV_HEADS * HEAD_DIM
    return pl.pallas_call(
        _dsa_proj_kernel,
        grid=(batch, nblk),
        in_specs=[row(D_MODEL), _const_spec((1, D_MODEL)),
                  _const_spec(wq.shape), _const_spec(wk2.shape), _const_spec(wv.shape),
                  _const_spec(wqi.shape), _const_spec(wki2.shape), _const_spec(wwi.shape),
                  tab, tab],
        out_specs=[row(ATT_HEADS * HEAD_DIM), seqrow(kvw), seqrow(kvw),
                   pl.BlockSpec((1, kvd, DSA_TM), lambda b, i: (b, 0, i)),
                   row(IDX_HEADS * HEAD_DIM), seqrow(LANES), seqrow(LANES),
                   pl.BlockSpec((LANES, DSA_TM), lambda b, i: (0, b * nblk + i))],
        out_shape=[jax.ShapeDtypeStruct((m, ATT_HEADS * HEAD_DIM), bf16),
                   jax.ShapeDtypeStruct((batch, seq, kvw), bf16),
                   jax.ShapeDtypeStruct((batch, seq, kvw), bf16),
                   jax.ShapeDtypeStruct((batch, kvd, seq), bf16),
                   jax.ShapeDtypeStruct((m, IDX_HEADS * HEAD_DIM), bf16),
                   jax.ShapeDtypeStruct((batch, seq, LANES), bf16),
                   jax.ShapeDtypeStruct((batch, seq, LANES), bf16),
                   jax.ShapeDtypeStruct((LANES, m), f32)],
        compiler_params=_cparams("parallel", "parallel"),
        name="dsa_proj",
    )(x2, g, wq, wk2, wv, wqi, wki2, wwi, cos, sin)


DSA_TQ = 128
DSA_CLS = 512
INT_MIN = -2 ** 31
NEG_INF_KEY = -2139095041


def _fold_rows(x, op):
    while x.shape[0] % 16 == 0:
        h = x.shape[0] // 2
        x = op(x[:h], x[h:])
    return x


def _col_count(mask):
    return jnp.sum(_fold_rows(mask.astype(jnp.int32), jnp.add), axis=0, keepdims=True)


def _dsa_attn_body(q_ref, qi_ref, wit_ref, kie_ref, kio_ref, ke_ref, ko_ref, vt_ref, o_ref,
                   pos_ref, *, width, top_k, t0):
    col_t = t0 + lax.broadcasted_iota(jnp.int32, (1, DSA_TQ), 1)
    row_s = lax.broadcasted_iota(jnp.int32, (width, 1), 0)
    causal = row_s <= col_t

    wit = wit_ref[...]
    score = jnp.zeros((width, DSA_TQ), f32)
    for p in range(IDX_HEADS // 2):
        qip = qi_ref[:, p * LANES:(p + 1) * LANES]
        for half, ki_ref in enumerate((kie_ref, kio_ref)):
            hd = 2 * p + half
            rel = lax.dot_general(ki_ref[0, 0:width, :], qip, NT, preferred_element_type=f32)
            score = score + wit[hd:hd + 1, :] * jnp.maximum(rel, 0.0)
    score = jnp.where(causal, score, -jnp.inf)

    bits = lax.bitcast_convert_type(score, jnp.int32)
    key = bits ^ ((bits >> 31) & jnp.int32(0x7FFFFFFF))

    thr = jnp.where(_col_count(key >= 0) >= top_k, jnp.int32(0), jnp.int32(INT_MIN))

    def value_step(i, thr):
        cand = thr | (jnp.int32(1) << (30 - i))
        return jnp.where(_col_count(key >= cand) >= top_k, cand, thr)

    thr = lax.fori_loop(0, 31, value_step, thr)

    above = key > thr
    tied = key == thr
    n_above = _col_count(above)
    overfull = ((n_above + _col_count(tied)) != top_k) & (thr != NEG_INF_KEY)
    pos_ref[...] = jnp.full(pos_ref.shape, width, jnp.int32)

    @pl.when(jnp.max(overfull.astype(f32)) > 0.0)
    def _():
        need = top_k - n_above
        nbits = (width - 1).bit_length()

        def index_step(i, pos):
            cand = pos | (jnp.int32(1) << (nbits - 1 - i))
            return jnp.where(_col_count(tied & (row_s < cand)) < need, cand, pos)

        pos = lax.fori_loop(0, nbits, index_step, jnp.zeros((1, DSA_TQ), jnp.int32))
        pos_ref[...] = jnp.broadcast_to(pos, pos_ref.shape)

    keep = (above | (tied & (row_s <= pos_ref[0:1, :]))) & causal
    bias = jnp.where(keep, 0.0, -jnp.inf)

    for pair in range(ATT_HEADS // 2):
        g = pair // 2
        qp = q_ref[:, pair * LANES:(pair + 1) * LANES]
        vt = vt_ref[0, g * HEAD_DIM:(g + 1) * HEAD_DIM, 0:width]
        halves = []
        for k_ref in (ke_ref, ko_ref):
            logits = lax.dot_general(k_ref[0, 0:width, g * LANES:(g + 1) * LANES], qp, NT,
                                     preferred_element_type=f32) + bias
            mx = jnp.max(_fold_rows(logits, jnp.maximum), axis=0, keepdims=True)
            p = jnp.exp2(logits - mx)
            inv = 1.0 / jnp.sum(_fold_rows(p, jnp.add), axis=0, keepdims=True)
            halves.append(jnp.dot(vt, p.astype(bf16), preferred_element_type=f32) * inv)
        o_ref[:, pair * LANES:(pair + 1) * LANES] = jnp.concatenate(halves, axis=0).T.astype(bf16)


def _dsa_attn_kernel(q_ref, qi_ref, wit_ref, kie_ref, kio_ref, ke_ref, ko_ref, vt_ref, o_ref,
                     pos_ref, *, seq, top_k):
    i = pl.program_id(1)
    blocks_per_class = DSA_CLS // DSA_TQ
    for cls in range(seq // DSA_CLS):
        @pl.when(i // blocks_per_class == cls)
        def _(cls=cls):
            _dsa_attn_body(q_ref, qi_ref, wit_ref, kie_ref, kio_ref, ke_ref, ko_ref, vt_ref, o_ref,
                           pos_ref, width=DSA_CLS * (cls + 1), top_k=top_k, t0=i * DSA_TQ)


def _dsa_attn(q, qi, wit, kie, kio, ke, ko, vt, batch, seq):
    nq = seq // DSA_TQ
    top_k = min(TOPK_MAX, seq // 4)
    qrow = lambda n: pl.BlockSpec((DSA_TQ, n), lambda b, i: (b * nq + i, 0))
    full = lambda n: pl.BlockSpec((1, seq, n), lambda b, i: (b, 0, 0))
    kvw = ATT_KV_HEADS * LANES
    kvd = ATT_KV_HEADS * HEAD_DIM
    return pl.pallas_call(
        functools.partial(_dsa_attn_kernel, seq=seq, top_k=top_k),
        grid=(batch, nq),
        in_specs=[qrow(ATT_HEADS * HEAD_DIM), qrow(IDX_HEADS * HEAD_DIM),
                  pl.BlockSpec((8, DSA_TQ), lambda b, i: (0, b * nq + i)),
                  full(LANES), full(LANES), full(kvw), full(kvw),
                  pl.BlockSpec((1, kvd, seq), lambda b, i: (b, 0, 0))],
        out_specs=qrow(ATT_HEADS * HEAD_DIM),
        out_shape=jax.ShapeDtypeStruct((batch * seq, ATT_HEADS * HEAD_DIM), bf16),
        scratch_shapes=[pltpu.VMEM((8, DSA_TQ), jnp.int32)],
        compiler_params=_cparams("parallel", "arbitrary"),
        name="dsa_attn",
    )(q, qi, wit, kie, kio, ke, ko, vt)


RW_TM = 256
LORA_PAD = 128
GATE_PAD = 256


def _rwkv_pre_kernel(x_ref, xp_ref, g_ref, mu_ref, wr_ref, wk_ref, wv_ref,
                     w1_ref, w2_ref, a1_ref, a2_ref, g1_ref, g2_ref,
                     w0_ref, a0_ref, kk_ref, ka_ref, rk_ref, tri_ref, hsum_ref,
                     at_ref, bt_ref, kt_ref, rt_ref, v_ref, gc_ref, bonus_ref, gate_ref,
                     lc_scr, *, seq):
    i = pl.program_id(0)
    g = g_ref[...]
    h = _rms(x_ref[...], g)
    hp = _rms(xp_ref[...], g)[7:8, :]
    hp = jnp.where((i * RW_TM) % seq == 0, 0.0, hp)
    rowi = lax.broadcasted_iota(jnp.int32, (RW_TM, 1), 0)
    hs = jnp.where(rowi == 0, hp, pltpu.roll(h, 1, 0))
    xx = hs - h
    mu = mu_ref[...]

    def mix(c):
        return (h + xx * mu[c:c + 1, :]).astype(bf16)

    r = jnp.dot(mix(0), wr_ref[...], preferred_element_type=f32)
    k = jnp.dot(mix(1), wk_ref[...], preferred_element_type=f32)
    v = jnp.dot(mix(2), wv_ref[...], preferred_element_type=f32)

    wl = w0_ref[...] + _bdot(jnp.tanh(jnp.dot(mix(3), w1_ref[...], preferred_element_type=f32)),
                             w2_ref[...])
    nwl = -wl
    softplus = jnp.maximum(nwl, 0.0) + jnp.log1p(jnp.exp(-jnp.abs(nwl)))
    ld = -jnp.exp(-softplus - 0.5)
    a = jax.nn.sigmoid(a0_ref[...] + _bdot(jnp.dot(mix(4), a1_ref[...],
                                                   preferred_element_type=f32), a2_ref[...]))
    gate_ref[...] = _bdot(jax.nn.sigmoid(jnp.dot(mix(5), g1_ref[...],
                                                 preferred_element_type=f32)), g2_ref[...])

    hsum = hsum_ref[...]
    z = k * kk_ref[...]
    kk = z * lax.rsqrt(jnp.maximum(_split_dot(z * z, hsum, 2), 1e-24))
    k2 = k * (1.0 + (a - 1.0) * ka_ref[...])
    bonus_ref[...] = _split_dot(r * k2 * rk_ref[...], hsum, 2) * v

    lc = _split_dot_left(tri_ref[...], ld, 3)
    lc_scr[...] = lc
    for c in range(RW_TM // CHUNK):
        last = lc_scr[c * CHUNK + CHUNK - 1:c * CHUNK + CHUNK, :]
        gc_ref[c * 8:(c + 1) * 8, :] = jnp.broadcast_to(jnp.exp(last), (8, D_MODEL))
    einv = jnp.exp(-lc)
    at_ref[...] = -kk * jnp.exp(lc - ld)
    bt_ref[...] = kk * a * einv
    kt_ref[...] = k2 * einv
    rt_ref[...] = r * jnp.exp(lc)
    v_ref[...] = v


def _split_dot_left(a, b, parts):
    out = None
    rem = b
    for p in range(parts):
        piece = rem.astype(bf16)
        term = jnp.dot(a, piece, preferred_element_type=f32)
        out = term if out is None else out + term
        if p + 1 < parts:
            rem = rem - piece.astype(f32)
    return out


def _rwkv_pre(x2, g, mu8, wr, wk, wv, w1, w2, a1, a2, g1, g2, w0, a0, k_k, k_a, r_k, tri, hsum, seq):
    m = x2.shape[0]
    row = pl.BlockSpec((RW_TM, D_MODEL), lambda i: (i, 0))
    prev = pl.BlockSpec((8, D_MODEL), lambda i: (jnp.maximum(i * (RW_TM // 8) - 1, 0), 0))
    vec = _const_spec((1, D_MODEL))
    gcrow = pl.BlockSpec((RW_TM // CHUNK * 8, D_MODEL), lambda i: (i, 0))
    act = jax.ShapeDtypeStruct((m, D_MODEL), f32)
    return pl.pallas_call(
        functools.partial(_rwkv_pre_kernel, seq=seq),
        grid=(m // RW_TM,),
        in_specs=[row, prev, vec, _const_spec((8, D_MODEL)),
                  _const_spec(wr.shape), _const_spec(wk.shape), _const_spec(wv.shape),
                  _const_spec(w1.shape), _const_spec(w2.shape),
                  _const_spec(a1.shape), _const_spec(a2.shape),
                  _const_spec(g1.shape), _const_spec(g2.shape),
                  vec, vec, vec, vec, vec,
                  _const_spec(tri.shape), _const_spec(hsum.shape)],
        out_specs=[row, row, row, row, row, gcrow, row, row],
        out_shape=[act, act, act, act, act,
                   jax.ShapeDtypeStruct((m // CHUNK * 8, D_MODEL), f32), act, act],
        scratch_shapes=[pltpu.VMEM((RW_TM, D_MODEL), f32)],
        compiler_params=_cparams("parallel"),
        name="rwkv_pre",
    )(x2, x2, g, mu8, wr, wk, wv, w1, w2, a1, a2, g1, g2, w0, a0, k_k, k_a, r_k, tri, hsum)


RA_ROWS = 256


def _inv_unit_lower(n):
    eye = (lax.broadcasted_iota(jnp.int32, n.shape, 0)
           == lax.broadcasted_iota(jnp.int32, n.shape, 1)).astype(f32)
    x = eye + n
    p = n
    steps = (CHUNK - 1).bit_length() - 1
    for _ in range(steps):
        p = _bdot(p, p)
        x = x + _bdot(x, p)
    return x


def _rwkv_chunk_kernel(at_ref, bt_ref, kt_ref, rt_ref, v_ref, gc_ref,
                       q_ref, yl_ref, m_ref, g_ref):
    n = RA_ROWS
    ri = lax.broadcasted_iota(jnp.int32, (n, n), 0)
    ci = lax.broadcasted_iota(jnp.int32, (n, n), 1)
    same = (ri // CHUNK) == (ci // CHUNK)
    strict = same & (ci < ri)
    incl = same & (ci <= ri)
    lane = lax.broadcasted_iota(jnp.int32, (1, LANES), 1)
    low = lane < RWKV_HEAD

    at = at_ref[...]
    bt = bt_ref[...]
    kt = kt_ref[...]
    rt = rt_ref[...]
    v = v_ref[...]
    ar = jnp.concatenate([at, rt], axis=0)
    bk = jnp.concatenate([bt, kt], axis=0).astype(bf16)
    vb = v.astype(bf16)

    per_head = []
    for head_lanes in (low, jnp.logical_not(low)):
        g = lax.dot_general(jnp.where(head_lanes, ar, 0.0).astype(bf16), bk, NT,
                            preferred_element_type=f32)
        aab = jnp.where(strict, g[:n, :n], 0.0)
        aak = jnp.where(strict, g[:n, n:], 0.0)
        arb = jnp.where(incl, g[n:, :n], 0.0)
        ark = jnp.where(incl, g[n:, n:], 0.0)
        t = _inv_unit_lower(aab)
        tw = _bdot(t, jnp.concatenate([at, _bdot(aak, vb)], axis=1))
        qy = _bdot(arb, tw)
        per_head.append((tw[:, :LANES], tw[:, LANES:], rt + qy[:, :LANES],
                         qy[:, LANES:] + _bdot(ark, vb)))
    w_hat, u_loc, q, yl = (jnp.where(low, a, b) for a, b in zip(*per_head))
    q_ref[...] = q
    yl_ref[...] = yl

    rk = lax.broadcasted_iota(jnp.int32, (RWKV_HEAD, LANES), 0)
    diag = rk == (lax.broadcasted_iota(jnp.int32, (RWKV_HEAD, LANES), 1) % RWKV_HEAD)
    for c in range(n // CHUNK):
        rows = slice(c * CHUNK, (c + 1) * CHUNK)
        gc = gc_ref[c * 8:c * 8 + 1, :]
        bh = bt[rows] * gc
        kh = kt[rows] * gc
        pm = _bdot(w_hat[rows], bh, TN)
        pg = _bdot(jnp.concatenate([u_loc[rows], v[rows]], axis=0),
                   jnp.concatenate([bh, kh], axis=0), TN)
        m_ref[rows, :] = (jnp.where(low, pm[:RWKV_HEAD], pm[RWKV_HEAD:])
                          + jnp.where(diag, gc, 0.0))
        g_ref[rows, :] = jnp.where(low, pg[:RWKV_HEAD], pg[RWKV_HEAD:])


def _rwkv_chunk(at, bt, kt, rt, v, gc):
    m = at.shape[0]
    blk = pl.BlockSpec((RA_ROWS, LANES), lambda i, p: (i, p))
    gblk = pl.BlockSpec((RA_ROWS // CHUNK * 8, LANES), lambda i, p: (i, p))
    act = jax.ShapeDtypeStruct((m, D_MODEL), f32)
    return pl.pallas_call(
        _rwkv_chunk_kernel,
        grid=(m // RA_ROWS, D_MODEL // LANES),
        in_specs=[blk, blk, blk, blk, blk, gblk],
        out_specs=[blk, blk, blk, blk],
        out_shape=[act, act, act, act],
        compiler_params=_cparams("parallel", "parallel"),
        name="rwkv_chunk",
    )(at, bt, kt, rt, v, gc)


SCAN_LANES = 512


def _rwkv_scan_kernel(q_ref, yl_ref, m_ref, g_ref, y_ref, *, seq):
    pairs = SCAN_LANES // LANES
    low = lax.broadcasted_iota(jnp.int32, (1, LANES), 1) < RWKV_HEAD

    def blockdiag(x):
        return jnp.concatenate([jnp.where(low, x, 0.0), jnp.where(low, 0.0, x)], axis=0)

    def body(c, states):
        rows = pl.ds(pl.multiple_of(c * CHUNK, CHUNK), CHUNK)
        new_states = []
        for p in range(pairs):
            cols = slice(p * LANES, (p + 1) * LANES)
            s = states[p]
            y_ref[rows, cols] = yl_ref[rows, cols] + _bdot(q_ref[rows, cols], s, NT)
            mc = blockdiag(m_ref[rows, cols])
            s_hi = s.astype(bf16)
            s_lo = (s - s_hi.astype(f32)).astype(bf16)
            m_hi = mc.astype(bf16)
            m_lo = (mc - m_hi.astype(f32)).astype(bf16)
            sm = (jnp.dot(s_hi, m_hi, preferred_element_type=f32)
                  + jnp.dot(s_lo, m_hi, preferred_element_type=f32)
                  + jnp.dot(s_hi, m_lo, preferred_element_type=f32))
            new_states.append(sm + blockdiag(g_ref[rows, cols]))
        return tuple(new_states)

    init = tuple(jnp.zeros((LANES, LANES), f32) for _ in range(pairs))
    lax.fori_loop(0, seq // CHUNK, body, init)


def _rwkv_scan(q, yl, mm, gg, batch, seq):
    blk = pl.BlockSpec((seq, SCAN_LANES), lambda b, p: (b, p))
    return pl.pallas_call(
        functools.partial(_rwkv_scan_kernel, seq=seq),
        grid=(batch, D_MODEL // SCAN_LANES),
        in_specs=[blk, blk, blk, blk],
        out_specs=blk,
        out_shape=jax.ShapeDtypeStruct((batch * seq, D_MODEL), f32),
        compiler_params=_cparams("parallel", "parallel"),
        name="rwkv_scan",
    )(q, yl, mm, gg)


def _rwkv_post_kernel(y_ref, bonus_ref, gate_ref, x_ref, lw_ref, lb_ref, hsum_ref, wo_ref, o_ref):
    y = y_ref[...]
    hsum = hsum_ref[...]
    mean = _split_dot(y, hsum, 2) * (1.0 / RWKV_HEAD)
    yc = y - mean
    var = _split_dot(yc * yc, hsum, 2) * (1.0 / RWKV_HEAD)
    yn = yc * lax.rsqrt(var + LNX_EPS) * lw_ref[...] + lb_ref[...]
    out = (yn + bonus_ref[...]) * gate_ref[...]
    o_ref[...] = x_ref[...] + jnp.dot(out.astype(bf16), wo_ref[...], preferred_element_type=f32)


def _rwkv_post(y, bonus, gate, x2, lw, lb, hsum, wo):
    m = y.shape[0]
    row = pl.BlockSpec((RW_TM, D_MODEL), lambda i: (i, 0))
    vec = _const_spec((1, D_MODEL))
    return pl.pallas_call(
        _rwkv_post_kernel,
        grid=(m // RW_TM,),
        in_specs=[row, row, row, row, vec, vec, _const_spec(hsum.shape), _const_spec(wo.shape)],
        out_specs=row,
        out_shape=jax.ShapeDtypeStruct((m, D_MODEL), f32),
        compiler_params=_cparams("parallel"),
        name="rwkv_post",
    )(y, bonus, gate, x2, lw, lb, hsum, wo)


def _dup_heads(w, n_heads):
    w = w.reshape(w.shape[0], n_heads, 1, HEAD_DIM)
    return jnp.broadcast_to(w, (w.shape[0], n_heads, 2, HEAD_DIM)).reshape(w.shape[0], n_heads * LANES)


def _pad_cols(w, n):
    return jnp.pad(w, ((0, 0), (0, n - w.shape[1])))


def _pad_rows(w, n):
    return jnp.pad(w, ((0, n - w.shape[0]), (0, 0)))


def _rope_tables(seq):
    inv = 1.0 / (ROPE_THETA ** (jnp.arange(0, HEAD_DIM, 2, dtype=f32) / HEAD_DIM))
    ang = jnp.arange(seq, dtype=f32)[:, None] * inv[None, :]
    cos, sin = jnp.cos(ang), jnp.sin(ang)
    return jnp.tile(jnp.concatenate([cos, cos], 1), (1, 2)), jnp.tile(jnp.concatenate([-sin, sin], 1), (1, 2))


def _dsa_layer(x2, g, w_in, w_o, batch, seq):
    o0 = ATT_HEADS * HEAD_DIM
    o1 = o0 + ATT_KV_HEADS * HEAD_DIM
    o2 = o1 + ATT_KV_HEADS * HEAD_DIM
    o3 = o2 + IDX_HEADS * HEAD_DIM
    o4 = o3 + HEAD_DIM
    w_in = w_in.astype(bf16)
    wq = w_in[:, :o0]
    wk2 = _dup_heads(w_in[:, o0:o1], ATT_KV_HEADS)
    wv = w_in[:, o1:o2].T
    wqi = w_in[:, o2:o3]
    wki2 = _dup_heads(w_in[:, o3:o4], 1)
    wwi = _pad_cols(w_in[:, o4:], LANES).T
    cos, sin = _rope_tables(seq)
    q, ke, ko, vt, qi, kie, kio, wit = _dsa_proj(x2, g, wq, wk2, wv, wqi, wki2, wwi, cos, sin,
                                                 batch, seq)
    o = _dsa_attn(q, qi, wit, kie, kio, ke, ko, vt, batch, seq)
    return _resproj(o, w_o.astype(bf16), x2)


def _rwkv_layer(x2, g, mu, w_rkv, w0, w1, w2, a0, a1, a2, g1, g2, k_k, k_a, r_k, lnx_w, lnx_b, w_o,
                batch, seq):
    vec = lambda p: p.reshape(1, D_MODEL)
    w_rkv = w_rkv.astype(bf16)
    ri = jnp.arange(RW_TM)
    tri = ((ri[:, None] >= ri[None, :]) & (ri[:, None] // CHUNK == ri[None, :] // CHUNK)).astype(bf16)
    di = jnp.arange(D_MODEL)
    hsum = (di[:, None] // RWKV_HEAD == di[None, :] // RWKV_HEAD).astype(bf16)
    at, bt, kt, rt, v, gc, bonus, gate = _rwkv_pre(
        x2, g, _pad_rows(mu, 8), w_rkv[0], w_rkv[1], w_rkv[2],
        _pad_cols(w1, LORA_PAD).astype(bf16), _pad_rows(w2, LORA_PAD).astype(bf16),
        _pad_cols(a1, LORA_PAD).astype(bf16), _pad_rows(a2, LORA_PAD).astype(bf16),
        _pad_cols(g1, GATE_PAD).astype(bf16), _pad_rows(g2, GATE_PAD).astype(bf16),
        vec(w0), vec(a0), vec(k_k), vec(k_a), vec(r_k), tri, hsum, seq)
    q, yl, mm, gg = _rwkv_chunk(at, bt, kt, rt, v, gc)
    y = _rwkv_scan(q, yl, mm, gg, batch, seq)
    return _rwkv_post(y, bonus, gate, x2, vec(lnx_w), vec(lnx_b), hsum, w_o.astype(bf16))


def kernel(x, mixer_norm, mlp_norm, mlp_w_up, mlp_w_down, final_norm, dsa_w_in, dsa_w_o, rwkv_mu, rwkv_w_rkv, rwkv_w0, rwkv_w1, rwkv_w2, rwkv_a0, rwkv_a1, rwkv_a2, rwkv_g1, rwkv_g2, rwkv_k_k, rwkv_k_a, rwkv_r_k, rwkv_lnx_w, rwkv_lnx_b, rwkv_w_o):
    batch, seq, _ = x.shape
    depth = mixer_norm.shape[0]
    x2 = x.reshape(batch * seq, D_MODEL)
    fg = final_norm.reshape(1, D_MODEL)
    for i in range(depth):
        g = mixer_norm[i].reshape(1, D_MODEL)
        j = i // 2
        if i % 2 == 0:
            x2 = _dsa_layer(x2, g, dsa_w_in[j], dsa_w_o[j], batch, seq)
        else:
            x2 = _rwkv_layer(x2, g, rwkv_mu[j], rwkv_w_rkv[j], rwkv_w0[j], rwkv_w1[j], rwkv_w2[j],
                             rwkv_a0[j], rwkv_a1[j], rwkv_a2[j], rwkv_g1[j], rwkv_g2[j],
                             rwkv_k_k[j], rwkv_k_a[j], rwkv_r_k[j], rwkv_lnx_w[j], rwkv_lnx_b[j],
                             rwkv_w_o[j], batch, seq)
        x2 = _mlp(x2, mlp_norm[i].reshape(1, D_MODEL), mlp_w_up[i].astype(bf16),
                  mlp_w_down[i].astype(bf16), fg, final=(i == depth - 1))
    return x2.reshape(batch, seq, D_MODEL)
```

```python
import functools

import jax
import jax.numpy as jnp
from jax import lax
from jax.experimental import pallas as pl
from jax.experimental.pallas import tpu as pltpu

f32 = jnp.float32
bf16 = jnp.bfloat16

D_MODEL = 1024
D_FF = 4 * D_MODEL
NORM_EPS = 1e-6

ATT_HEADS = 16
ATT_KV_HEADS = 4
HEAD_DIM = 64
IDX_HEADS = 8
TOPK_MAX = 256
ROPE_THETA = 10000.0

RWKV_HEAD = 64
RWKV_HEADS = 16
LNX_EPS = 64e-5
CHUNK = 64

LANES = 128
VMEM_LIMIT = 56 * 1024 * 1024

NT = (((1,), (1,)), ((), ()))
TN = (((0,), (0,)), ((), ()))


def _cparams(*sem):
    return pltpu.CompilerParams(dimension_semantics=sem, vmem_limit_bytes=VMEM_LIMIT)


def _rms(x, g):
    return x * lax.rsqrt(jnp.mean(x * x, axis=-1, keepdims=True) + NORM_EPS) * g


def _bdot(a, b, dims=None):
    a = a.astype(bf16)
    b = b.astype(bf16)
    if dims is None:
        return jnp.dot(a, b, preferred_element_type=f32)
    return lax.dot_general(a, b, dims, preferred_element_type=f32)


def _split_dot(a, b, parts):
    out = None
    rem = a
    for p in range(parts):
        piece = rem.astype(bf16)
        term = jnp.dot(piece, b, preferred_element_type=f32)
        out = term if out is None else out + term
        if p + 1 < parts:
            rem = rem - piece.astype(f32)
    return out


def _const_spec(shape):
    return pl.BlockSpec(shape, lambda *_: (0,) * len(shape))


MLP_TM = 512
MLP_FCH = 512


def _mlp_kernel(x_ref, g_ref, wu_ref, wd_ref, fg_ref, o_ref, *, final):
    x = x_ref[...]
    xn = _rms(x, g_ref[...]).astype(bf16)
    o_ref[...] = x
    for f in range(0, D_FF, MLP_FCH):
        u = jnp.dot(xn, wu_ref[:, f:f + MLP_FCH], preferred_element_type=f32)
        u = jnp.maximum(u, 0.0)
        o_ref[...] += jnp.dot((u * u).astype(bf16), wd_ref[f:f + MLP_FCH, :],
                              preferred_element_type=f32)
    if final:
        o_ref[...] = _rms(o_ref[...], fg_ref[...])


def _mlp(x2, g, wu, wd, fg, final):
    m = x2.shape[0]
    return pl.pallas_call(
        functools.partial(_mlp_kernel, final=final),
        grid=(m // MLP_TM,),
        in_specs=[pl.BlockSpec((MLP_TM, D_MODEL), lambda i: (i, 0)),
                  _const_spec((1, D_MODEL)),
                  _const_spec((D_MODEL, D_FF)),
                  _const_spec((D_FF, D_MODEL)),
                  _const_spec((1, D_MODEL))],
        out_specs=pl.BlockSpec((MLP_TM, D_MODEL), lambda i: (i, 0)),
        out_shape=jax.ShapeDtypeStruct((m, D_MODEL), f32),
        compiler_params=_cparams("parallel"),
        name="mlp",
    )(x2, g, wu, wd, fg)


PROJ_TM = 512


def _resproj_kernel(a_ref, w_ref, r_ref, o_ref):
    o_ref[...] = r_ref[...] + jnp.dot(a_ref[...], w_ref[...], preferred_element_type=f32)


def _resproj(a, w, res):
    m, k = a.shape
    n = w.shape[1]
    return pl.pallas_call(
        _resproj_kernel,
        grid=(m // PROJ_TM,),
        in_specs=[pl.BlockSpec((PROJ_TM, k), lambda i: (i, 0)),
                  _const_spec((k, n)),
                  pl.BlockSpec((PROJ_TM, n), lambda i: (i, 0))],
        out_specs=pl.BlockSpec((PROJ_TM, n), lambda i: (i, 0)),
        out_shape=jax.ShapeDtypeStruct((m, n), f32),
        compiler_params=_cparams("parallel"),
        name="resproj",
    )(a, w, res)


DSA_TM = 256


def _rope(x, cos, sin_signed, first_half):
    fwd = pltpu.roll(x, 32, 1)
    bwd = pltpu.roll(x, 96, 1)
    return x * cos + jnp.where(first_half, bwd, fwd) * sin_signed


LOG2E = 1.4426950408889634


VT_ROWS = 80


def _dsa_proj_kernel(x_ref, g_ref, wq_ref, wk_ref, wv_ref, wqi_ref, wki_ref, wwi_ref,
                     cos_ref, sin_ref,
                     q_ref, k_ref, vt_ref, qi_ref, ki_ref, wit_ref):
    h = _rms(x_ref[...], g_ref[...]).astype(bf16)
    tm = h.shape[0]
    cos = cos_ref[...]
    sin = sin_ref[...]
    lane = lax.broadcasted_iota(jnp.int32, (1, LANES), 1)
    first_half = (lane % HEAD_DIM) < (HEAD_DIM // 2)
    low = lane < HEAD_DIM

    def per_head(x, out_ref, pair, scale):
        xc = _rope(x[:, pair * LANES:(pair + 1) * LANES], cos, sin, first_half) * scale
        swapped = pltpu.roll(xc, HEAD_DIM, 1)
        out_ref[:, (2 * pair) * LANES:(2 * pair + 1) * LANES] = jnp.where(low, xc, 0.0).astype(bf16)
        out_ref[:, (2 * pair + 1) * LANES:(2 * pair + 2) * LANES] = jnp.where(low, swapped, 0.0).astype(bf16)

    q = jnp.dot(h, wq_ref[...], preferred_element_type=f32)
    for pair in range(ATT_HEADS // 2):
        per_head(q, q_ref, pair, HEAD_DIM ** -0.5 * LOG2E)
    qi = jnp.dot(h, wqi_ref[...], preferred_element_type=f32)
    for pair in range(IDX_HEADS // 2):
        per_head(qi, qi_ref, pair, 1.0)

    k2 = jnp.dot(h, wk_ref[...], preferred_element_type=f32)
    for c in range(0, ATT_KV_HEADS * LANES, LANES):
        kc = _rope(k2[:, c:c + LANES], cos, sin, first_half)
        k_ref[0, :, c:c + LANES] = jnp.where(low, kc, 0.0).astype(bf16)
    ki2 = _rope(jnp.dot(h, wki_ref[...], preferred_element_type=f32), cos, sin, first_half)
    ki_ref[0] = jnp.where(low, ki2, 0.0).astype(bf16)

    vt = lax.dot_general(wv_ref[...], h, NT, preferred_element_type=f32)
    extra = (lax.broadcasted_iota(jnp.int32, (VT_ROWS - HEAD_DIM, tm), 0) == 0).astype(f32)
    for g in range(ATT_KV_HEADS):
        vt_ref[0, g * VT_ROWS:(g + 1) * VT_ROWS, :] = jnp.concatenate(
            [vt[g * HEAD_DIM:(g + 1) * HEAD_DIM], extra], axis=0).astype(bf16)
    wit_ref[...] = (lax.dot_general(wwi_ref[...], h, NT, preferred_element_type=f32)
                    * (IDX_HEADS ** -0.5 * HEAD_DIM ** -0.5))


def _dsa_proj(x2, g, wq, wk2, wv, wqi, wki2, wwi, cos, sin, batch, seq):
    m = x2.shape[0]
    nblk = seq // DSA_TM
    row = lambda n: pl.BlockSpec((DSA_TM, n), lambda b, i: (b * nblk + i, 0))
    seqrow = lambda n: pl.BlockSpec((1, DSA_TM, n), lambda b, i: (b, i, 0))
    tab = pl.BlockSpec((DSA_TM, LANES), lambda b, i: (i, 0))
    kvw = ATT_KV_HEADS * LANES
    vtr = ATT_KV_HEADS * VT_ROWS
    return pl.pallas_call(
        _dsa_proj_kernel,
        grid=(batch, nblk),
        in_specs=[row(D_MODEL), _const_spec((1, D_MODEL)),
                  _const_spec(wq.shape), _const_spec(wk2.shape), _const_spec(wv.shape),
                  _const_spec(wqi.shape), _const_spec(wki2.shape), _const_spec(wwi.shape),
                  tab, tab],
        out_specs=[row(ATT_HEADS * LANES), seqrow(kvw),
                   pl.BlockSpec((1, vtr, DSA_TM), lambda b, i: (b, 0, i)),
                   row(IDX_HEADS * LANES), seqrow(LANES),
                   pl.BlockSpec((LANES, DSA_TM), lambda b, i: (0, b * nblk + i))],
        out_shape=[jax.ShapeDtypeStruct((m, ATT_HEADS * LANES), bf16),
                   jax.ShapeDtypeStruct((batch, seq, kvw), bf16),
                   jax.ShapeDtypeStruct((batch, vtr, seq), bf16),
                   jax.ShapeDtypeStruct((m, IDX_HEADS * LANES), bf16),
                   jax.ShapeDtypeStruct((batch, seq, LANES), bf16),
                   jax.ShapeDtypeStruct((LANES, m), f32)],
        compiler_params=_cparams("parallel", "parallel"),
        name="dsa_proj",
    )(x2, g, wq, wk2, wv, wqi, wki2, wwi, cos, sin)


DSA_TQ = 128
DSA_CLS = 512
IDX_GROUP = 4
INT_MIN = -2 ** 31
NEG_INF_KEY = -2139095041


def _fold_rows(x, op):
    while x.shape[0] % 16 == 0:
        h = x.shape[0] // 2
        x = op(x[:h], x[h:])
    return x


def _col_count(mask):
    return jnp.sum(_fold_rows(jnp.where(mask, 1.0, 0.0), jnp.add), axis=0, keepdims=True).astype(jnp.int32)


def _stack_heads(ref, first, count):
    return jnp.concatenate([ref[:, (first + j) * LANES:(first + j + 1) * LANES] for j in range(count)],
                           axis=0)


def _dsa_attn_body(q_ref, qi_ref, wit_ref, ki_ref, k_ref, vt_ref, o_ref,
                   pos_ref, *, width, top_k, t0):
    col_t = t0 + lax.broadcasted_iota(jnp.int32, (1, DSA_TQ), 1)
    row_s = lax.broadcasted_iota(jnp.int32, (width, 1), 0)
    causal = row_s <= col_t

    wit = wit_ref[...]
    ki = ki_ref[0, 0:width, :]
    score = jnp.zeros((width, DSA_TQ), f32)
    for first in range(0, IDX_HEADS, IDX_GROUP):
        rel = lax.dot_general(ki, _stack_heads(qi_ref, first, IDX_GROUP), NT,
                              preferred_element_type=f32)
        for j in range(IDX_GROUP):
            hd = first + j
            score = score + wit[hd:hd + 1, :] * jnp.maximum(rel[:, j * DSA_TQ:(j + 1) * DSA_TQ], 0.0)
    score = jnp.where(causal, score, -jnp.inf)

    bits = lax.bitcast_convert_type(score, jnp.int32)
    key = bits ^ ((bits >> 31) & jnp.int32(0x7FFFFFFF))

    thr = jnp.where(_col_count(key >= 0) >= top_k, jnp.int32(0), jnp.int32(INT_MIN))

    def value_step(i, thr):
        cand = thr | (jnp.int32(1) << (30 - i))
        return jnp.where(_col_count(key >= cand) >= top_k, cand, thr)

    thr = lax.fori_loop(0, 31, value_step, thr)

    above = key > thr
    tied = key == thr
    n_above = _col_count(above)
    overfull = ((n_above + _col_count(tied)) != top_k) & (thr != NEG_INF_KEY)
    pos_ref[...] = jnp.full(pos_ref.shape, width, jnp.int32)

    @pl.when(jnp.max(overfull.astype(f32)) > 0.0)
    def _():
        need = top_k - n_above
        nbits = (width - 1).bit_length()

        def index_step(i, pos):
            cand = pos | (jnp.int32(1) << (nbits - 1 - i))
            return jnp.where(_col_count(tied & (row_s < cand)) < need, cand, pos)

        pos = lax.fori_loop(0, nbits, index_step, jnp.zeros((1, DSA_TQ), jnp.int32))
        pos_ref[...] = jnp.broadcast_to(pos, pos_ref.shape)

    keep = (above | (tied & (row_s <= pos_ref[0:1, :]))) & causal
    bias = jnp.where(keep, 0.0, -jnp.inf)

    group = ATT_HEADS // ATT_KV_HEADS
    for g in range(ATT_KV_HEADS):
        logits4 = lax.dot_general(k_ref[0, 0:width, g * LANES:(g + 1) * LANES],
                                  _stack_heads(q_ref, g * group, group), NT,
                                  preferred_element_type=f32)
        vt = vt_ref[0, g * VT_ROWS:(g + 1) * VT_ROWS, 0:width]
        outs = []
        for j in range(group):
            logits = logits4[:, j * DSA_TQ:(j + 1) * DSA_TQ] + bias
            mx = jnp.max(_fold_rows(logits, jnp.maximum), axis=0, keepdims=True)
            p = jnp.exp2(logits - mx).astype(bf16)
            pv = jnp.dot(vt, p, preferred_element_type=f32)
            outs.append(pv[:HEAD_DIM] * (1.0 / pv[HEAD_DIM:HEAD_DIM + 1]))
        for pair in range(group // 2):
            both = jnp.concatenate(outs[2 * pair:2 * pair + 2], axis=0).T
            col = (g * group // 2 + pair) * LANES
            o_ref[:, col:col + LANES] = both.astype(bf16)


def _dsa_attn_kernel(q_ref, qi_ref, wit_ref, ki_ref, k_ref, vt_ref, o_ref,
                     pos_ref, *, seq, top_k):
    i = pl.program_id(1)
    blocks_per_class = DSA_CLS // DSA_TQ
    for cls in range(seq // DSA_CLS):
        @pl.when(i // blocks_per_class == cls)
        def _(cls=cls):
            _dsa_attn_body(q_ref, qi_ref, wit_ref, ki_ref, k_ref, vt_ref, o_ref,
                           pos_ref, width=DSA_CLS * (cls + 1), top_k=top_k, t0=i * DSA_TQ)


def _dsa_attn(q, qi, wit, ki, k, vt, batch, seq):
    nq = seq // DSA_TQ
    top_k = min(TOPK_MAX, seq // 4)
    qrow = lambda n: pl.BlockSpec((DSA_TQ, n), lambda b, i: (b * nq + i, 0))
    full = lambda n: pl.BlockSpec((1, seq, n), lambda b, i: (b, 0, 0))
    return pl.pallas_call(
        functools.partial(_dsa_attn_kernel, seq=seq, top_k=top_k),
        grid=(batch, nq),
        in_specs=[qrow(ATT_HEADS * LANES), qrow(IDX_HEADS * LANES),
                  pl.BlockSpec((8, DSA_TQ), lambda b, i: (0, b * nq + i)),
                  full(LANES), full(ATT_KV_HEADS * LANES),
                  pl.BlockSpec((1, ATT_KV_HEADS * VT_ROWS, seq), lambda b, i: (b, 0, 0))],
        out_specs=qrow(ATT_HEADS * HEAD_DIM),
        out_shape=jax.ShapeDtypeStruct((batch * seq, ATT_HEADS * HEAD_DIM), bf16),
        scratch_shapes=[pltpu.VMEM((8, DSA_TQ), jnp.int32)],
        compiler_params=_cparams("parallel", "arbitrary"),
        name="dsa_attn",
    )(q, qi, wit, ki, k, vt)


RW_TM = 256
LORA_PAD = 128
GATE_PAD = 256


def _rwkv_pre_kernel(x_ref, xp_ref, g_ref, mu_ref, wr_ref, wk_ref, wv_ref,
                     w1_ref, w2_ref, a1_ref, a2_ref, g1_ref, g2_ref,
                     w0_ref, a0_ref, kk_ref, ka_ref, rk_ref, tri_ref, hsum_ref,
                     at_ref, bt_ref, kt_ref, rt_ref, v_ref, gc_ref, bonus_ref, gate_ref,
                     lc_scr, *, seq):
    i = pl.program_id(0)
    g = g_ref[...]
    h = _rms(x_ref[...], g)
    hp = _rms(xp_ref[...], g)[7:8, :]
    hp = jnp.where((i * RW_TM) % seq == 0, 0.0, hp)
    rowi = lax.broadcasted_iota(jnp.int32, (RW_TM, 1), 0)
    hs = jnp.where(rowi == 0, hp, pltpu.roll(h, 1, 0))
    xx = hs - h
    mu = mu_ref[...]

    def mix(c):
        return (h + xx * mu[c:c + 1, :]).astype(bf16)

    r = jnp.dot(mix(0), wr_ref[...], preferred_element_type=f32)
    k = jnp.dot(mix(1), wk_ref[...], preferred_element_type=f32)
    v = jnp.dot(mix(2), wv_ref[...], preferred_element_type=f32)

    wl = w0_ref[...] + _bdot(jnp.tanh(jnp.dot(mix(3), w1_ref[...], preferred_element_type=f32)),
                             w2_ref[...])
    nwl = -wl
    softplus = jnp.maximum(nwl, 0.0) + jnp.log1p(jnp.exp(-jnp.abs(nwl)))
    ld = -jnp.exp(-softplus - 0.5)
    a = jax.nn.sigmoid(a0_ref[...] + _bdot(jnp.dot(mix(4), a1_ref[...],
                                                   preferred_element_type=f32), a2_ref[...]))
    gate_ref[...] = _bdot(jax.nn.sigmoid(jnp.dot(mix(5), g1_ref[...],
                                                 preferred_element_type=f32)), g2_ref[...])

    hsum = hsum_ref[...]
    z = k * kk_ref[...]
    kk = z * lax.rsqrt(jnp.maximum(_split_dot(z * z, hsum, 2), 1e-24))
    k2 = k * (1.0 + (a - 1.0) * ka_ref[...])
    bonus_ref[...] = _split_dot(r * k2 * rk_ref[...], hsum, 2) * v

    lc = _split_dot_left(tri_ref[...], ld, 3)
    lc_scr[...] = lc
    for c in range(RW_TM // CHUNK):
        last = lc_scr[c * CHUNK + CHUNK - 1:c * CHUNK + CHUNK, :]
        gc_ref[c * 8:(c + 1) * 8, :] = jnp.broadcast_to(jnp.exp(last), (8, D_MODEL))
    einv = jnp.exp(-lc)
    at_ref[...] = -kk * jnp.exp(lc - ld)
    bt_ref[...] = kk * a * einv
    kt_ref[...] = k2 * einv
    rt_ref[...] = r * jnp.exp(lc)
    v_ref[...] = v


def _split_dot_left(a, b, parts):
    out = None
    rem = b
    for p in range(parts):
        piece = rem.astype(bf16)
        term = jnp.dot(a, piece, preferred_element_type=f32)
        out = term if out is None else out + term
        if p + 1 < parts:
            rem = rem - piece.astype(f32)
    return out


def _rwkv_pre(x2, g, mu8, wr, wk, wv, w1, w2, a1, a2, g1, g2, w0, a0, k_k, k_a, r_k, tri, hsum, seq):
    m = x2.shape[0]
    row = pl.BlockSpec((RW_TM, D_MODEL), lambda i: (i, 0))
    prev = pl.BlockSpec((8, D_MODEL), lambda i: (jnp.maximum(i * (RW_TM // 8) - 1, 0), 0))
    vec = _const_spec((1, D_MODEL))
    gcrow = pl.BlockSpec((RW_TM // CHUNK * 8, D_MODEL), lambda i: (i, 0))
    act = jax.ShapeDtypeStruct((m, D_MODEL), f32)
    return pl.pallas_call(
        functools.partial(_rwkv_pre_kernel, seq=seq),
        grid=(m // RW_TM,),
        in_specs=[row, prev, vec, _const_spec((8, D_MODEL)),
                  _const_spec(wr.shape), _const_spec(wk.shape), _const_spec(wv.shape),
                  _const_spec(w1.shape), _const_spec(w2.shape),
                  _const_spec(a1.shape), _const_spec(a2.shape),
                  _const_spec(g1.shape), _const_spec(g2.shape),
                  vec, vec, vec, vec, vec,
                  _const_spec(tri.shape), _const_spec(hsum.shape)],
        out_specs=[row, row, row, row, row, gcrow, row, row],
        out_shape=[act, act, act, act, act,
                   jax.ShapeDtypeStruct((m // CHUNK * 8, D_MODEL), f32), act, act],
        scratch_shapes=[pltpu.VMEM((RW_TM, D_MODEL), f32)],
        compiler_params=_cparams("parallel"),
        name="rwkv_pre",
    )(x2, x2, g, mu8, wr, wk, wv, w1, w2, a1, a2, g1, g2, w0, a0, k_k, k_a, r_k, tri, hsum)


RA_GROUP = 128
RA_ROWS = 1024


def _rwkv_chunk_kernel(at_ref, bt_ref, kt_ref, rt_ref, v_ref, gc_ref,
                       q_ref, yl_ref, m_ref, g_ref):
    n = RA_GROUP
    ri = lax.broadcasted_iota(jnp.int32, (n, n), 0)
    ci = lax.broadcasted_iota(jnp.int32, (n, n), 1)
    same = (ri // CHUNK) == (ci // CHUNK)
    strict = same & (ci < ri)
    incl = same & (ci <= ri)
    eye = (ri == ci).astype(f32)
    lane = lax.broadcasted_iota(jnp.int32, (1, LANES), 1)
    low = lane < RWKV_HEAD
    rk = lax.broadcasted_iota(jnp.int32, (RWKV_HEAD, LANES), 0)
    diag = rk == (lax.broadcasted_iota(jnp.int32, (RWKV_HEAD, LANES), 1) % RWKV_HEAD)

    groups = []
    for gi in range(RA_ROWS // n):
        rows = slice(gi * n, (gi + 1) * n)
        at, bt, kt, rt, v = (r[rows, :] for r in (at_ref, bt_ref, kt_ref, rt_ref, v_ref))
        groups.append(dict(at=at, bt=bt, kt=kt, rt=rt, v=v, vb=v.astype(bf16),
                           ar=jnp.concatenate([at, rt], axis=0),
                           bk=jnp.concatenate([bt, kt], axis=0).astype(bf16)))
    probs = [(gr, hl) for gr in groups for hl in (low, jnp.logical_not(low))]

    gs = [lax.dot_general(jnp.where(hl, gr["ar"], 0.0).astype(bf16), gr["bk"], NT,
                          preferred_element_type=f32) for gr, hl in probs]
    aab = [jnp.where(strict, g[:n, :n], 0.0) for g in gs]
    aak = [jnp.where(strict, g[:n, n:], 0.0) for g in gs]
    arb = [jnp.where(incl, g[n:, :n], 0.0) for g in gs]
    ark = [jnp.where(incl, g[n:, n:], 0.0) for g in gs]

    ts = [eye + a for a in aab]
    ps = aab
    for _ in range((CHUNK - 1).bit_length() - 1):
        ps = [_bdot(p, p) for p in ps]
        ts = [t + _bdot(t, p) for t, p in zip(ts, ps)]

    akv = [_bdot(a, gr["vb"]) for a, (gr, _) in zip(aak, probs)]
    tw = [_bdot(t, jnp.concatenate([gr["at"], x], axis=1))
          for t, x, (gr, _) in zip(ts, akv, probs)]
    qy = [_bdot(a, w) for a, w in zip(arb, tw)]
    rkv = [_bdot(a, gr["vb"]) for a, (gr, _) in zip(ark, probs)]

    for gi, gr in enumerate(groups):
        lo, hi = 2 * gi, 2 * gi + 1
        w_hat = jnp.where(low, tw[lo][:, :LANES], tw[hi][:, :LANES])
        u_loc = jnp.where(low, tw[lo][:, LANES:], tw[hi][:, LANES:])
        base = gi * n
        q_ref[base:base + n, :] = gr["rt"] + jnp.where(low, qy[lo][:, :LANES], qy[hi][:, :LANES])
        yl_ref[base:base + n, :] = (jnp.where(low, qy[lo][:, LANES:], qy[hi][:, LANES:])
                                    + jnp.where(low, rkv[lo], rkv[hi]))
        for c in range(n // CHUNK):
            rows = slice(c * CHUNK, (c + 1) * CHUNK)
            ch = base // CHUNK + c
            gc = gc_ref[ch * 8:ch * 8 + 1, :]
            bh = gr["bt"][rows] * gc
            kh = gr["kt"][rows] * gc
            pm = _bdot(w_hat[rows], bh, TN)
            pg = _bdot(jnp.concatenate([u_loc[rows], gr["v"][rows]], axis=0),
                       jnp.concatenate([bh, kh], axis=0), TN)
            out = slice(base + c * CHUNK, base + (c + 1) * CHUNK)
            m_ref[out, :] = (jnp.where(low, pm[:RWKV_HEAD], pm[RWKV_HEAD:])
                             + jnp.where(diag, gc, 0.0))
            g_ref[out, :] = jnp.where(low, pg[:RWKV_HEAD], pg[RWKV_HEAD:])


def _rwkv_chunk(at, bt, kt, rt, v, gc):
    m = at.shape[0]
    blk = pl.BlockSpec((RA_ROWS, LANES), lambda i, p: (i, p))
    gblk = pl.BlockSpec((RA_ROWS // CHUNK * 8, LANES), lambda i, p: (i, p))
    act = jax.ShapeDtypeStruct((m, D_MODEL), f32)
    return pl.pallas_call(
        _rwkv_chunk_kernel,
        grid=(m // RA_ROWS, D_MODEL // LANES),
        in_specs=[blk, blk, blk, blk, blk, gblk],
        out_specs=[blk, blk, blk, blk],
        out_shape=[act, act, act, act],
        compiler_params=_cparams("parallel", "parallel"),
        name="rwkv_chunk",
    )(at, bt, kt, rt, v, gc)


SCAN_LANES = 512


def _rwkv_scan_kernel(q_ref, yl_ref, m_ref, g_ref, y_ref, *, seq):
    pairs = SCAN_LANES // LANES
    low = lax.broadcasted_iota(jnp.int32, (1, LANES), 1) < RWKV_HEAD

    def blockdiag(x):
        return jnp.concatenate([jnp.where(low, x, 0.0), jnp.where(low, 0.0, x)], axis=0)

    def body(c, states):
        rows = pl.ds(pl.multiple_of(c * CHUNK, CHUNK), CHUNK)
        new_states = []
        for p in range(pairs):
            cols = slice(p * LANES, (p + 1) * LANES)
            s = states[p]
            y_ref[rows, cols] = yl_ref[rows, cols] + _bdot(q_ref[rows, cols], s, NT)
            mc = blockdiag(m_ref[rows, cols])
            s_hi = s.astype(bf16)
            s_lo = (s - s_hi.astype(f32)).astype(bf16)
            m_hi = mc.astype(bf16)
            m_lo = (mc - m_hi.astype(f32)).astype(bf16)
            sm = (jnp.dot(s_hi, m_hi, preferred_element_type=f32)
                  + jnp.dot(s_lo, m_hi, preferred_element_type=f32)
                  + jnp.dot(s_hi, m_lo, preferred_element_type=f32))
            new_states.append(sm + blockdiag(g_ref[rows, cols]))
        return tuple(new_states)

    init = tuple(jnp.zeros((LANES, LANES), f32) for _ in range(pairs))
    lax.fori_loop(0, seq // CHUNK, body, init)


def _rwkv_scan(q, yl, mm, gg, batch, seq):
    blk = pl.BlockSpec((seq, SCAN_LANES), lambda b, p: (b, p))
    return pl.pallas_call(
        functools.partial(_rwkv_scan_kernel, seq=seq),
        grid=(batch, D_MODEL // SCAN_LANES),
        in_specs=[blk, blk, blk, blk],
        out_specs=blk,
        out_shape=jax.ShapeDtypeStruct((batch * seq, D_MODEL), f32),
        compiler_params=_cparams("parallel", "parallel"),
        name="rwkv_scan",
    )(q, yl, mm, gg)


def _rwkv_post_kernel(y_ref, bonus_ref, gate_ref, x_ref, lw_ref, lb_ref, hsum_ref, wo_ref, o_ref):
    y = y_ref[...]
    hsum = hsum_ref[...]
    mean = _split_dot(y, hsum, 2) * (1.0 / RWKV_HEAD)
    yc = y - mean
    var = _split_dot(yc * yc, hsum, 2) * (1.0 / RWKV_HEAD)
    yn = yc * lax.rsqrt(var + LNX_EPS) * lw_ref[...] + lb_ref[...]
    out = (yn + bonus_ref[...]) * gate_ref[...]
    o_ref[...] = x_ref[...] + jnp.dot(out.astype(bf16), wo_ref[...], preferred_element_type=f32)


def _rwkv_post(y, bonus, gate, x2, lw, lb, hsum, wo):
    m = y.shape[0]
    row = pl.BlockSpec((RW_TM, D_MODEL), lambda i: (i, 0))
    vec = _const_spec((1, D_MODEL))
    return pl.pallas_call(
        _rwkv_post_kernel,
        grid=(m // RW_TM,),
        in_specs=[row, row, row, row, vec, vec, _const_spec(hsum.shape), _const_spec(wo.shape)],
        out_specs=row,
        out_shape=jax.ShapeDtypeStruct((m, D_MODEL), f32),
        compiler_params=_cparams("parallel"),
        name="rwkv_post",
    )(y, bonus, gate, x2, lw, lb, hsum, wo)


def _dup_heads(w, n_heads):
    w = w.reshape(w.shape[0], n_heads, 1, HEAD_DIM)
    return jnp.broadcast_to(w, (w.shape[0], n_heads, 2, HEAD_DIM)).reshape(w.shape[0], n_heads * LANES)


def _pad_cols(w, n):
    return jnp.pad(w, ((0, 0), (0, n - w.shape[1])))


def _pad_rows(w, n):
    return jnp.pad(w, ((0, n - w.shape[0]), (0, 0)))


def _rope_tables(seq):
    inv = 1.0 / (ROPE_THETA ** (jnp.arange(0, HEAD_DIM, 2, dtype=f32) / HEAD_DIM))
    ang = jnp.arange(seq, dtype=f32)[:, None] * inv[None, :]
    cos, sin = jnp.cos(ang), jnp.sin(ang)
    return jnp.tile(jnp.concatenate([cos, cos], 1), (1, 2)), jnp.tile(jnp.concatenate([-sin, sin], 1), (1, 2))


def _dsa_layer(x2, g, w_in, w_o, batch, seq):
    o0 = ATT_HEADS * HEAD_DIM
    o1 = o0 + ATT_KV_HEADS * HEAD_DIM
    o2 = o1 + ATT_KV_HEADS * HEAD_DIM
    o3 = o2 + IDX_HEADS * HEAD_DIM
    o4 = o3 + HEAD_DIM
    w_in = w_in.astype(bf16)
    wq = w_in[:, :o0]
    wk2 = _dup_heads(w_in[:, o0:o1], ATT_KV_HEADS)
    wv = w_in[:, o1:o2].T
    wqi = w_in[:, o2:o3]
    wki2 = _dup_heads(w_in[:, o3:o4], 1)
    wwi = _pad_cols(w_in[:, o4:], LANES).T
    cos, sin = _rope_tables(seq)
    q, k, vt, qi, ki, wit = _dsa_proj(x2, g, wq, wk2, wv, wqi, wki2, wwi, cos, sin, batch, seq)
    o = _dsa_attn(q, qi, wit, ki, k, vt, batch, seq)
    return _resproj(o, w_o.astype(bf16), x2)


def _rwkv_layer(x2, g, mu, w_rkv, w0, w1, w2, a0, a1, a2, g1, g2, k_k, k_a, r_k, lnx_w, lnx_b, w_o,
                batch, seq):
    vec = lambda p: p.reshape(1, D_MODEL)
    w_rkv = w_rkv.astype(bf16)
    ri = jnp.arange(RW_TM)
    tri = ((ri[:, None] >= ri[None, :]) & (ri[:, None] // CHUNK == ri[None, :] // CHUNK)).astype(bf16)
    di = jnp.arange(D_MODEL)
    hsum = (di[:, None] // RWKV_HEAD == di[None, :] // RWKV_HEAD).astype(bf16)
    at, bt, kt, rt, v, gc, bonus, gate = _rwkv_pre(
        x2, g, _pad_rows(mu, 8), w_rkv[0], w_rkv[1], w_rkv[2],
        _pad_cols(w1, LORA_PAD).astype(bf16), _pad_rows(w2, LORA_PAD).astype(bf16),
        _pad_cols(a1, LORA_PAD).astype(bf16), _pad_rows(a2, LORA_PAD).astype(bf16),
        _pad_cols(g1, GATE_PAD).astype(bf16), _pad_rows(g2, GATE_PAD).astype(bf16),
        vec(w0), vec(a0), vec(k_k), vec(k_a), vec(r_k), tri, hsum, seq)
    q, yl, mm, gg = _rwkv_chunk(at, bt, kt, rt, v, gc)
    y = _rwkv_scan(q, yl, mm, gg, batch, seq)
    return _rwkv_post(y, bonus, gate, x2, vec(lnx_w), vec(lnx_b), hsum, w_o.astype(bf16))


def kernel(x, mixer_norm, mlp_norm, mlp_w_up, mlp_w_down, final_norm, dsa_w_in, dsa_w_o, rwkv_mu, rwkv_w_rkv, rwkv_w0, rwkv_w1, rwkv_w2, rwkv_a0, rwkv_a1, rwkv_a2, rwkv_g1, rwkv_g2, rwkv_k_k, rwkv_k_a, rwkv_r_k, rwkv_lnx_w, rwkv_lnx_b, rwkv_w_o):
    batch, seq, _ = x.shape
    depth = mixer_norm.shape[0]
    x2 = x.reshape(batch * seq, D_MODEL)
    fg = final_norm.reshape(1, D_MODEL)
    for i in range(depth):
        g = mixer_norm[i].reshape(1, D_MODEL)
        j = i // 2
        if i % 2 == 0:
            x2 = _dsa_layer(x2, g, dsa_w_in[j], dsa_w_o[j], batch, seq)
        else:
            x2 = _rwkv_layer(x2, g, rwkv_mu[j], rwkv_w_rkv[j], rwkv_w0[j], rwkv_w1[j], rwkv_w2[j],
                             rwkv_a0[j], rwkv_a1[j], rwkv_a2[j], rwkv_g1[j], rwkv_g2[j],
                             rwkv_k_k[j], rwkv_k_a[j], rwkv_r_k[j], rwkv_lnx_w[j], rwkv_lnx_b[j],
                             rwkv_w_o[j], batch, seq)
        x2 = _mlp(x2, mlp_norm[i].reshape(1, D_MODEL), mlp_w_up[i].astype(bf16),
                  mlp_w_down[i].astype(bf16), fg, final=(i == depth - 1))
    return x2.reshape(batch, seq, D_MODEL)
```

```python
import functools

import jax
import jax.numpy as jnp
from jax import lax
from jax.experimental import pallas as pl
from jax.experimental.pallas import tpu as pltpu

f32 = jnp.float32
bf16 = jnp.bfloat16

D_MODEL = 1024
D_FF = 4 * D_MODEL
NORM_EPS = 1e-6

ATT_HEADS = 16
ATT_KV_HEADS = 4
HEAD_DIM = 64
IDX_HEADS = 8
TOPK_MAX = 256
ROPE_THETA = 10000.0

RWKV_HEAD = 64
RWKV_HEADS = 16
LNX_EPS = 64e-5
CHUNK = 64

LANES = 128
VMEM_LIMIT = 56 * 1024 * 1024

NT = (((1,), (1,)), ((), ()))
TN = (((0,), (0,)), ((), ()))


def _cparams(*sem):
    return pltpu.CompilerParams(dimension_semantics=sem, vmem_limit_bytes=VMEM_LIMIT)


def _rms(x, g):
    return x * lax.rsqrt(jnp.mean(x * x, axis=-1, keepdims=True) + NORM_EPS) * g


def _bdot(a, b, dims=None):
    a = a.astype(bf16)
    b = b.astype(bf16)
    if dims is None:
        return jnp.dot(a, b, preferred_element_type=f32)
    return lax.dot_general(a, b, dims, preferred_element_type=f32)


def _split_dot(a, b, parts):
    out = None
    rem = a
    for p in range(parts):
        piece = rem.astype(bf16)
        term = jnp.dot(piece, b, preferred_element_type=f32)
        out = term if out is None else out + term
        if p + 1 < parts:
            rem = rem - piece.astype(f32)
    return out


def _head_sum(x, hred, hexp):
    red = jnp.dot(x.astype(bf16), hred, preferred_element_type=f32)
    return _split_dot(red, hexp, 2)


def _const_spec(shape):
    return pl.BlockSpec(shape, lambda *_: (0,) * len(shape))


MLP_TM = 512
MLP_FCH = 512


def _mlp_kernel(x_ref, g_ref, wu_ref, wd_ref, fg_ref, o_ref, *, final):
    x = x_ref[...]
    xn = _rms(x, g_ref[...]).astype(bf16)
    o_ref[...] = x
    for f in range(0, D_FF, MLP_FCH):
        u = jnp.dot(xn, wu_ref[:, f:f + MLP_FCH], preferred_element_type=f32)
        u = jnp.maximum(u, 0.0)
        o_ref[...] += jnp.dot((u * u).astype(bf16), wd_ref[f:f + MLP_FCH, :],
                              preferred_element_type=f32)
    if final:
        o_ref[...] = _rms(o_ref[...], fg_ref[...])


def _mlp(x2, g, wu, wd, fg, final):
    m = x2.shape[0]
    return pl.pallas_call(
        functools.partial(_mlp_kernel, final=final),
        grid=(m // MLP_TM,),
        in_specs=[pl.BlockSpec((MLP_TM, D_MODEL), lambda i: (i, 0)),
                  _const_spec((1, D_MODEL)),
                  _const_spec((D_MODEL, D_FF)),
                  _const_spec((D_FF, D_MODEL)),
                  _const_spec((1, D_MODEL))],
        out_specs=pl.BlockSpec((MLP_TM, D_MODEL), lambda i: (i, 0)),
        out_shape=jax.ShapeDtypeStruct((m, D_MODEL), f32),
        compiler_params=_cparams("parallel"),
        name="mlp",
    )(x2, g, wu, wd, fg)


PROJ_TM = 512


def _resproj_kernel(a_ref, w_ref, r_ref, o_ref):
    o_ref[...] = r_ref[...] + jnp.dot(a_ref[...], w_ref[...], preferred_element_type=f32)


def _resproj(a, w, res):
    m, k = a.shape
    n = w.shape[1]
    return pl.pallas_call(
        _resproj_kernel,
        grid=(m // PROJ_TM,),
        in_specs=[pl.BlockSpec((PROJ_TM, k), lambda i: (i, 0)),
                  _const_spec((k, n)),
                  pl.BlockSpec((PROJ_TM, n), lambda i: (i, 0))],
        out_specs=pl.BlockSpec((PROJ_TM, n), lambda i: (i, 0)),
        out_shape=jax.ShapeDtypeStruct((m, n), f32),
        compiler_params=_cparams("parallel"),
        name="resproj",
    )(a, w, res)


DSA_TM = 256


def _rope(x, cos, sin_signed, first_half):
    fwd = pltpu.roll(x, 32, 1)
    bwd = pltpu.roll(x, 96, 1)
    return x * cos + jnp.where(first_half, bwd, fwd) * sin_signed


LOG2E = 1.4426950408889634


VT_ROWS = 80


def _dsa_proj_kernel(x_ref, g_ref, wq_ref, wk_ref, wv_ref, wqi_ref, wki_ref, wwi_ref,
                     cos_ref, sin_ref,
                     q_ref, k_ref, vt_ref, qi_ref, ki_ref, wit_ref):
    h = _rms(x_ref[...], g_ref[...]).astype(bf16)
    tm = h.shape[0]
    cos = cos_ref[...]
    sin = sin_ref[...]
    lane = lax.broadcasted_iota(jnp.int32, (1, LANES), 1)
    first_half = (lane % HEAD_DIM) < (HEAD_DIM // 2)
    low = lane < HEAD_DIM

    def per_head(x, out_ref, pair, scale):
        xc = _rope(x[:, pair * LANES:(pair + 1) * LANES], cos, sin, first_half) * scale
        swapped = pltpu.roll(xc, HEAD_DIM, 1)
        out_ref[:, (2 * pair) * LANES:(2 * pair + 1) * LANES] = jnp.where(low, xc, 0.0).astype(bf16)
        out_ref[:, (2 * pair + 1) * LANES:(2 * pair + 2) * LANES] = jnp.where(low, swapped, 0.0).astype(bf16)

    q = jnp.dot(h, wq_ref[...], preferred_element_type=f32)
    for pair in range(ATT_HEADS // 2):
        per_head(q, q_ref, pair, HEAD_DIM ** -0.5 * LOG2E)
    qi = jnp.dot(h, wqi_ref[...], preferred_element_type=f32)
    for pair in range(IDX_HEADS // 2):
        per_head(qi, qi_ref, pair, 1.0)

    k2 = jnp.dot(h, wk_ref[...], preferred_element_type=f32)
    for c in range(0, ATT_KV_HEADS * LANES, LANES):
        kc = _rope(k2[:, c:c + LANES], cos, sin, first_half)
        k_ref[0, :, c:c + LANES] = jnp.where(low, kc, 0.0).astype(bf16)
    ki2 = _rope(jnp.dot(h, wki_ref[...], preferred_element_type=f32), cos, sin, first_half)
    ki_ref[0] = jnp.where(low, ki2, 0.0).astype(bf16)

    vt = lax.dot_general(wv_ref[...], h, NT, preferred_element_type=f32)
    extra = (lax.broadcasted_iota(jnp.int32, (VT_ROWS - HEAD_DIM, tm), 0) == 0).astype(f32)
    for g in range(ATT_KV_HEADS):
        vt_ref[0, g * VT_ROWS:(g + 1) * VT_ROWS, :] = jnp.concatenate(
            [vt[g * HEAD_DIM:(g + 1) * HEAD_DIM], extra], axis=0).astype(bf16)
    wit_ref[...] = (lax.dot_general(wwi_ref[...], h, NT, preferred_element_type=f32)
                    * (IDX_HEADS ** -0.5 * HEAD_DIM ** -0.5))


def _dsa_proj(x2, g, wq, wk2, wv, wqi, wki2, wwi, cos, sin, batch, seq):
    m = x2.shape[0]
    nblk = seq // DSA_TM
    row = lambda n: pl.BlockSpec((DSA_TM, n), lambda b, i: (b * nblk + i, 0))
    seqrow = lambda n: pl.BlockSpec((1, DSA_TM, n), lambda b, i: (b, i, 0))
    tab = pl.BlockSpec((DSA_TM, LANES), lambda b, i: (i, 0))
    kvw = ATT_KV_HEADS * LANES
    vtr = ATT_KV_HEADS * VT_ROWS
    return pl.pallas_call(
        _dsa_proj_kernel,
        grid=(batch, nblk),
        in_specs=[row(D_MODEL), _const_spec((1, D_MODEL)),
                  _const_spec(wq.shape), _const_spec(wk2.shape), _const_spec(wv.shape),
                  _const_spec(wqi.shape), _const_spec(wki2.shape), _const_spec(wwi.shape),
                  tab, tab],
        out_specs=[row(ATT_HEADS * LANES), seqrow(kvw),
                   pl.BlockSpec((1, vtr, DSA_TM), lambda b, i: (b, 0, i)),
                   row(IDX_HEADS * LANES), seqrow(LANES),
                   pl.BlockSpec((LANES, DSA_TM), lambda b, i: (0, b * nblk + i))],
        out_shape=[jax.ShapeDtypeStruct((m, ATT_HEADS * LANES), bf16),
                   jax.ShapeDtypeStruct((batch, seq, kvw), bf16),
                   jax.ShapeDtypeStruct((batch, vtr, seq), bf16),
                   jax.ShapeDtypeStruct((m, IDX_HEADS * LANES), bf16),
                   jax.ShapeDtypeStruct((batch, seq, LANES), bf16),
                   jax.ShapeDtypeStruct((LANES, m), f32)],
        compiler_params=_cparams("parallel", "parallel"),
        name="dsa_proj",
    )(x2, g, wq, wk2, wv, wqi, wki2, wwi, cos, sin)


DSA_TQ = 128
DSA_CLS = 512
IDX_GROUP = 4
INT_MIN = -2 ** 31
NEG_INF_KEY = -2139095041


def _fold_rows(x, op):
    while x.shape[0] % 16 == 0:
        h = x.shape[0] // 2
        x = op(x[:h], x[h:])
    return x


def _col_count(mask):
    return jnp.sum(_fold_rows(jnp.where(mask, 1.0, 0.0), jnp.add), axis=0, keepdims=True).astype(jnp.int32)


def _stack_heads(ref, first, count):
    return jnp.concatenate([ref[:, (first + j) * LANES:(first + j + 1) * LANES] for j in range(count)],
                           axis=0)


def _dsa_attn_body(q_ref, qi_ref, wit_ref, ki_ref, k_ref, vt_ref, o_ref,
                   pos_ref, *, width, top_k, t0):
    col_t = t0 + lax.broadcasted_iota(jnp.int32, (1, DSA_TQ), 1)
    row_s = lax.broadcasted_iota(jnp.int32, (width, 1), 0)
    causal = row_s <= col_t

    wit = wit_ref[...]
    ki = ki_ref[0, 0:width, :]
    score = jnp.zeros((width, DSA_TQ), f32)
    for first in range(0, IDX_HEADS, IDX_GROUP):
        rel = lax.dot_general(ki, _stack_heads(qi_ref, first, IDX_GROUP), NT,
                              preferred_element_type=f32)
        for j in range(IDX_GROUP):
            hd = first + j
            score = score + wit[hd:hd + 1, :] * jnp.maximum(rel[:, j * DSA_TQ:(j + 1) * DSA_TQ], 0.0)
    score = jnp.where(causal, score, -jnp.inf)

    bits = lax.bitcast_convert_type(score, jnp.int32)
    key = bits ^ ((bits >> 31) & jnp.int32(0x7FFFFFFF))

    thr = jnp.where(_col_count(key >= 0) >= top_k, jnp.int32(0), jnp.int32(INT_MIN))

    def value_step(i, thr):
        cand = thr | (jnp.int32(1) << (30 - i))
        return jnp.where(_col_count(key >= cand) >= top_k, cand, thr)

    thr = lax.fori_loop(0, 31, value_step, thr)

    above = key > thr
    tied = key == thr
    n_above = _col_count(above)
    overfull = ((n_above + _col_count(tied)) != top_k) & (thr != NEG_INF_KEY)
    pos_ref[...] = jnp.full(pos_ref.shape, width, jnp.int32)

    @pl.when(jnp.max(overfull.astype(f32)) > 0.0)
    def _():
        need = top_k - n_above
        nbits = (width - 1).bit_length()

        def index_step(i, pos):
            cand = pos | (jnp.int32(1) << (nbits - 1 - i))
            return jnp.where(_col_count(tied & (row_s < cand)) < need, cand, pos)

        pos = lax.fori_loop(0, nbits, index_step, jnp.zeros((1, DSA_TQ), jnp.int32))
        pos_ref[...] = jnp.broadcast_to(pos, pos_ref.shape)

    keep = (above | (tied & (row_s <= pos_ref[0:1, :]))) & causal
    bias = jnp.where(keep, 0.0, -jnp.inf)

    group = ATT_HEADS // ATT_KV_HEADS
    for g in range(ATT_KV_HEADS):
        logits4 = lax.dot_general(k_ref[0, 0:width, g * LANES:(g + 1) * LANES],
                                  _stack_heads(q_ref, g * group, group), NT,
                                  preferred_element_type=f32)
        vt = vt_ref[0, g * VT_ROWS:(g + 1) * VT_ROWS, 0:width]
        outs = []
        for j in range(group):
            logits = logits4[:, j * DSA_TQ:(j + 1) * DSA_TQ] + bias
            mx = jnp.max(_fold_rows(logits, jnp.maximum), axis=0, keepdims=True)
            p = jnp.exp2(logits - mx).astype(bf16)
            pv = jnp.dot(vt, p, preferred_element_type=f32)
            outs.append(pv[:HEAD_DIM] * (1.0 / pv[HEAD_DIM:HEAD_DIM + 1]))
        for pair in range(group // 2):
            both = jnp.concatenate(outs[2 * pair:2 * pair + 2], axis=0).T
            col = (g * group // 2 + pair) * LANES
            o_ref[:, col:col + LANES] = both.astype(bf16)


def _dsa_attn_kernel(q_ref, qi_ref, wit_ref, ki_ref, k_ref, vt_ref, o_ref,
                     pos_ref, *, seq, top_k):
    i = pl.program_id(1)
    blocks_per_class = DSA_CLS // DSA_TQ
    for cls in range(seq // DSA_CLS):
        @pl.when(i // blocks_per_class == cls)
        def _(cls=cls):
            _dsa_attn_body(q_ref, qi_ref, wit_ref, ki_ref, k_ref, vt_ref, o_ref,
                           pos_ref, width=DSA_CLS * (cls + 1), top_k=top_k, t0=i * DSA_TQ)


def _dsa_attn(q, qi, wit, ki, k, vt, batch, seq):
    nq = seq // DSA_TQ
    top_k = min(TOPK_MAX, seq // 4)
    qrow = lambda n: pl.BlockSpec((DSA_TQ, n), lambda b, i: (b * nq + i, 0))
    full = lambda n: pl.BlockSpec((1, seq, n), lambda b, i: (b, 0, 0))
    return pl.pallas_call(
        functools.partial(_dsa_attn_kernel, seq=seq, top_k=top_k),
        grid=(batch, nq),
        in_specs=[qrow(ATT_HEADS * LANES), qrow(IDX_HEADS * LANES),
                  pl.BlockSpec((8, DSA_TQ), lambda b, i: (0, b * nq + i)),
                  full(LANES), full(ATT_KV_HEADS * LANES),
                  pl.BlockSpec((1, ATT_KV_HEADS * VT_ROWS, seq), lambda b, i: (b, 0, 0))],
        out_specs=qrow(ATT_HEADS * HEAD_DIM),
        out_shape=jax.ShapeDtypeStruct((batch * seq, ATT_HEADS * HEAD_DIM), bf16),
        scratch_shapes=[pltpu.VMEM((8, DSA_TQ), jnp.int32)],
        compiler_params=_cparams("parallel", "arbitrary"),
        name="dsa_attn",
    )(q, qi, wit, ki, k, vt)


RW_TM = 256
LORA_PAD = 128
GATE_PAD = 256


def _rwkv_pre_kernel(x_ref, xp_ref, g_ref, mu_ref, wr_ref, wk_ref, wv_ref,
                     w1_ref, w2_ref, a1_ref, a2_ref, g1_ref, g2_ref,
                     w0_ref, a0_ref, kk_ref, ka_ref, rk_ref, tri_ref, hred_ref, hexp_ref,
                     at_ref, bt_ref, kt_ref, rt_ref, v_ref, gc_ref, bonus_ref, gate_ref,
                     lc_scr, *, seq):
    i = pl.program_id(0)
    g = g_ref[...]
    h = _rms(x_ref[...], g)
    hp = _rms(xp_ref[...], g)[7:8, :]
    hp = jnp.where((i * RW_TM) % seq == 0, 0.0, hp)
    rowi = lax.broadcasted_iota(jnp.int32, (RW_TM, 1), 0)
    hs = jnp.where(rowi == 0, hp, pltpu.roll(h, 1, 0))
    xx = hs - h
    mu = mu_ref[...]

    def mix(c):
        return (h + xx * mu[c:c + 1, :]).astype(bf16)

    r = jnp.dot(mix(0), wr_ref[...], preferred_element_type=f32)
    k = jnp.dot(mix(1), wk_ref[...], preferred_element_type=f32)
    v = jnp.dot(mix(2), wv_ref[...], preferred_element_type=f32)

    wl = w0_ref[...] + _bdot(jnp.tanh(jnp.dot(mix(3), w1_ref[...], preferred_element_type=f32)),
                             w2_ref[...])
    nwl = -wl
    softplus = jnp.maximum(nwl, 0.0) + jnp.log1p(jnp.exp(-jnp.abs(nwl)))
    ld = -jnp.exp(-softplus - 0.5)
    a = jax.nn.sigmoid(a0_ref[...] + _bdot(jnp.dot(mix(4), a1_ref[...],
                                                   preferred_element_type=f32), a2_ref[...]))
    gate_ref[...] = _bdot(jax.nn.sigmoid(jnp.dot(mix(5), g1_ref[...],
                                                 preferred_element_type=f32)), g2_ref[...])

    hred = hred_ref[...]
    hexp = hexp_ref[...]
    z = k * kk_ref[...]
    kk = z * lax.rsqrt(jnp.maximum(_head_sum(z * z, hred, hexp), 1e-24))
    k2 = k * (1.0 + (a - 1.0) * ka_ref[...])
    bonus_ref[...] = _head_sum(r * k2 * rk_ref[...], hred, hexp) * v

    lc = _split_dot_left(tri_ref[...], ld, 2)
    lc_scr[...] = lc
    for c in range(RW_TM // CHUNK):
        last = lc_scr[c * CHUNK + CHUNK - 1:c * CHUNK + CHUNK, :]
        gc_ref[c * 8:(c + 1) * 8, :] = jnp.broadcast_to(jnp.exp(last), (8, D_MODEL))
    einv = jnp.exp(-lc)
    at_ref[...] = -kk * jnp.exp(lc - ld)
    bt_ref[...] = kk * a * einv
    kt_ref[...] = k2 * einv
    rt_ref[...] = r * jnp.exp(lc)
    v_ref[...] = v


def _split_dot_left(a, b, parts):
    out = None
    rem = b
    for p in range(parts):
        piece = rem.astype(bf16)
        term = jnp.dot(a, piece, preferred_element_type=f32)
        out = term if out is None else out + term
        if p + 1 < parts:
            rem = rem - piece.astype(f32)
    return out


def _rwkv_pre(x2, g, mu8, wr, wk, wv, w1, w2, a1, a2, g1, g2, w0, a0, k_k, k_a, r_k, tri, hred, hexp,
              seq):
    m = x2.shape[0]
    row = pl.BlockSpec((RW_TM, D_MODEL), lambda i: (i, 0))
    prev = pl.BlockSpec((8, D_MODEL), lambda i: (jnp.maximum(i * (RW_TM // 8) - 1, 0), 0))
    vec = _const_spec((1, D_MODEL))
    gcrow = pl.BlockSpec((RW_TM // CHUNK * 8, D_MODEL), lambda i: (i, 0))
    act = jax.ShapeDtypeStruct((m, D_MODEL), f32)
    return pl.pallas_call(
        functools.partial(_rwkv_pre_kernel, seq=seq),
        grid=(m // RW_TM,),
        in_specs=[row, prev, vec, _const_spec((8, D_MODEL)),
                  _const_spec(wr.shape), _const_spec(wk.shape), _const_spec(wv.shape),
                  _const_spec(w1.shape), _const_spec(w2.shape),
                  _const_spec(a1.shape), _const_spec(a2.shape),
                  _const_spec(g1.shape), _const_spec(g2.shape),
                  vec, vec, vec, vec, vec,
                  _const_spec(tri.shape), _const_spec(hred.shape), _const_spec(hexp.shape)],
        out_specs=[row, row, row, row, row, gcrow, row, row],
        out_shape=[act, act, act, act, act,
                   jax.ShapeDtypeStruct((m // CHUNK * 8, D_MODEL), f32), act, act],
        scratch_shapes=[pltpu.VMEM((RW_TM, D_MODEL), f32)],
        compiler_params=_cparams("parallel"),
        name="rwkv_pre",
    )(x2, x2, g, mu8, wr, wk, wv, w1, w2, a1, a2, g1, g2, w0, a0, k_k, k_a, r_k, tri, hred, hexp)


RA_GROUP = 128
RA_ROWS = 1024


def _rwkv_chunk_kernel(at_ref, bt_ref, kt_ref, rt_ref, v_ref, gc_ref,
                       q_ref, yl_ref, m_ref, g_ref):
    n = RA_GROUP
    ri = lax.broadcasted_iota(jnp.int32, (n, n), 0)
    ci = lax.broadcasted_iota(jnp.int32, (n, n), 1)
    same = (ri // CHUNK) == (ci // CHUNK)
    strict = same & (ci < ri)
    incl = same & (ci <= ri)
    eye = (ri == ci).astype(f32)
    lane = lax.broadcasted_iota(jnp.int32, (1, LANES), 1)
    low = lane < RWKV_HEAD
    rk = lax.broadcasted_iota(jnp.int32, (RWKV_HEAD, LANES), 0)
    diag = rk == (lax.broadcasted_iota(jnp.int32, (RWKV_HEAD, LANES), 1) % RWKV_HEAD)

    groups = []
    for gi in range(RA_ROWS // n):
        rows = slice(gi * n, (gi + 1) * n)
        at, bt, kt, rt, v = (r[rows, :] for r in (at_ref, bt_ref, kt_ref, rt_ref, v_ref))
        groups.append(dict(at=at, bt=bt, kt=kt, rt=rt, v=v, vb=v.astype(bf16),
                           ar=jnp.concatenate([at, rt], axis=0),
                           bk=jnp.concatenate([bt, kt], axis=0).astype(bf16)))
    probs = [(gr, hl) for gr in groups for hl in (low, jnp.logical_not(low))]

    gs = [lax.dot_general(jnp.where(hl, gr["ar"], 0.0).astype(bf16), gr["bk"], NT,
                          preferred_element_type=f32) for gr, hl in probs]
    aab = [jnp.where(strict, g[:n, :n], 0.0) for g in gs]
    aak = [jnp.where(strict, g[:n, n:], 0.0) for g in gs]
    arb = [jnp.where(incl, g[n:, :n], 0.0) for g in gs]
    ark = [jnp.where(incl, g[n:, n:], 0.0) for g in gs]

    ts = [eye + a for a in aab]
    ps = aab
    for _ in range((CHUNK - 1).bit_length() - 1):
        ps = [_bdot(p, p) for p in ps]
        ts = [t + _bdot(t, p) for t, p in zip(ts, ps)]

    akv = [_bdot(a, gr["vb"]) for a, (gr, _) in zip(aak, probs)]
    tw = [_bdot(t, jnp.concatenate([gr["at"], x], axis=1))
          for t, x, (gr, _) in zip(ts, akv, probs)]
    qy = [_bdot(a, w) for a, w in zip(arb, tw)]
    rkv = [_bdot(a, gr["vb"]) for a, (gr, _) in zip(ark, probs)]

    for gi, gr in enumerate(groups):
        lo, hi = 2 * gi, 2 * gi + 1
        w_hat = jnp.where(low, tw[lo][:, :LANES], tw[hi][:, :LANES])
        u_loc = jnp.where(low, tw[lo][:, LANES:], tw[hi][:, LANES:])
        base = gi * n
        q_ref[base:base + n, :] = gr["rt"] + jnp.where(low, qy[lo][:, :LANES], qy[hi][:, :LANES])
        yl_ref[base:base + n, :] = (jnp.where(low, qy[lo][:, LANES:], qy[hi][:, LANES:])
                                    + jnp.where(low, rkv[lo], rkv[hi]))
        for c in range(n // CHUNK):
            rows = slice(c * CHUNK, (c + 1) * CHUNK)
            ch = base // CHUNK + c
            gc = gc_ref[ch * 8:ch * 8 + 1, :]
            bh = gr["bt"][rows] * gc
            kh = gr["kt"][rows] * gc
            pm = _bdot(w_hat[rows], bh, TN)
            pg = _bdot(jnp.concatenate([u_loc[rows], gr["v"][rows]], axis=0),
                       jnp.concatenate([bh, kh], axis=0), TN)
            out = slice(base + c * CHUNK, base + (c + 1) * CHUNK)
            m_ref[out, :] = (jnp.where(low, pm[:RWKV_HEAD], pm[RWKV_HEAD:])
                             + jnp.where(diag, gc, 0.0))
            g_ref[out, :] = jnp.where(low, pg[:RWKV_HEAD], pg[RWKV_HEAD:])


def _rwkv_chunk(at, bt, kt, rt, v, gc):
    m = at.shape[0]
    blk = pl.BlockSpec((RA_ROWS, LANES), lambda i, p: (i, p))
    gblk = pl.BlockSpec((RA_ROWS // CHUNK * 8, LANES), lambda i, p: (i, p))
    act = jax.ShapeDtypeStruct((m, D_MODEL), f32)
    return pl.pallas_call(
        _rwkv_chunk_kernel,
        grid=(m // RA_ROWS, D_MODEL // LANES),
        in_specs=[blk, blk, blk, blk, blk, gblk],
        out_specs=[blk, blk, blk, blk],
        out_shape=[act, act, act, act],
        compiler_params=_cparams("parallel", "parallel"),
        name="rwkv_chunk",
    )(at, bt, kt, rt, v, gc)


SCAN_LANES = 512


def _rwkv_scan_kernel(q_ref, yl_ref, m_ref, g_ref, y_ref, *, seq):
    pairs = SCAN_LANES // LANES
    low = lax.broadcasted_iota(jnp.int32, (1, LANES), 1) < RWKV_HEAD

    def blockdiag(x):
        return jnp.concatenate([jnp.where(low, x, 0.0), jnp.where(low, 0.0, x)], axis=0)

    def body(c, states):
        rows = pl.ds(pl.multiple_of(c * CHUNK, CHUNK), CHUNK)
        new_states = []
        for p in range(pairs):
            cols = slice(p * LANES, (p + 1) * LANES)
            s = states[p]
            y_ref[rows, cols] = yl_ref[rows, cols] + _bdot(q_ref[rows, cols], s, NT)
            mc = blockdiag(m_ref[rows, cols]).astype(bf16)
            s_hi = s.astype(bf16)
            s_lo = (s - s_hi.astype(f32)).astype(bf16)
            sm = (jnp.dot(s_hi, mc, preferred_element_type=f32)
                  + jnp.dot(s_lo, mc, preferred_element_type=f32))
            new_states.append(sm + blockdiag(g_ref[rows, cols]))
        return tuple(new_states)

    init = tuple(jnp.zeros((LANES, LANES), f32) for _ in range(pairs))
    lax.fori_loop(0, seq // CHUNK, body, init)


def _rwkv_scan(q, yl, mm, gg, batch, seq):
    blk = pl.BlockSpec((seq, SCAN_LANES), lambda b, p: (b, p))
    return pl.pallas_call(
        functools.partial(_rwkv_scan_kernel, seq=seq),
        grid=(batch, D_MODEL // SCAN_LANES),
        in_specs=[blk, blk, blk, blk],
        out_specs=blk,
        out_shape=jax.ShapeDtypeStruct((batch * seq, D_MODEL), f32),
        compiler_params=_cparams("parallel", "parallel"),
        name="rwkv_scan",
    )(q, yl, mm, gg)


def _rwkv_post_kernel(y_ref, bonus_ref, gate_ref, x_ref, lw_ref, lb_ref, hred_ref, hexp_ref,
                      wo_ref, o_ref):
    y = y_ref[...]
    hred = hred_ref[...]
    hexp = hexp_ref[...]
    mean = _head_sum(y, hred, hexp) * (1.0 / RWKV_HEAD)
    yc = y - mean
    var = _head_sum(yc * yc, hred, hexp) * (1.0 / RWKV_HEAD)
    yn = yc * lax.rsqrt(var + LNX_EPS) * lw_ref[...] + lb_ref[...]
    out = (yn + bonus_ref[...]) * gate_ref[...]
    o_ref[...] = x_ref[...] + jnp.dot(out.astype(bf16), wo_ref[...], preferred_element_type=f32)


def _rwkv_post(y, bonus, gate, x2, lw, lb, hred, hexp, wo):
    m = y.shape[0]
    row = pl.BlockSpec((RW_TM, D_MODEL), lambda i: (i, 0))
    vec = _const_spec((1, D_MODEL))
    return pl.pallas_call(
        _rwkv_post_kernel,
        grid=(m // RW_TM,),
        in_specs=[row, row, row, row, vec, vec, _const_spec(hred.shape), _const_spec(hexp.shape),
                  _const_spec(wo.shape)],
        out_specs=row,
        out_shape=jax.ShapeDtypeStruct((m, D_MODEL), f32),
        compiler_params=_cparams("parallel"),
        name="rwkv_post",
    )(y, bonus, gate, x2, lw, lb, hred, hexp, wo)


def _dup_heads(w, n_heads):
    w = w.reshape(w.shape[0], n_heads, 1, HEAD_DIM)
    return jnp.broadcast_to(w, (w.shape[0], n_heads, 2, HEAD_DIM)).reshape(w.shape[0], n_heads * LANES)


def _pad_cols(w, n):
    return jnp.pad(w, ((0, 0), (0, n - w.shape[1])))


def _pad_rows(w, n):
    return jnp.pad(w, ((0, n - w.shape[0]), (0, 0)))


def _rope_tables(seq):
    inv = 1.0 / (ROPE_THETA ** (jnp.arange(0, HEAD_DIM, 2, dtype=f32) / HEAD_DIM))
    ang = jnp.arange(seq, dtype=f32)[:, None] * inv[None, :]
    cos, sin = jnp.cos(ang), jnp.sin(ang)
    return jnp.tile(jnp.concatenate([cos, cos], 1), (1, 2)), jnp.tile(jnp.concatenate([-sin, sin], 1), (1, 2))


def _dsa_layer(x2, g, w_in, w_o, batch, seq):
    o0 = ATT_HEADS * HEAD_DIM
    o1 = o0 + ATT_KV_HEADS * HEAD_DIM
    o2 = o1 + ATT_KV_HEADS * HEAD_DIM
    o3 = o2 + IDX_HEADS * HEAD_DIM
    o4 = o3 + HEAD_DIM
    w_in = w_in.astype(bf16)
    wq = w_in[:, :o0]
    wk2 = _dup_heads(w_in[:, o0:o1], ATT_KV_HEADS)
    wv = w_in[:, o1:o2].T
    wqi = w_in[:, o2:o3]
    wki2 = _dup_heads(w_in[:, o3:o4], 1)
    wwi = _pad_cols(w_in[:, o4:], LANES).T
    cos, sin = _rope_tables(seq)
    q, k, vt, qi, ki, wit = _dsa_proj(x2, g, wq, wk2, wv, wqi, wki2, wwi, cos, sin, batch, seq)
    o = _dsa_attn(q, qi, wit, ki, k, vt, batch, seq)
    return _resproj(o, w_o.astype(bf16), x2)


def _rwkv_layer(x2, g, mu, w_rkv, w0, w1, w2, a0, a1, a2, g1, g2, k_k, k_a, r_k, lnx_w, lnx_b, w_o,
                batch, seq):
    vec = lambda p: p.reshape(1, D_MODEL)
    w_rkv = w_rkv.astype(bf16)
    ri = jnp.arange(RW_TM)
    tri = ((ri[:, None] >= ri[None, :]) & (ri[:, None] // CHUNK == ri[None, :] // CHUNK)).astype(bf16)
    di = jnp.arange(D_MODEL)
    hred = (di[:, None] // RWKV_HEAD == jnp.arange(LANES)[None, :]).astype(bf16)
    hexp = hred.T
    at, bt, kt, rt, v, gc, bonus, gate = _rwkv_pre(
        x2, g, _pad_rows(mu, 8), w_rkv[0], w_rkv[1], w_rkv[2],
        _pad_cols(w1, LORA_PAD).astype(bf16), _pad_rows(w2, LORA_PAD).astype(bf16),
        _pad_cols(a1, LORA_PAD).astype(bf16), _pad_rows(a2, LORA_PAD).astype(bf16),
        _pad_cols(g1, GATE_PAD).astype(bf16), _pad_rows(g2, GATE_PAD).astype(bf16),
        vec(w0), vec(a0), vec(k_k), vec(k_a), vec(r_k), tri, hred, hexp, seq)
    q, yl, mm, gg = _rwkv_chunk(at, bt, kt, rt, v, gc)
    y = _rwkv_scan(q, yl, mm, gg, batch, seq)
    return _rwkv_post(y, bonus, gate, x2, vec(lnx_w), vec(lnx_b), hred, hexp, w_o.astype(bf16))


def kernel(x, mixer_norm, mlp_norm, mlp_w_up, mlp_w_down, final_norm, dsa_w_in, dsa_w_o, rwkv_mu, rwkv_w_rkv, rwkv_w0, rwkv_w1, rwkv_w2, rwkv_a0, rwkv_a1, rwkv_a2, rwkv_g1, rwkv_g2, rwkv_k_k, rwkv_k_a, rwkv_r_k, rwkv_lnx_w, rwkv_lnx_b, rwkv_w_o):
    batch, seq, _ = x.shape
    depth = mixer_norm.shape[0]
    x2 = x.reshape(batch * seq, D_MODEL)
    fg = final_norm.reshape(1, D_MODEL)
    for i in range(depth):
        g = mixer_norm[i].reshape(1, D_MODEL)
        j = i // 2
        if i % 2 == 0:
            x2 = _dsa_layer(x2, g, dsa_w_in[j], dsa_w_o[j], batch, seq)
        else:
            x2 = _rwkv_layer(x2, g, rwkv_mu[j], rwkv_w_rkv[j], rwkv_w0[j], rwkv_w1[j], rwkv_w2[j],
                             rwkv_a0[j], rwkv_a1[j], rwkv_a2[j], rwkv_g1[j], rwkv_g2[j],
                             rwkv_k_k[j], rwkv_k_a[j], rwkv_r_k[j], rwkv_lnx_w[j], rwkv_lnx_b[j],
                             rwkv_w_o[j], batch, seq)
        x2 = _mlp(x2, mlp_norm[i].reshape(1, D_MODEL), mlp_w_up[i].astype(bf16),
                  mlp_w_down[i].astype(bf16), fg, final=(i == depth - 1))
    return x2.reshape(batch, seq, D_MODEL)
```

```python
import functools

import jax
import jax.numpy as jnp
from jax import lax
from jax.experimental import pallas as pl
from jax.experimental.pallas import tpu as pltpu

f32 = jnp.float32
bf16 = jnp.bfloat16

D_MODEL = 1024
D_FF = 4 * D_MODEL
NORM_EPS = 1e-6

ATT_HEADS = 16
ATT_KV_HEADS = 4
HEAD_DIM = 64
IDX_HEADS = 8
TOPK_MAX = 256
ROPE_THETA = 10000.0

RWKV_HEAD = 64
RWKV_HEADS = 16
LNX_EPS = 64e-5
CHUNK = 64

LANES = 128
VMEM_LIMIT = 56 * 1024 * 1024

NT = (((1,), (1,)), ((), ()))
TN = (((0,), (0,)), ((), ()))


def _cparams(*sem):
    return pltpu.CompilerParams(dimension_semantics=sem, vmem_limit_bytes=VMEM_LIMIT)


def _rms(x, g):
    return x * lax.rsqrt(jnp.mean(x * x, axis=-1, keepdims=True) + NORM_EPS) * g


def _bdot(a, b, dims=None):
    a = a.astype(bf16)
    b = b.astype(bf16)
    if dims is None:
        return jnp.dot(a, b, preferred_element_type=f32)
    return lax.dot_general(a, b, dims, preferred_element_type=f32)


def _split_dot(a, b, parts):
    out = None
    rem = a
    for p in range(parts):
        piece = rem.astype(bf16)
        term = jnp.dot(piece, b, preferred_element_type=f32)
        out = term if out is None else out + term
        if p + 1 < parts:
            rem = rem - piece.astype(f32)
    return out


def _head_sum(x, hred, hexp):
    red = jnp.dot(x.astype(bf16), hred, preferred_element_type=f32)
    return _split_dot(red, hexp, 2)


def _const_spec(shape):
    return pl.BlockSpec(shape, lambda *_: (0,) * len(shape))


MLP_TM = 512
MLP_FCH = 512


def _mlp_kernel(x_ref, g_ref, wu_ref, wd_ref, fg_ref, o_ref, *, final):
    x = x_ref[...]
    xn = _rms(x, g_ref[...]).astype(bf16)
    o_ref[...] = x
    for f in range(0, D_FF, MLP_FCH):
        u = jnp.dot(xn, wu_ref[:, f:f + MLP_FCH], preferred_element_type=f32)
        u = jnp.maximum(u, 0.0)
        o_ref[...] += jnp.dot((u * u).astype(bf16), wd_ref[f:f + MLP_FCH, :],
                              preferred_element_type=f32)
    if final:
        o_ref[...] = _rms(o_ref[...], fg_ref[...])


def _mlp(x2, g, wu, wd, fg, final):
    m = x2.shape[0]
    return pl.pallas_call(
        functools.partial(_mlp_kernel, final=final),
        grid=(m // MLP_TM,),
        in_specs=[pl.BlockSpec((MLP_TM, D_MODEL), lambda i: (i, 0)),
                  _const_spec((1, D_MODEL)),
                  _const_spec((D_MODEL, D_FF)),
                  _const_spec((D_FF, D_MODEL)),
                  _const_spec((1, D_MODEL))],
        out_specs=pl.BlockSpec((MLP_TM, D_MODEL), lambda i: (i, 0)),
        out_shape=jax.ShapeDtypeStruct((m, D_MODEL), f32),
        compiler_params=_cparams("parallel"),
        name="mlp",
    )(x2, g, wu, wd, fg)


PROJ_TM = 512


def _resproj_kernel(a_ref, w_ref, r_ref, o_ref):
    o_ref[...] = r_ref[...] + jnp.dot(a_ref[...], w_ref[...], preferred_element_type=f32)


def _resproj(a, w, res):
    m, k = a.shape
    n = w.shape[1]
    return pl.pallas_call(
        _resproj_kernel,
        grid=(m // PROJ_TM,),
        in_specs=[pl.BlockSpec((PROJ_TM, k), lambda i: (i, 0)),
                  _const_spec((k, n)),
                  pl.BlockSpec((PROJ_TM, n), lambda i: (i, 0))],
        out_specs=pl.BlockSpec((PROJ_TM, n), lambda i: (i, 0)),
        out_shape=jax.ShapeDtypeStruct((m, n), f32),
        compiler_params=_cparams("parallel"),
        name="resproj",
    )(a, w, res)


DSA_TM = 256


def _rope(x, cos, sin_signed, first_half):
    fwd = pltpu.roll(x, 32, 1)
    bwd = pltpu.roll(x, 96, 1)
    return x * cos + jnp.where(first_half, bwd, fwd) * sin_signed


LOG2E = 1.4426950408889634


VT_ROWS = 80


def _dsa_proj_kernel(x_ref, g_ref, wq_ref, wk_ref, wv_ref, wqi_ref, wki_ref, wwi_ref,
                     cos_ref, sin_ref,
                     q_ref, k_ref, vt_ref, qi_ref, ki_ref, wit_ref):
    h = _rms(x_ref[...], g_ref[...]).astype(bf16)
    tm = h.shape[0]
    cos = cos_ref[...]
    sin = sin_ref[...]
    lane = lax.broadcasted_iota(jnp.int32, (1, LANES), 1)
    first_half = (lane % HEAD_DIM) < (HEAD_DIM // 2)
    low = lane < HEAD_DIM

    def per_head(x, out_ref, pair, scale):
        xc = _rope(x[:, pair * LANES:(pair + 1) * LANES], cos, sin, first_half) * scale
        swapped = pltpu.roll(xc, HEAD_DIM, 1)
        out_ref[:, (2 * pair) * LANES:(2 * pair + 1) * LANES] = jnp.where(low, xc, 0.0).astype(bf16)
        out_ref[:, (2 * pair + 1) * LANES:(2 * pair + 2) * LANES] = jnp.where(low, swapped, 0.0).astype(bf16)

    q = jnp.dot(h, wq_ref[...], preferred_element_type=f32)
    for pair in range(ATT_HEADS // 2):
        per_head(q, q_ref, pair, HEAD_DIM ** -0.5 * LOG2E)
    qi = jnp.dot(h, wqi_ref[...], preferred_element_type=f32)
    for pair in range(IDX_HEADS // 2):
        per_head(qi, qi_ref, pair, 1.0)

    k2 = jnp.dot(h, wk_ref[...], preferred_element_type=f32)
    for c in range(0, ATT_KV_HEADS * LANES, LANES):
        kc = _rope(k2[:, c:c + LANES], cos, sin, first_half)
        k_ref[0, :, c:c + LANES] = jnp.where(low, kc, 0.0).astype(bf16)
    ki2 = _rope(jnp.dot(h, wki_ref[...], preferred_element_type=f32), cos, sin, first_half)
    ki_ref[0] = jnp.where(low, ki2, 0.0).astype(bf16)

    vt = lax.dot_general(wv_ref[...], h, NT, preferred_element_type=f32)
    extra = (lax.broadcasted_iota(jnp.int32, (VT_ROWS - HEAD_DIM, tm), 0) == 0).astype(f32)
    for g in range(ATT_KV_HEADS):
        vt_ref[0, g * VT_ROWS:(g + 1) * VT_ROWS, :] = jnp.concatenate(
            [vt[g * HEAD_DIM:(g + 1) * HEAD_DIM], extra], axis=0).astype(bf16)
    wit_ref[...] = (lax.dot_general(wwi_ref[...], h, NT, preferred_element_type=f32)
                    * (IDX_HEADS ** -0.5 * HEAD_DIM ** -0.5))


def _dsa_proj(x2, g, wq, wk2, wv, wqi, wki2, wwi, cos, sin, batch, seq):
    m = x2.shape[0]
    nblk = seq // DSA_TM
    row = lambda n: pl.BlockSpec((DSA_TM, n), lambda b, i: (b * nblk + i, 0))
    seqrow = lambda n: pl.BlockSpec((1, DSA_TM, n), lambda b, i: (b, i, 0))
    tab = pl.BlockSpec((DSA_TM, LANES), lambda b, i: (i, 0))
    kvw = ATT_KV_HEADS * LANES
    vtr = ATT_KV_HEADS * VT_ROWS
    return pl.pallas_call(
        _dsa_proj_kernel,
        grid=(batch, nblk),
        in_specs=[row(D_MODEL), _const_spec((1, D_MODEL)),
                  _const_spec(wq.shape), _const_spec(wk2.shape), _const_spec(wv.shape),
                  _const_spec(wqi.shape), _const_spec(wki2.shape), _const_spec(wwi.shape),
                  tab, tab],
        out_specs=[row(ATT_HEADS * LANES), seqrow(kvw),
                   pl.BlockSpec((1, vtr, DSA_TM), lambda b, i: (b, 0, i)),
                   row(IDX_HEADS * LANES), seqrow(LANES),
                   pl.BlockSpec((LANES, DSA_TM), lambda b, i: (0, b * nblk + i))],
        out_shape=[jax.ShapeDtypeStruct((m, ATT_HEADS * LANES), bf16),
                   jax.ShapeDtypeStruct((batch, seq, kvw), bf16),
                   jax.ShapeDtypeStruct((batch, vtr, seq), bf16),
                   jax.ShapeDtypeStruct((m, IDX_HEADS * LANES), bf16),
                   jax.ShapeDtypeStruct((batch, seq, LANES), bf16),
                   jax.ShapeDtypeStruct((LANES, m), f32)],
        compiler_params=_cparams("parallel", "parallel"),
        name="dsa_proj",
    )(x2, g, wq, wk2, wv, wqi, wki2, wwi, cos, sin)


DSA_TQ = 128
DSA_CLS = 512
SEARCH_BITS_PER_CHECK = 4
IDX_GROUP = 4
DSA_ROWS = 256
INT_MIN = -2 ** 31
NEG_INF_KEY = -2139095041


def _fold_rows(x, op):
    while x.shape[0] % 16 == 0:
        h = x.shape[0] // 2
        x = op(x[:h], x[h:])
    return x


def _col_count(mask):
    return jnp.sum(_fold_rows(jnp.where(mask, 1.0, 0.0), jnp.add), axis=0, keepdims=True).astype(jnp.int32)


def _stack_heads(ref, first, count):
    return jnp.concatenate([ref[:, (first + j) * LANES:(first + j + 1) * LANES] for j in range(count)],
                           axis=0)


def _dsa_attn_body(q_ref, qi_ref, wit_ref, ki_ref, k_ref, vt_ref, o_ref,
                   bias_ref, logit_ref, *, width, top_k, t0):
    col_t = t0 + lax.broadcasted_iota(jnp.int32, (1, DSA_TQ), 1)
    row_s = lax.broadcasted_iota(jnp.int32, (width, 1), 0)
    causal = row_s <= col_t

    wit = wit_ref[...]
    ki = ki_ref[0, 0:width, :]
    score = jnp.zeros((width, DSA_TQ), f32)
    for first in range(0, IDX_HEADS, IDX_GROUP):
        rel = lax.dot_general(ki, _stack_heads(qi_ref, first, IDX_GROUP), NT,
                              preferred_element_type=f32)
        for j in range(IDX_GROUP):
            hd = first + j
            score = score + wit[hd:hd + 1, :] * jnp.maximum(rel[:, j * DSA_TQ:(j + 1) * DSA_TQ], 0.0)
    score = jnp.where(causal, score, -jnp.inf)

    def key_to_float(k):
        k = jnp.maximum(k, NEG_INF_KEY)
        return lax.bitcast_convert_type(k ^ ((k >> 31) & jnp.int32(0x7FFFFFFF)), f32)

    n_nonneg = _col_count(score >= 0.0)
    start_high = n_nonneg >= top_k
    thr = jnp.where(start_high, jnp.int32(0), jnp.int32(INT_MIN))
    n_ge = jnp.where(start_high, n_nonneg, jnp.int32(width))

    def all_exact(n_ge):
        return (jnp.min((n_ge == top_k).astype(f32)) > 0.5).astype(jnp.int32)

    def search_cond(state):
        bit, _, _, done = state
        return (bit >= 0) & (done == 0)

    def search_body(state):
        bit, thr, n_ge, _ = state
        for _ in range(SEARCH_BITS_PER_CHECK):
            step = jnp.where(bit >= 0, jnp.int32(1) << jnp.maximum(bit, 0), jnp.int32(0))
            cand = thr | step
            cnt = _col_count(score >= key_to_float(cand))
            take = cnt >= top_k
            thr = jnp.where(take, cand, thr)
            n_ge = jnp.where(take, cnt, n_ge)
            bit = bit - 1
        return bit, thr, n_ge, all_exact(n_ge)

    _, thr, n_ge, _ = lax.while_loop(search_cond, search_body,
                                     (jnp.int32(30), thr, n_ge, all_exact(n_ge)))

    thr_f = key_to_float(thr)
    bias_ref[0:width, :] = jnp.where((score >= thr_f) & causal, 0.0, -jnp.inf)
    overfull = (n_ge != top_k) & (thr > NEG_INF_KEY)

    @pl.when(jnp.max(overfull.astype(f32)) > 0.0)
    def _():
        real_thr = thr > NEG_INF_KEY
        next_f = key_to_float(thr + 1)
        beyond = score >= next_f
        in_bin = (score >= thr_f) & jnp.logical_not(beyond) & real_thr
        offset = jnp.where(in_bin, score - jnp.where(real_thr, thr_f, 0.0), -1.0)
        need = top_k - _col_count(beyond)

        def offset_step(i, okey):
            cand = okey | (jnp.int32(1) << (30 - i))
            cnt = _col_count(offset >= lax.bitcast_convert_type(cand, f32))
            return jnp.where(cnt >= need, cand, okey)

        okey = lax.fori_loop(0, 31, offset_step, jnp.zeros((1, DSA_TQ), jnp.int32))
        othr = lax.bitcast_convert_type(okey, f32)
        above = offset > othr
        tied = offset == othr
        need = need - _col_count(above)
        nbits = (width - 1).bit_length()

        def index_step(i, pos):
            cand = pos | (jnp.int32(1) << (nbits - 1 - i))
            return jnp.where(_col_count(tied & (row_s < cand)) < need, cand, pos)

        pos = lax.fori_loop(0, nbits, index_step, jnp.zeros((1, DSA_TQ), jnp.int32))
        keep = (beyond | above | (tied & (row_s <= pos))) & causal
        bias_ref[0:width, :] = jnp.where(keep, 0.0, -jnp.inf)

    group = ATT_HEADS // ATT_KV_HEADS
    for g in range(ATT_KV_HEADS):
        rq = _stack_heads(q_ref, g * group, group)
        maxes = [None] * group
        for rc in range(0, width, DSA_ROWS):
            logits4 = lax.dot_general(k_ref[0, rc:rc + DSA_ROWS, g * LANES:(g + 1) * LANES], rq, NT,
                                      preferred_element_type=f32)
            b = bias_ref[rc:rc + DSA_ROWS, :]
            for j in range(group):
                logits = logits4[:, j * DSA_TQ:(j + 1) * DSA_TQ] + b
                logit_ref[j, rc:rc + DSA_ROWS, :] = logits
                m = _fold_rows(logits, jnp.maximum)
                maxes[j] = m if maxes[j] is None else jnp.maximum(maxes[j], m)
        outs = []
        for j in range(group):
            mx = jnp.max(maxes[j], axis=0, keepdims=True)
            pv = None
            for rc in range(0, width, DSA_ROWS):
                p = jnp.exp2(logit_ref[j, rc:rc + DSA_ROWS, :] - mx).astype(bf16)
                part = jnp.dot(vt_ref[0, g * VT_ROWS:(g + 1) * VT_ROWS, rc:rc + DSA_ROWS], p,
                               preferred_element_type=f32)
                pv = part if pv is None else pv + part
            outs.append(pv[:HEAD_DIM] * (1.0 / pv[HEAD_DIM:HEAD_DIM + 1]))
        for pair in range(group // 2):
            both = jnp.concatenate(outs[2 * pair:2 * pair + 2], axis=0).T
            col = (g * group // 2 + pair) * LANES
            o_ref[:, col:col + LANES] = both.astype(bf16)


def _dsa_attn_kernel(q_ref, qi_ref, wit_ref, ki_ref, k_ref, vt_ref, o_ref,
                     bias_ref, logit_ref, *, seq, top_k):
    i = pl.program_id(1)
    blocks_per_class = DSA_CLS // DSA_TQ
    for cls in range(seq // DSA_CLS):
        @pl.when(i // blocks_per_class == cls)
        def _(cls=cls):
            _dsa_attn_body(q_ref, qi_ref, wit_ref, ki_ref, k_ref, vt_ref, o_ref,
                           bias_ref, logit_ref, width=DSA_CLS * (cls + 1), top_k=top_k,
                           t0=i * DSA_TQ)


def _dsa_attn(q, qi, wit, ki, k, vt, batch, seq):
    nq = seq // DSA_TQ
    top_k = min(TOPK_MAX, seq // 4)
    qrow = lambda n: pl.BlockSpec((DSA_TQ, n), lambda b, i: (b * nq + i, 0))
    full = lambda n: pl.BlockSpec((1, seq, n), lambda b, i: (b, 0, 0))
    return pl.pallas_call(
        functools.partial(_dsa_attn_kernel, seq=seq, top_k=top_k),
        grid=(batch, nq),
        in_specs=[qrow(ATT_HEADS * LANES), qrow(IDX_HEADS * LANES),
                  pl.BlockSpec((8, DSA_TQ), lambda b, i: (0, b * nq + i)),
                  full(LANES), full(ATT_KV_HEADS * LANES),
                  pl.BlockSpec((1, ATT_KV_HEADS * VT_ROWS, seq), lambda b, i: (b, 0, 0))],
        out_specs=qrow(ATT_HEADS * HEAD_DIM),
        out_shape=jax.ShapeDtypeStruct((batch * seq, ATT_HEADS * HEAD_DIM), bf16),
        scratch_shapes=[pltpu.VMEM((seq, DSA_TQ), f32),
                        pltpu.VMEM((ATT_HEADS // ATT_KV_HEADS, seq, DSA_TQ), f32)],
        compiler_params=_cparams("parallel", "arbitrary"),
        name="dsa_attn",
    )(q, qi, wit, ki, k, vt)


RW_TM = 256
LORA_PAD = 128
GATE_PAD = 256


def _rwkv_pre_kernel(x_ref, xp_ref, g_ref, mu_ref, wr_ref, wk_ref, wv_ref,
                     w1_ref, w2_ref, a1_ref, a2_ref, g1_ref, g2_ref,
                     w0_ref, a0_ref, kk_ref, ka_ref, rk_ref, tri_ref, hred_ref, hexp_ref,
                     at_ref, bt_ref, kt_ref, rt_ref, v_ref, gc_ref, bonus_ref, gate_ref,
                     lc_scr, *, seq):
    i = pl.program_id(0)
    g = g_ref[...]
    h = _rms(x_ref[...], g)
    hp = _rms(xp_ref[...], g)[7:8, :]
    hp = jnp.where((i * RW_TM) % seq == 0, 0.0, hp)
    rowi = lax.broadcasted_iota(jnp.int32, (RW_TM, 1), 0)
    hs = jnp.where(rowi == 0, hp, pltpu.roll(h, 1, 0))
    xx = hs - h
    mu = mu_ref[...]

    def mix(c):
        return (h + xx * mu[c:c + 1, :]).astype(bf16)

    r = jnp.dot(mix(0), wr_ref[...], preferred_element_type=f32)
    k = jnp.dot(mix(1), wk_ref[...], preferred_element_type=f32)
    v = jnp.dot(mix(2), wv_ref[...], preferred_element_type=f32)

    wl = w0_ref[...] + _bdot(jnp.tanh(jnp.dot(mix(3), w1_ref[...], preferred_element_type=f32)),
                             w2_ref[...])
    nwl = -wl
    softplus = jnp.maximum(nwl, 0.0) + jnp.log1p(jnp.exp(-jnp.abs(nwl)))
    ld = -jnp.exp(-softplus - 0.5)
    a = jax.nn.sigmoid(a0_ref[...] + _bdot(jnp.dot(mix(4), a1_ref[...],
                                                   preferred_element_type=f32), a2_ref[...]))
    gate_ref[...] = _bdot(jax.nn.sigmoid(jnp.dot(mix(5), g1_ref[...],
                                                 preferred_element_type=f32)), g2_ref[...])

    hred = hred_ref[...]
    hexp = hexp_ref[...]
    z = k * kk_ref[...]
    kk = z * lax.rsqrt(jnp.maximum(_head_sum(z * z, hred, hexp), 1e-24))
    k2 = k * (1.0 + (a - 1.0) * ka_ref[...])
    bonus_ref[...] = _head_sum(r * k2 * rk_ref[...], hred, hexp) * v

    lc = _split_dot_left(tri_ref[...], ld, 2)
    lc_scr[...] = lc
    for c in range(RW_TM // CHUNK):
        last = lc_scr[c * CHUNK + CHUNK - 1:c * CHUNK + CHUNK, :]
        gc_ref[c * 8:(c + 1) * 8, :] = jnp.broadcast_to(jnp.exp(last), (8, D_MODEL))
    einv = jnp.exp(-lc)
    at_ref[...] = -kk * jnp.exp(lc - ld)
    bt_ref[...] = kk * a * einv
    kt_ref[...] = k2 * einv
    rt_ref[...] = r * jnp.exp(lc)
    v_ref[...] = v


def _split_dot_left(a, b, parts):
    out = None
    rem = b
    for p in range(parts):
        piece = rem.astype(bf16)
        term = jnp.dot(a, piece, preferred_element_type=f32)
        out = term if out is None else out + term
        if p + 1 < parts:
            rem = rem - piece.astype(f32)
    return out


def _rwkv_pre(x2, g, mu8, wr, wk, wv, w1, w2, a1, a2, g1, g2, w0, a0, k_k, k_a, r_k, tri, hred, hexp,
              seq):
    m = x2.shape[0]
    row = pl.BlockSpec((RW_TM, D_MODEL), lambda i: (i, 0))
    prev = pl.BlockSpec((8, D_MODEL), lambda i: (jnp.maximum(i * (RW_TM // 8) - 1, 0), 0))
    vec = _const_spec((1, D_MODEL))
    gcrow = pl.BlockSpec((RW_TM // CHUNK * 8, D_MODEL), lambda i: (i, 0))
    act = jax.ShapeDtypeStruct((m, D_MODEL), f32)
    return pl.pallas_call(
        functools.partial(_rwkv_pre_kernel, seq=seq),
        grid=(m // RW_TM,),
        in_specs=[row, prev, vec, _const_spec((8, D_MODEL)),
                  _const_spec(wr.shape), _const_spec(wk.shape), _const_spec(wv.shape),
                  _const_spec(w1.shape), _const_spec(w2.shape),
                  _const_spec(a1.shape), _const_spec(a2.shape),
                  _const_spec(g1.shape), _const_spec(g2.shape),
                  vec, vec, vec, vec, vec,
                  _const_spec(tri.shape), _const_spec(hred.shape), _const_spec(hexp.shape)],
        out_specs=[row, row, row, row, row, gcrow, row, row],
        out_shape=[act, act, act, act, act,
                   jax.ShapeDtypeStruct((m // CHUNK * 8, D_MODEL), f32), act, act],
        scratch_shapes=[pltpu.VMEM((RW_TM, D_MODEL), f32)],
        compiler_params=_cparams("parallel"),
        name="rwkv_pre",
    )(x2, x2, g, mu8, wr, wk, wv, w1, w2, a1, a2, g1, g2, w0, a0, k_k, k_a, r_k, tri, hred, hexp)


RA_GROUP = 128
RA_ROWS = 1024


def _rwkv_chunk_kernel(at_ref, bt_ref, kt_ref, rt_ref, v_ref, gc_ref,
                       q_ref, yl_ref, m_ref, g_ref):
    n = RA_GROUP
    ri = lax.broadcasted_iota(jnp.int32, (n, n), 0)
    ci = lax.broadcasted_iota(jnp.int32, (n, n), 1)
    same = (ri // CHUNK) == (ci // CHUNK)
    strict = same & (ci < ri)
    incl = same & (ci <= ri)
    eye = (ri == ci).astype(f32)
    lane = lax.broadcasted_iota(jnp.int32, (1, LANES), 1)
    low = lane < RWKV_HEAD
    rk = lax.broadcasted_iota(jnp.int32, (RWKV_HEAD, LANES), 0)
    diag = rk == (lax.broadcasted_iota(jnp.int32, (RWKV_HEAD, LANES), 1) % RWKV_HEAD)

    groups = []
    for gi in range(RA_ROWS // n):
        rows = slice(gi * n, (gi + 1) * n)
        at, bt, kt, rt, v = (r[rows, :] for r in (at_ref, bt_ref, kt_ref, rt_ref, v_ref))
        groups.append(dict(at=at, bt=bt, kt=kt, rt=rt, v=v, vb=v.astype(bf16),
                           ar=jnp.concatenate([at, rt], axis=0),
                           bk=jnp.concatenate([bt, kt], axis=0).astype(bf16)))
    probs = [(gr, hl) for gr in groups for hl in (low, jnp.logical_not(low))]

    gs = [lax.dot_general(jnp.where(hl, gr["ar"], 0.0).astype(bf16), gr["bk"], NT,
                          preferred_element_type=f32) for gr, hl in probs]
    aab = [jnp.where(strict, g[:n, :n], 0.0) for g in gs]
    aak = [jnp.where(strict, g[:n, n:], 0.0) for g in gs]
    arb = [jnp.where(incl, g[n:, :n], 0.0) for g in gs]
    ark = [jnp.where(incl, g[n:, n:], 0.0) for g in gs]

    ts = [eye + a for a in aab]
    ps = aab
    for _ in range((CHUNK - 1).bit_length() - 1):
        ps = [_bdot(p, p) for p in ps]
        ts = [t + _bdot(t, p) for t, p in zip(ts, ps)]

    akv = [_bdot(a, gr["vb"]) for a, (gr, _) in zip(aak, probs)]
    tw = [_bdot(t, jnp.concatenate([gr["at"], x], axis=1))
          for t, x, (gr, _) in zip(ts, akv, probs)]
    qy = [_bdot(a, w) for a, w in zip(arb, tw)]
    rkv = [_bdot(a, gr["vb"]) for a, (gr, _) in zip(ark, probs)]

    for gi, gr in enumerate(groups):
        lo, hi = 2 * gi, 2 * gi + 1
        w_hat = jnp.where(low, tw[lo][:, :LANES], tw[hi][:, :LANES])
        u_loc = jnp.where(low, tw[lo][:, LANES:], tw[hi][:, LANES:])
        base = gi * n
        q_ref[base:base + n, :] = gr["rt"] + jnp.where(low, qy[lo][:, :LANES], qy[hi][:, :LANES])
        yl_ref[base:base + n, :] = (jnp.where(low, qy[lo][:, LANES:], qy[hi][:, LANES:])
                                    + jnp.where(low, rkv[lo], rkv[hi]))
        for c in range(n // CHUNK):
            rows = slice(c * CHUNK, (c + 1) * CHUNK)
            ch = base // CHUNK + c
            gc = gc_ref[ch * 8:ch * 8 + 1, :]
            bh = gr["bt"][rows] * gc
            kh = gr["kt"][rows] * gc
            pm = _bdot(w_hat[rows], bh, TN)
            pg = _bdot(jnp.concatenate([u_loc[rows], gr["v"][rows]], axis=0),
                       jnp.concatenate([bh, kh], axis=0), TN)
            out = slice(base + c * CHUNK, base + (c + 1) * CHUNK)
            m_ref[out, :] = (jnp.where(low, pm[:RWKV_HEAD], pm[RWKV_HEAD:])
                             + jnp.where(diag, gc, 0.0))
            g_ref[out, :] = jnp.where(low, pg[:RWKV_HEAD], pg[RWKV_HEAD:])


def _rwkv_chunk(at, bt, kt, rt, v, gc):
    m = at.shape[0]
    blk = pl.BlockSpec((RA_ROWS, LANES), lambda i, p: (i, p))
    gblk = pl.BlockSpec((RA_ROWS // CHUNK * 8, LANES), lambda i, p: (i, p))
    act = jax.ShapeDtypeStruct((m, D_MODEL), f32)
    return pl.pallas_call(
        _rwkv_chunk_kernel,
        grid=(m // RA_ROWS, D_MODEL // LANES),
        in_specs=[blk, blk, blk, blk, blk, gblk],
        out_specs=[blk, blk, blk, blk],
        out_shape=[act, act, act, act],
        compiler_params=_cparams("parallel", "parallel"),
        name="rwkv_chunk",
    )(at, bt, kt, rt, v, gc)


SCAN_LANES = 512


def _rwkv_scan_kernel(q_ref, yl_ref, m_ref, g_ref, y_ref, *, seq):
    pairs = SCAN_LANES // LANES
    low = lax.broadcasted_iota(jnp.int32, (1, LANES), 1) < RWKV_HEAD

    def blockdiag(x):
        return jnp.concatenate([jnp.where(low, x, 0.0), jnp.where(low, 0.0, x)], axis=0)

    def body(c, states):
        rows = pl.ds(pl.multiple_of(c * CHUNK, CHUNK), CHUNK)
        new_states = []
        for p in range(pairs):
            cols = slice(p * LANES, (p + 1) * LANES)
            s = states[p]
            y_ref[rows, cols] = yl_ref[rows, cols] + _bdot(q_ref[rows, cols], s, NT)
            mc = blockdiag(m_ref[rows, cols]).astype(bf16)
            s_hi = s.astype(bf16)
            s_lo = (s - s_hi.astype(f32)).astype(bf16)
            sm = (jnp.dot(s_hi, mc, preferred_element_type=f32)
                  + jnp.dot(s_lo, mc, preferred_element_type=f32))
            new_states.append(sm + blockdiag(g_ref[rows, cols]))
        return tuple(new_states)

    init = tuple(jnp.zeros((LANES, LANES), f32) for _ in range(pairs))
    lax.fori_loop(0, seq // CHUNK, body, init)


def _rwkv_scan(q, yl, mm, gg, batch, seq):
    blk = pl.BlockSpec((seq, SCAN_LANES), lambda b, p: (b, p))
    return pl.pallas_call(
        functools.partial(_rwkv_scan_kernel, seq=seq),
        grid=(batch, D_MODEL // SCAN_LANES),
        in_specs=[blk, blk, blk, blk],
        out_specs=blk,
        out_shape=jax.ShapeDtypeStruct((batch * seq, D_MODEL), f32),
        compiler_params=_cparams("parallel", "parallel"),
        name="rwkv_scan",
    )(q, yl, mm, gg)


def _rwkv_post_kernel(y_ref, bonus_ref, gate_ref, x_ref, lw_ref, lb_ref, hred_ref, hexp_ref,
                      wo_ref, o_ref):
    y = y_ref[...]
    hred = hred_ref[...]
    hexp = hexp_ref[...]
    mean = _head_sum(y, hred, hexp) * (1.0 / RWKV_HEAD)
    yc = y - mean
    var = _head_sum(yc * yc, hred, hexp) * (1.0 / RWKV_HEAD)
    yn = yc * lax.rsqrt(var + LNX_EPS) * lw_ref[...] + lb_ref[...]
    out = (yn + bonus_ref[...]) * gate_ref[...]
    o_ref[...] = x_ref[...] + jnp.dot(out.astype(bf16), wo_ref[...], preferred_element_type=f32)


def _rwkv_post(y, bonus, gate, x2, lw, lb, hred, hexp, wo):
    m = y.shape[0]
    row = pl.BlockSpec((RW_TM, D_MODEL), lambda i: (i, 0))
    vec = _const_spec((1, D_MODEL))
    return pl.pallas_call(
        _rwkv_post_kernel,
        grid=(m // RW_TM,),
        in_specs=[row, row, row, row, vec, vec, _const_spec(hred.shape), _const_spec(hexp.shape),
                  _const_spec(wo.shape)],
        out_specs=row,
        out_shape=jax.ShapeDtypeStruct((m, D_MODEL), f32),
        compiler_params=_cparams("parallel"),
        name="rwkv_post",
    )(y, bonus, gate, x2, lw, lb, hred, hexp, wo)


def _dup_heads(w, n_heads):
    w = w.reshape(w.shape[0], n_heads, 1, HEAD_DIM)
    return jnp.broadcast_to(w, (w.shape[0], n_heads, 2, HEAD_DIM)).reshape(w.shape[0], n_heads * LANES)


def _pad_cols(w, n):
    return jnp.pad(w, ((0, 0), (0, n - w.shape[1])))


def _pad_rows(w, n):
    return jnp.pad(w, ((0, n - w.shape[0]), (0, 0)))


def _rope_tables(seq):
    inv = 1.0 / (ROPE_THETA ** (jnp.arange(0, HEAD_DIM, 2, dtype=f32) / HEAD_DIM))
    ang = jnp.arange(seq, dtype=f32)[:, None] * inv[None, :]
    cos, sin = jnp.cos(ang), jnp.sin(ang)
    return jnp.tile(jnp.concatenate([cos, cos], 1), (1, 2)), jnp.tile(jnp.concatenate([-sin, sin], 1), (1, 2))


def _dsa_layer(x2, g, w_in, w_o, batch, seq):
    o0 = ATT_HEADS * HEAD_DIM
    o1 = o0 + ATT_KV_HEADS * HEAD_DIM
    o2 = o1 + ATT_KV_HEADS * HEAD_DIM
    o3 = o2 + IDX_HEADS * HEAD_DIM
    o4 = o3 + HEAD_DIM
    w_in = w_in.astype(bf16)
    wq = w_in[:, :o0]
    wk2 = _dup_heads(w_in[:, o0:o1], ATT_KV_HEADS)
    wv = w_in[:, o1:o2].T
    wqi = w_in[:, o2:o3]
    wki2 = _dup_heads(w_in[:, o3:o4], 1)
    wwi = _pad_cols(w_in[:, o4:], LANES).T
    cos, sin = _rope_tables(seq)
    q, k, vt, qi, ki, wit = _dsa_proj(x2, g, wq, wk2, wv, wqi, wki2, wwi, cos, sin, batch, seq)
    o = _dsa_attn(q, qi, wit, ki, k, vt, batch, seq)
    return _resproj(o, w_o.astype(bf16), x2)


def _rwkv_layer(x2, g, mu, w_rkv, w0, w1, w2, a0, a1, a2, g1, g2, k_k, k_a, r_k, lnx_w, lnx_b, w_o,
                batch, seq):
    vec = lambda p: p.reshape(1, D_MODEL)
    w_rkv = w_rkv.astype(bf16)
    ri = jnp.arange(RW_TM)
    tri = ((ri[:, None] >= ri[None, :]) & (ri[:, None] // CHUNK == ri[None, :] // CHUNK)).astype(bf16)
    di = jnp.arange(D_MODEL)
    hred = (di[:, None] // RWKV_HEAD == jnp.arange(LANES)[None, :]).astype(bf16)
    hexp = hred.T
    at, bt, kt, rt, v, gc, bonus, gate = _rwkv_pre(
        x2, g, _pad_rows(mu, 8), w_rkv[0], w_rkv[1], w_rkv[2],
        _pad_cols(w1, LORA_PAD).astype(bf16), _pad_rows(w2, LORA_PAD).astype(bf16),
        _pad_cols(a1, LORA_PAD).astype(bf16), _pad_rows(a2, LORA_PAD).astype(bf16),
        _pad_cols(g1, GATE_PAD).astype(bf16), _pad_rows(g2, GATE_PAD).astype(bf16),
        vec(w0), vec(a0), vec(k_k), vec(k_a), vec(r_k), tri, hred, hexp, seq)
    q, yl, mm, gg = _rwkv_chunk(at, bt, kt, rt, v, gc)
    y = _rwkv_scan(q, yl, mm, gg, batch, seq)
    return _rwkv_post(y, bonus, gate, x2, vec(lnx_w), vec(lnx_b), hred, hexp, w_o.astype(bf16))


def kernel(x, mixer_norm, mlp_norm, mlp_w_up, mlp_w_down, final_norm, dsa_w_in, dsa_w_o, rwkv_mu, rwkv_w_rkv, rwkv_w0, rwkv_w1, rwkv_w2, rwkv_a0, rwkv_a1, rwkv_a2, rwkv_g1, rwkv_g2, rwkv_k_k, rwkv_k_a, rwkv_r_k, rwkv_lnx_w, rwkv_lnx_b, rwkv_w_o):
    batch, seq, _ = x.shape
    depth = mixer_norm.shape[0]
    x2 = x.reshape(batch * seq, D_MODEL)
    fg = final_norm.reshape(1, D_MODEL)
    for i in range(depth):
        g = mixer_norm[i].reshape(1, D_MODEL)
        j = i // 2
        if i % 2 == 0:
            x2 = _dsa_layer(x2, g, dsa_w_in[j], dsa_w_o[j], batch, seq)
        else:
            x2 = _rwkv_layer(x2, g, rwkv_mu[j], rwkv_w_rkv[j], rwkv_w0[j], rwkv_w1[j], rwkv_w2[j],
                             rwkv_a0[j], rwkv_a1[j], rwkv_a2[j], rwkv_g1[j], rwkv_g2[j],
                             rwkv_k_k[j], rwkv_k_a[j], rwkv_r_k[j], rwkv_lnx_w[j], rwkv_lnx_b[j],
                             rwkv_w_o[j], batch, seq)
        x2 = _mlp(x2, mlp_norm[i].reshape(1, D_MODEL), mlp_w_up[i].astype(bf16),
                  mlp_w_down[i].astype(bf16), fg, final=(i == depth - 1))
    return x2.reshape(batch, seq, D_MODEL)
```

```python
import functools

import jax
import jax.numpy as jnp
from jax import lax
from jax.experimental import pallas as pl
from jax.experimental.pallas import tpu as pltpu

f32 = jnp.float32
bf16 = jnp.bfloat16

D_MODEL = 1024
D_FF = 4 * D_MODEL
NORM_EPS = 1e-6

ATT_HEADS = 16
ATT_KV_HEADS = 4
HEAD_DIM = 64
IDX_HEADS = 8
TOPK_MAX = 256
ROPE_THETA = 10000.0

RWKV_HEAD = 64
RWKV_HEADS = 16
LNX_EPS = 64e-5
CHUNK = 64

LANES = 128
VMEM_LIMIT = 56 * 1024 * 1024

NT = (((1,), (1,)), ((), ()))
TN = (((0,), (0,)), ((), ()))


def _cparams(*sem):
    return pltpu.CompilerParams(dimension_semantics=sem, vmem_limit_bytes=VMEM_LIMIT)


def _rms(x, g):
    return x * lax.rsqrt(jnp.mean(x * x, axis=-1, keepdims=True) + NORM_EPS) * g


def _bdot(a, b, dims=None):
    a = a.astype(bf16)
    b = b.astype(bf16)
    if dims is None:
        return jnp.dot(a, b, preferred_element_type=f32)
    return lax.dot_general(a, b, dims, preferred_element_type=f32)


def _split_dot(a, b, parts):
    out = None
    rem = a
    for p in range(parts):
        piece = rem.astype(bf16)
        term = jnp.dot(piece, b, preferred_element_type=f32)
        out = term if out is None else out + term
        if p + 1 < parts:
            rem = rem - piece.astype(f32)
    return out


def _head_sum(x, hred, hexp):
    red = jnp.dot(x.astype(bf16), hred, preferred_element_type=f32)
    return _split_dot(red, hexp, 2)


def _const_spec(shape):
    return pl.BlockSpec(shape, lambda *_: (0,) * len(shape))


MLP_TM = 512
MLP_FCH = 512


def _mlp_kernel(x_ref, g_ref, wu_ref, wd_ref, fg_ref, o_ref, *, final):
    x = x_ref[...]
    xn = _rms(x, g_ref[...]).astype(bf16)
    o_ref[...] = x
    for f in range(0, D_FF, MLP_FCH):
        u = jnp.dot(xn, wu_ref[:, f:f + MLP_FCH], preferred_element_type=f32)
        u = jnp.maximum(u, 0.0)
        o_ref[...] += jnp.dot((u * u).astype(bf16), wd_ref[f:f + MLP_FCH, :],
                              preferred_element_type=f32)
    if final:
        o_ref[...] = _rms(o_ref[...], fg_ref[...])


def _mlp(x2, g, wu, wd, fg, final):
    m = x2.shape[0]
    return pl.pallas_call(
        functools.partial(_mlp_kernel, final=final),
        grid=(m // MLP_TM,),
        in_specs=[pl.BlockSpec((MLP_TM, D_MODEL), lambda i: (i, 0)),
                  _const_spec((1, D_MODEL)),
                  _const_spec((D_MODEL, D_FF)),
                  _const_spec((D_FF, D_MODEL)),
                  _const_spec((1, D_MODEL))],
        out_specs=pl.BlockSpec((MLP_TM, D_MODEL), lambda i: (i, 0)),
        out_shape=jax.ShapeDtypeStruct((m, D_MODEL), f32),
        compiler_params=_cparams("parallel"),
        name="mlp",
    )(x2, g, wu, wd, fg)


PROJ_TM = 512


def _resproj_kernel(a_ref, w_ref, r_ref, o_ref):
    o_ref[...] = r_ref[...] + jnp.dot(a_ref[...], w_ref[...], preferred_element_type=f32)


def _resproj(a, w, res):
    m, k = a.shape
    n = w.shape[1]
    return pl.pallas_call(
        _resproj_kernel,
        grid=(m // PROJ_TM,),
        in_specs=[pl.BlockSpec((PROJ_TM, k), lambda i: (i, 0)),
                  _const_spec((k, n)),
                  pl.BlockSpec((PROJ_TM, n), lambda i: (i, 0))],
        out_specs=pl.BlockSpec((PROJ_TM, n), lambda i: (i, 0)),
        out_shape=jax.ShapeDtypeStruct((m, n), f32),
        compiler_params=_cparams("parallel"),
        name="resproj",
    )(a, w, res)


DSA_TM = 256


def _rope(x, cos, sin_signed, first_half):
    fwd = pltpu.roll(x, 32, 1)
    bwd = pltpu.roll(x, 96, 1)
    return x * cos + jnp.where(first_half, bwd, fwd) * sin_signed


LOG2E = 1.4426950408889634


VT_ROWS = 80


def _dsa_proj_kernel(x_ref, g_ref, wq_ref, wk_ref, wv_ref, wqi_ref, wki_ref, wwi_ref,
                     cos_ref, sin_ref,
                     q_ref, k_ref, vt_ref, qi_ref, ki_ref, wit_ref):
    h = _rms(x_ref[...], g_ref[...]).astype(bf16)
    tm = h.shape[0]
    cos = cos_ref[...]
    sin = sin_ref[...]
    lane = lax.broadcasted_iota(jnp.int32, (1, LANES), 1)
    first_half = (lane % HEAD_DIM) < (HEAD_DIM // 2)
    low = lane < HEAD_DIM

    def per_head(x, out_ref, pair, scale):
        xc = _rope(x[:, pair * LANES:(pair + 1) * LANES], cos, sin, first_half) * scale
        swapped = pltpu.roll(xc, HEAD_DIM, 1)
        out_ref[:, (2 * pair) * LANES:(2 * pair + 1) * LANES] = jnp.where(low, xc, 0.0).astype(bf16)
        out_ref[:, (2 * pair + 1) * LANES:(2 * pair + 2) * LANES] = jnp.where(low, swapped, 0.0).astype(bf16)

    q = jnp.dot(h, wq_ref[...], preferred_element_type=f32)
    for pair in range(ATT_HEADS // 2):
        per_head(q, q_ref, pair, HEAD_DIM ** -0.5 * LOG2E)
    qi = jnp.dot(h, wqi_ref[...], preferred_element_type=f32)
    for pair in range(IDX_HEADS // 2):
        per_head(qi, qi_ref, pair, 1.0)

    k2 = jnp.dot(h, wk_ref[...], preferred_element_type=f32)
    for c in range(0, ATT_KV_HEADS * LANES, LANES):
        kc = _rope(k2[:, c:c + LANES], cos, sin, first_half)
        k_ref[0, :, c:c + LANES] = jnp.where(low, kc, 0.0).astype(bf16)
    ki2 = _rope(jnp.dot(h, wki_ref[...], preferred_element_type=f32), cos, sin, first_half)
    ki_ref[0] = jnp.where(low, ki2, 0.0).astype(bf16)

    vt = lax.dot_general(wv_ref[...], h, NT, preferred_element_type=f32)
    extra = (lax.broadcasted_iota(jnp.int32, (VT_ROWS - HEAD_DIM, tm), 0) == 0).astype(f32)
    for g in range(ATT_KV_HEADS):
        vt_ref[0, g * VT_ROWS:(g + 1) * VT_ROWS, :] = jnp.concatenate(
            [vt[g * HEAD_DIM:(g + 1) * HEAD_DIM], extra], axis=0).astype(bf16)
    wit_ref[...] = (lax.dot_general(wwi_ref[...], h, NT, preferred_element_type=f32)
                    * (IDX_HEADS ** -0.5 * HEAD_DIM ** -0.5))


def _dsa_proj(x2, g, wq, wk2, wv, wqi, wki2, wwi, cos, sin, batch, seq):
    m = x2.shape[0]
    nblk = seq // DSA_TM
    row = lambda n: pl.BlockSpec((DSA_TM, n), lambda b, i: (b * nblk + i, 0))
    seqrow = lambda n: pl.BlockSpec((1, DSA_TM, n), lambda b, i: (b, i, 0))
    tab = pl.BlockSpec((DSA_TM, LANES), lambda b, i: (i, 0))
    kvw = ATT_KV_HEADS * LANES
    vtr = ATT_KV_HEADS * VT_ROWS
    return pl.pallas_call(
        _dsa_proj_kernel,
        grid=(batch, nblk),
        in_specs=[row(D_MODEL), _const_spec((1, D_MODEL)),
                  _const_spec(wq.shape), _const_spec(wk2.shape), _const_spec(wv.shape),
                  _const_spec(wqi.shape), _const_spec(wki2.shape), _const_spec(wwi.shape),
                  tab, tab],
        out_specs=[row(ATT_HEADS * LANES), seqrow(kvw),
                   pl.BlockSpec((1, vtr, DSA_TM), lambda b, i: (b, 0, i)),
                   row(IDX_HEADS * LANES), seqrow(LANES),
                   pl.BlockSpec((LANES, DSA_TM), lambda b, i: (0, b * nblk + i))],
        out_shape=[jax.ShapeDtypeStruct((m, ATT_HEADS * LANES), bf16),
                   jax.ShapeDtypeStruct((batch, seq, kvw), bf16),
                   jax.ShapeDtypeStruct((batch, vtr, seq), bf16),
                   jax.ShapeDtypeStruct((m, IDX_HEADS * LANES), bf16),
                   jax.ShapeDtypeStruct((batch, seq, LANES), bf16),
                   jax.ShapeDtypeStruct((LANES, m), f32)],
        compiler_params=_cparams("parallel", "parallel"),
        name="dsa_proj",
    )(x2, g, wq, wk2, wv, wqi, wki2, wwi, cos, sin)


DSA_TQ = 128
DSA_CLS = 512
IDX_GROUP = 4
DSA_ROWS = 256
INT_MIN = -2 ** 31
NEG_INF_KEY = -2139095041


FOLD_ROWS = 64


def _fold_rows(x, op):
    rows = x.shape[0]
    if rows > FOLD_ROWS and rows % FOLD_ROWS == 0:
        acc = x[:FOLD_ROWS]
        for r in range(FOLD_ROWS, rows, FOLD_ROWS):
            acc = op(acc, x[r:r + FOLD_ROWS])
        x = acc
    while x.shape[0] % 16 == 0:
        h = x.shape[0] // 2
        x = op(x[:h], x[h:])
    return x


def _col_count(mask):
    return jnp.sum(_fold_rows(jnp.where(mask, 1.0, 0.0), jnp.add), axis=0, keepdims=True).astype(jnp.int32)


def _stack_heads(ref, first, count):
    return jnp.concatenate([ref[:, (first + j) * LANES:(first + j + 1) * LANES] for j in range(count)],
                           axis=0)


def _dsa_attn_body(q_ref, qi_ref, wit_ref, ki_ref, k_ref, vt_ref, o_ref,
                   bias_ref, logit_ref, *, width, top_k, t0):
    col_t = t0 + lax.broadcasted_iota(jnp.int32, (1, DSA_TQ), 1)
    row_s = lax.broadcasted_iota(jnp.int32, (width, 1), 0)
    causal = row_s <= col_t

    wit = wit_ref[...]
    ki = ki_ref[0, 0:width, :]
    score = jnp.zeros((width, DSA_TQ), f32)
    for first in range(0, IDX_HEADS, IDX_GROUP):
        rel = lax.dot_general(ki, _stack_heads(qi_ref, first, IDX_GROUP), NT,
                              preferred_element_type=f32)
        for j in range(IDX_GROUP):
            hd = first + j
            score = score + wit[hd:hd + 1, :] * jnp.maximum(rel[:, j * DSA_TQ:(j + 1) * DSA_TQ], 0.0)
    score = jnp.where(causal, score, -jnp.inf)

    def key_to_float(k):
        k = jnp.maximum(k, NEG_INF_KEY)
        return lax.bitcast_convert_type(k ^ ((k >> 31) & jnp.int32(0x7FFFFFFF)), f32)

    n_nonneg = _col_count(score >= 0.0)
    start_high = n_nonneg >= top_k
    thr = jnp.where(start_high, jnp.int32(0), jnp.int32(INT_MIN))
    n_ge = jnp.where(start_high, n_nonneg, jnp.int32(width))

    def value_step(i, state):
        thr, n_ge = state
        cand = thr | (jnp.int32(1) << (30 - i))
        cnt = _col_count(score >= key_to_float(cand))
        take = cnt >= top_k
        return jnp.where(take, cand, thr), jnp.where(take, cnt, n_ge)

    thr, n_ge = lax.fori_loop(0, 31, value_step, (thr, n_ge))

    thr_f = key_to_float(thr)
    bias_ref[0:width, :] = jnp.where((score >= thr_f) & causal, 0.0, -jnp.inf)
    overfull = (n_ge != top_k) & (thr > NEG_INF_KEY)

    @pl.when(jnp.max(overfull.astype(f32)) > 0.0)
    def _():
        real_thr = thr > NEG_INF_KEY
        next_f = key_to_float(thr + 1)
        beyond = score >= next_f
        in_bin = (score >= thr_f) & jnp.logical_not(beyond) & real_thr
        offset = jnp.where(in_bin, score - jnp.where(real_thr, thr_f, 0.0), -1.0)
        need = top_k - _col_count(beyond)

        def offset_step(i, okey):
            cand = okey | (jnp.int32(1) << (30 - i))
            cnt = _col_count(offset >= lax.bitcast_convert_type(cand, f32))
            return jnp.where(cnt >= need, cand, okey)

        okey = lax.fori_loop(0, 31, offset_step, jnp.zeros((1, DSA_TQ), jnp.int32))
        othr = lax.bitcast_convert_type(okey, f32)
        above = offset > othr
        tied = offset == othr
        need = need - _col_count(above)
        nbits = (width - 1).bit_length()

        def index_step(i, pos):
            cand = pos | (jnp.int32(1) << (nbits - 1 - i))
            return jnp.where(_col_count(tied & (row_s < cand)) < need, cand, pos)

        pos = lax.fori_loop(0, nbits, index_step, jnp.zeros((1, DSA_TQ), jnp.int32))
        keep = (beyond | above | (tied & (row_s <= pos))) & causal
        bias_ref[0:width, :] = jnp.where(keep, 0.0, -jnp.inf)

    group = ATT_HEADS // ATT_KV_HEADS
    for g in range(ATT_KV_HEADS):
        rq = _stack_heads(q_ref, g * group, group)
        maxes = [None] * group
        for rc in range(0, width, DSA_ROWS):
            logits4 = lax.dot_general(k_ref[0, rc:rc + DSA_ROWS, g * LANES:(g + 1) * LANES], rq, NT,
                                      preferred_element_type=f32)
            b = bias_ref[rc:rc + DSA_ROWS, :]
            for j in range(group):
                logits = logits4[:, j * DSA_TQ:(j + 1) * DSA_TQ] + b
                logit_ref[j, rc:rc + DSA_ROWS, :] = logits
                m = _fold_rows(logits, jnp.maximum)
                maxes[j] = m if maxes[j] is None else jnp.maximum(maxes[j], m)
        outs = []
        for j in range(group):
            mx = jnp.max(maxes[j], axis=0, keepdims=True)
            pv = None
            for rc in range(0, width, DSA_ROWS):
                p = jnp.exp2(logit_ref[j, rc:rc + DSA_ROWS, :] - mx).astype(bf16)
                part = jnp.dot(vt_ref[0, g * VT_ROWS:(g + 1) * VT_ROWS, rc:rc + DSA_ROWS], p,
                               preferred_element_type=f32)
                pv = part if pv is None else pv + part
            outs.append(pv[:HEAD_DIM] * (1.0 / pv[HEAD_DIM:HEAD_DIM + 1]))
        for pair in range(group // 2):
            both = jnp.concatenate(outs[2 * pair:2 * pair + 2], axis=0).T
            col = (g * group // 2 + pair) * LANES
            o_ref[:, col:col + LANES] = both.astype(bf16)


def _dsa_attn_kernel(q_ref, qi_ref, wit_ref, ki_ref, k_ref, vt_ref, o_ref,
                     bias_ref, logit_ref, *, seq, top_k):
    i = pl.program_id(1)
    blocks_per_class = DSA_CLS // DSA_TQ
    for cls in range(seq // DSA_CLS):
        @pl.when(i // blocks_per_class == cls)
        def _(cls=cls):
            _dsa_attn_body(q_ref, qi_ref, wit_ref, ki_ref, k_ref, vt_ref, o_ref,
                           bias_ref, logit_ref, width=DSA_CLS * (cls + 1), top_k=top_k,
                           t0=i * DSA_TQ)


def _dsa_attn(q, qi, wit, ki, k, vt, batch, seq):
    nq = seq // DSA_TQ
    top_k = min(TOPK_MAX, seq // 4)
    qrow = lambda n: pl.BlockSpec((DSA_TQ, n), lambda b, i: (b * nq + i, 0))
    full = lambda n: pl.BlockSpec((1, seq, n), lambda b, i: (b, 0, 0))
    return pl.pallas_call(
        functools.partial(_dsa_attn_kernel, seq=seq, top_k=top_k),
        grid=(batch, nq),
        in_specs=[qrow(ATT_HEADS * LANES), qrow(IDX_HEADS * LANES),
                  pl.BlockSpec((8, DSA_TQ), lambda b, i: (0, b * nq + i)),
                  full(LANES), full(ATT_KV_HEADS * LANES),
                  pl.BlockSpec((1, ATT_KV_HEADS * VT_ROWS, seq), lambda b, i: (b, 0, 0))],
        out_specs=qrow(ATT_HEADS * HEAD_DIM),
        out_shape=jax.ShapeDtypeStruct((batch * seq, ATT_HEADS * HEAD_DIM), bf16),
        scratch_shapes=[pltpu.VMEM((seq, DSA_TQ), f32),
                        pltpu.VMEM((ATT_HEADS // ATT_KV_HEADS, seq, DSA_TQ), f32)],
        compiler_params=_cparams("parallel", "arbitrary"),
        name="dsa_attn",
    )(q, qi, wit, ki, k, vt)


RW_TM = 256
LORA_PAD = 128
GATE_PAD = 256


def _rwkv_pre_kernel(x_ref, xp_ref, g_ref, mu_ref, wr_ref, wk_ref, wv_ref,
                     w1_ref, w2_ref, a1_ref, a2_ref, g1_ref, g2_ref,
                     w0_ref, a0_ref, kk_ref, ka_ref, rk_ref, tri_ref, hred_ref, hexp_ref,
                     at_ref, bt_ref, kt_ref, rt_ref, v_ref, gc_ref, bonus_ref, gate_ref,
                     lc_scr, *, seq):
    i = pl.program_id(0)
    g = g_ref[...]
    h = _rms(x_ref[...], g)
    hp = _rms(xp_ref[...], g)[7:8, :]
    hp = jnp.where((i * RW_TM) % seq == 0, 0.0, hp)
    rowi = lax.broadcasted_iota(jnp.int32, (RW_TM, 1), 0)
    hs = jnp.where(rowi == 0, hp, pltpu.roll(h, 1, 0))
    xx = hs - h
    mu = mu_ref[...]

    def mix(c):
        return (h + xx * mu[c:c + 1, :]).astype(bf16)

    r = jnp.dot(mix(0), wr_ref[...], preferred_element_type=f32)
    k = jnp.dot(mix(1), wk_ref[...], preferred_element_type=f32)
    v = jnp.dot(mix(2), wv_ref[...], preferred_element_type=f32)

    wl = w0_ref[...] + _bdot(jnp.tanh(jnp.dot(mix(3), w1_ref[...], preferred_element_type=f32)),
                             w2_ref[...])
    nwl = -wl
    softplus = jnp.maximum(nwl, 0.0) + jnp.log1p(jnp.exp(-jnp.abs(nwl)))
    ld = -jnp.exp(-softplus - 0.5)
    a = jax.nn.sigmoid(a0_ref[...] + _bdot(jnp.dot(mix(4), a1_ref[...],
                                                   preferred_element_type=f32), a2_ref[...]))
    gate_ref[...] = _bdot(jax.nn.sigmoid(jnp.dot(mix(5), g1_ref[...],
                                                 preferred_element_type=f32)), g2_ref[...])

    hred = hred_ref[...]
    hexp = hexp_ref[...]
    z = k * kk_ref[...]
    kk = z * lax.rsqrt(jnp.maximum(_head_sum(z * z, hred, hexp), 1e-24))
    k2 = k * (1.0 + (a - 1.0) * ka_ref[...])
    bonus_ref[...] = _head_sum(r * k2 * rk_ref[...], hred, hexp) * v

    lc = _split_dot_left(tri_ref[...], ld, 2)
    lc_scr[...] = lc
    for c in range(RW_TM // CHUNK):
        last = lc_scr[c * CHUNK + CHUNK - 1:c * CHUNK + CHUNK, :]
        gc_ref[c * 8:(c + 1) * 8, :] = jnp.broadcast_to(jnp.exp(last), (8, D_MODEL))
    einv = jnp.exp(-lc)
    at_ref[...] = -kk * jnp.exp(lc - ld)
    bt_ref[...] = kk * a * einv
    kt_ref[...] = k2 * einv
    rt_ref[...] = r * jnp.exp(lc)
    v_ref[...] = v


def _split_dot_left(a, b, parts):
    out = None
    rem = b
    for p in range(parts):
        piece = rem.astype(bf16)
        term = jnp.dot(a, piece, preferred_element_type=f32)
        out = term if out is None else out + term
        if p + 1 < parts:
            rem = rem - piece.astype(f32)
    return out


def _rwkv_pre(x2, g, mu8, wr, wk, wv, w1, w2, a1, a2, g1, g2, w0, a0, k_k, k_a, r_k, tri, hred, hexp,
              seq):
    m = x2.shape[0]
    row = pl.BlockSpec((RW_TM, D_MODEL), lambda i: (i, 0))
    prev = pl.BlockSpec((8, D_MODEL), lambda i: (jnp.maximum(i * (RW_TM // 8) - 1, 0), 0))
    vec = _const_spec((1, D_MODEL))
    gcrow = pl.BlockSpec((RW_TM // CHUNK * 8, D_MODEL), lambda i: (i, 0))
    act = jax.ShapeDtypeStruct((m, D_MODEL), f32)
    return pl.pallas_call(
        functools.partial(_rwkv_pre_kernel, seq=seq),
        grid=(m // RW_TM,),
        in_specs=[row, prev, vec, _const_spec((8, D_MODEL)),
                  _const_spec(wr.shape), _const_spec(wk.shape), _const_spec(wv.shape),
                  _const_spec(w1.shape), _const_spec(w2.shape),
                  _const_spec(a1.shape), _const_spec(a2.shape),
                  _const_spec(g1.shape), _const_spec(g2.shape),
                  vec, vec, vec, vec, vec,
                  _const_spec(tri.shape), _const_spec(hred.shape), _const_spec(hexp.shape)],
        out_specs=[row, row, row, row, row, gcrow, row, row],
        out_shape=[act, act, act, act, act,
                   jax.ShapeDtypeStruct((m // CHUNK * 8, D_MODEL), f32), act, act],
        scratch_shapes=[pltpu.VMEM((RW_TM, D_MODEL), f32)],
        compiler_params=_cparams("parallel"),
        name="rwkv_pre",
    )(x2, x2, g, mu8, wr, wk, wv, w1, w2, a1, a2, g1, g2, w0, a0, k_k, k_a, r_k, tri, hred, hexp)


RA_GROUP = 128
RA_ROWS = 1024


def _rwkv_chunk_kernel(at_ref, bt_ref, kt_ref, rt_ref, v_ref, gc_ref,
                       q_ref, yl_ref, m_ref, g_ref):
    n = RA_GROUP
    ri = lax.broadcasted_iota(jnp.int32, (n, n), 0)
    ci = lax.broadcasted_iota(jnp.int32, (n, n), 1)
    same = (ri // CHUNK) == (ci // CHUNK)
    strict = same & (ci < ri)
    incl = same & (ci <= ri)
    eye = (ri == ci).astype(f32)
    lane = lax.broadcasted_iota(jnp.int32, (1, LANES), 1)
    low = lane < RWKV_HEAD
    rk = lax.broadcasted_iota(jnp.int32, (RWKV_HEAD, LANES), 0)
    diag = rk == (lax.broadcasted_iota(jnp.int32, (RWKV_HEAD, LANES), 1) % RWKV_HEAD)

    groups = []
    for gi in range(RA_ROWS // n):
        rows = slice(gi * n, (gi + 1) * n)
        at, bt, kt, rt, v = (r[rows, :] for r in (at_ref, bt_ref, kt_ref, rt_ref, v_ref))
        groups.append(dict(at=at, bt=bt, kt=kt, rt=rt, v=v, vb=v.astype(bf16),
                           ar=jnp.concatenate([at, rt], axis=0),
                           bk=jnp.concatenate([bt, kt], axis=0).astype(bf16)))
    probs = [(gr, hl) for gr in groups for hl in (low, jnp.logical_not(low))]

    gs = [lax.dot_general(jnp.where(hl, gr["ar"], 0.0).astype(bf16), gr["bk"], NT,
                          preferred_element_type=f32) for gr, hl in probs]
    aab = [jnp.where(strict, g[:n, :n], 0.0) for g in gs]
    aak = [jnp.where(strict, g[:n, n:], 0.0) for g in gs]
    arb = [jnp.where(incl, g[n:, :n], 0.0) for g in gs]
    ark = [jnp.where(incl, g[n:, n:], 0.0) for g in gs]

    ts = [eye + a for a in aab]
    ps = aab
    for _ in range((CHUNK - 1).bit_length() - 1):
        ps = [_bdot(p, p) for p in ps]
        ts = [t + _bdot(t, p) for t, p in zip(ts, ps)]

    akv = [_bdot(a, gr["vb"]) for a, (gr, _) in zip(aak, probs)]
    tw = [_bdot(t, jnp.concatenate([gr["at"], x], axis=1))
          for t, x, (gr, _) in zip(ts, akv, probs)]
    qy = [_bdot(a, w) for a, w in zip(arb, tw)]
    rkv = [_bdot(a, gr["vb"]) for a, (gr, _) in zip(ark, probs)]

    for gi, gr in enumerate(groups):
        lo, hi = 2 * gi, 2 * gi + 1
        w_hat = jnp.where(low, tw[lo][:, :LANES], tw[hi][:, :LANES])
        u_loc = jnp.where(low, tw[lo][:, LANES:], tw[hi][:, LANES:])
        base = gi * n
        q_ref[base:base + n, :] = gr["rt"] + jnp.where(low, qy[lo][:, :LANES], qy[hi][:, :LANES])
        yl_ref[base:base + n, :] = (jnp.where(low, qy[lo][:, LANES:], qy[hi][:, LANES:])
                                    + jnp.where(low, rkv[lo], rkv[hi]))
        for c in range(n // CHUNK):
            rows = slice(c * CHUNK, (c + 1) * CHUNK)
            ch = base // CHUNK + c
            gc = gc_ref[ch * 8:ch * 8 + 1, :]
            bh = gr["bt"][rows] * gc
            kh = gr["kt"][rows] * gc
            pm = _bdot(w_hat[rows], bh, TN)
            pg = _bdot(jnp.concatenate([u_loc[rows], gr["v"][rows]], axis=0),
                       jnp.concatenate([bh, kh], axis=0), TN)
            out = slice(base + c * CHUNK, base + (c + 1) * CHUNK)
            m_ref[out, :] = (jnp.where(low, pm[:RWKV_HEAD], pm[RWKV_HEAD:])
                             + jnp.where(diag, gc, 0.0))
            g_ref[out, :] = jnp.where(low, pg[:RWKV_HEAD], pg[RWKV_HEAD:])


def _rwkv_chunk(at, bt, kt, rt, v, gc):
    m = at.shape[0]
    blk = pl.BlockSpec((RA_ROWS, LANES), lambda i, p: (i, p))
    gblk = pl.BlockSpec((RA_ROWS // CHUNK * 8, LANES), lambda i, p: (i, p))
    act = jax.ShapeDtypeStruct((m, D_MODEL), f32)
    return pl.pallas_call(
        _rwkv_chunk_kernel,
        grid=(m // RA_ROWS, D_MODEL // LANES),
        in_specs=[blk, blk, blk, blk, blk, gblk],
        out_specs=[blk, blk, blk, blk],
        out_shape=[act, act, act, act],
        compiler_params=_cparams("parallel", "parallel"),
        name="rwkv_chunk",
    )(at, bt, kt, rt, v, gc)


SCAN_LANES = 512


def _rwkv_scan_kernel(q_ref, yl_ref, m_ref, g_ref, y_ref, *, seq):
    pairs = SCAN_LANES // LANES
    low = lax.broadcasted_iota(jnp.int32, (1, LANES), 1) < RWKV_HEAD

    def blockdiag(x):
        return jnp.concatenate([jnp.where(low, x, 0.0), jnp.where(low, 0.0, x)], axis=0)

    def body(c, states):
        rows = pl.ds(pl.multiple_of(c * CHUNK, CHUNK), CHUNK)
        new_states = []
        for p in range(pairs):
            cols = slice(p * LANES, (p + 1) * LANES)
            s = states[p]
            y_ref[rows, cols] = yl_ref[rows, cols] + _bdot(q_ref[rows, cols], s, NT)
            mc = blockdiag(m_ref[rows, cols]).astype(bf16)
            s_hi = s.astype(bf16)
            s_lo = (s - s_hi.astype(f32)).astype(bf16)
            sm = (jnp.dot(s_hi, mc, preferred_element_type=f32)
                  + jnp.dot(s_lo, mc, preferred_element_type=f32))
            new_states.append(sm + blockdiag(g_ref[rows, cols]))
        return tuple(new_states)

    init = tuple(jnp.zeros((LANES, LANES), f32) for _ in range(pairs))
    lax.fori_loop(0, seq // CHUNK, body, init)


def _rwkv_scan(q, yl, mm, gg, batch, seq):
    blk = pl.BlockSpec((seq, SCAN_LANES), lambda b, p: (b, p))
    return pl.pallas_call(
        functools.partial(_rwkv_scan_kernel, seq=seq),
        grid=(batch, D_MODEL // SCAN_LANES),
        in_specs=[blk, blk, blk, blk],
        out_specs=blk,
        out_shape=jax.ShapeDtypeStruct((batch * seq, D_MODEL), f32),
        compiler_params=_cparams("parallel", "parallel"),
        name="rwkv_scan",
    )(q, yl, mm, gg)


def _rwkv_post_kernel(y_ref, bonus_ref, gate_ref, x_ref, lw_ref, lb_ref, hred_ref, hexp_ref,
                      wo_ref, o_ref):
    y = y_ref[...]
    hred = hred_ref[...]
    hexp = hexp_ref[...]
    mean = _head_sum(y, hred, hexp) * (1.0 / RWKV_HEAD)
    yc = y - mean
    var = _head_sum(yc * yc, hred, hexp) * (1.0 / RWKV_HEAD)
    yn = yc * lax.rsqrt(var + LNX_EPS) * lw_ref[...] + lb_ref[...]
    out = (yn + bonus_ref[...]) * gate_ref[...]
    o_ref[...] = x_ref[...] + jnp.dot(out.astype(bf16), wo_ref[...], preferred_element_type=f32)


def _rwkv_post(y, bonus, gate, x2, lw, lb, hred, hexp, wo):
    m = y.shape[0]
    row = pl.BlockSpec((RW_TM, D_MODEL), lambda i: (i, 0))
    vec = _const_spec((1, D_MODEL))
    return pl.pallas_call(
        _rwkv_post_kernel,
        grid=(m // RW_TM,),
        in_specs=[row, row, row, row, vec, vec, _const_spec(hred.shape), _const_spec(hexp.shape),
                  _const_spec(wo.shape)],
        out_specs=row,
        out_shape=jax.ShapeDtypeStruct((m, D_MODEL), f32),
        compiler_params=_cparams("parallel"),
        name="rwkv_post",
    )(y, bonus, gate, x2, lw, lb, hred, hexp, wo)


def _dup_heads(w, n_heads):
    w = w.reshape(w.shape[0], n_heads, 1, HEAD_DIM)
    return jnp.broadcast_to(w, (w.shape[0], n_heads, 2, HEAD_DIM)).reshape(w.shape[0], n_heads * LANES)


def _pad_cols(w, n):
    return jnp.pad(w, ((0, 0), (0, n - w.shape[1])))


def _pad_rows(w, n):
    return jnp.pad(w, ((0, n - w.shape[0]), (0, 0)))


def _rope_tables(seq):
    inv = 1.0 / (ROPE_THETA ** (jnp.arange(0, HEAD_DIM, 2, dtype=f32) / HEAD_DIM))
    ang = jnp.arange(seq, dtype=f32)[:, None] * inv[None, :]
    cos, sin = jnp.cos(ang), jnp.sin(ang)
    return jnp.tile(jnp.concatenate([cos, cos], 1), (1, 2)), jnp.tile(jnp.concatenate([-sin, sin], 1), (1, 2))


def _dsa_layer(x2, g, w_in, w_o, batch, seq):
    o0 = ATT_HEADS * HEAD_DIM
    o1 = o0 + ATT_KV_HEADS * HEAD_DIM
    o2 = o1 + ATT_KV_HEADS * HEAD_DIM
    o3 = o2 + IDX_HEADS * HEAD_DIM
    o4 = o3 + HEAD_DIM
    w_in = w_in.astype(bf16)
    wq = w_in[:, :o0]
    wk2 = _dup_heads(w_in[:, o0:o1], ATT_KV_HEADS)
    wv = w_in[:, o1:o2].T
    wqi = w_in[:, o2:o3]
    wki2 = _dup_heads(w_in[:, o3:o4], 1)
    wwi = _pad_cols(w_in[:, o4:], LANES).T
    cos, sin = _rope_tables(seq)
    q, k, vt, qi, ki, wit = _dsa_proj(x2, g, wq, wk2, wv, wqi, wki2, wwi, cos, sin, batch, seq)
    o = _dsa_attn(q, qi, wit, ki, k, vt, batch, seq)
    return _resproj(o, w_o.astype(bf16), x2)


def _rwkv_layer(x2, g, mu, w_rkv, w0, w1, w2, a0, a1, a2, g1, g2, k_k, k_a, r_k, lnx_w, lnx_b, w_o,
                batch, seq):
    vec = lambda p: p.reshape(1, D_MODEL)
    w_rkv = w_rkv.astype(bf16)
    ri = jnp.arange(RW_TM)
    tri = ((ri[:, None] >= ri[None, :]) & (ri[:, None] // CHUNK == ri[None, :] // CHUNK)).astype(bf16)
    di = jnp.arange(D_MODEL)
    hred = (di[:, None] // RWKV_HEAD == jnp.arange(LANES)[None, :]).astype(bf16)
    hexp = hred.T
    at, bt, kt, rt, v, gc, bonus, gate = _rwkv_pre(
        x2, g, _pad_rows(mu, 8), w_rkv[0], w_rkv[1], w_rkv[2],
        _pad_cols(w1, LORA_PAD).astype(bf16), _pad_rows(w2, LORA_PAD).astype(bf16),
        _pad_cols(a1, LORA_PAD).astype(bf16), _pad_rows(a2, LORA_PAD).astype(bf16),
        _pad_cols(g1, GATE_PAD).astype(bf16), _pad_rows(g2, GATE_PAD).astype(bf16),
        vec(w0), vec(a0), vec(k_k), vec(k_a), vec(r_k), tri, hred, hexp, seq)
    q, yl, mm, gg = _rwkv_chunk(at, bt, kt, rt, v, gc)
    y = _rwkv_scan(q, yl, mm, gg, batch, seq)
    return _rwkv_post(y, bonus, gate, x2, vec(lnx_w), vec(lnx_b), hred, hexp, w_o.astype(bf16))


def kernel(x, mixer_norm, mlp_norm, mlp_w_up, mlp_w_down, final_norm, dsa_w_in, dsa_w_o, rwkv_mu, rwkv_w_rkv, rwkv_w0, rwkv_w1, rwkv_w2, rwkv_a0, rwkv_a1, rwkv_a2, rwkv_g1, rwkv_g2, rwkv_k_k, rwkv_k_a, rwkv_r_k, rwkv_lnx_w, rwkv_lnx_b, rwkv_w_o):
    batch, seq, _ = x.shape
    depth = mixer_norm.shape[0]
    x2 = x.reshape(batch * seq, D_MODEL)
    fg = final_norm.reshape(1, D_MODEL)
    for i in range(depth):
        g = mixer_norm[i].reshape(1, D_MODEL)
        j = i // 2
        if i % 2 == 0:
            x2 = _dsa_layer(x2, g, dsa_w_in[j], dsa_w_o[j], batch, seq)
        else:
            x2 = _rwkv_layer(x2, g, rwkv_mu[j], rwkv_w_rkv[j], rwkv_w0[j], rwkv_w1[j], rwkv_w2[j],
                             rwkv_a0[j], rwkv_a1[j], rwkv_a2[j], rwkv_g1[j], rwkv_g2[j],
                             rwkv_k_k[j], rwkv_k_a[j], rwkv_r_k[j], rwkv_lnx_w[j], rwkv_lnx_b[j],
                             rwkv_w_o[j], batch, seq)
        x2 = _mlp(x2, mlp_norm[i].reshape(1, D_MODEL), mlp_w_up[i].astype(bf16),
                  mlp_w_down[i].astype(bf16), fg, final=(i == depth - 1))
    return x2.reshape(batch, seq, D_MODEL)
```

```python
import functools

import jax
import jax.numpy as jnp
from jax import lax
from jax.experimental import pallas as pl
from jax.experimental.pallas import tpu as pltpu

f32 = jnp.float32
bf16 = jnp.bfloat16

D_MODEL = 1024
D_FF = 4 * D_MODEL
NORM_EPS = 1e-6

ATT_HEADS = 16
ATT_KV_HEADS = 4
HEAD_DIM = 64
IDX_HEADS = 8
TOPK_MAX = 256
ROPE_THETA = 10000.0

RWKV_HEAD = 64
RWKV_HEADS = 16
LNX_EPS = 64e-5
CHUNK = 64

LANES = 128
VMEM_LIMIT = 56 * 1024 * 1024

NT = (((1,), (1,)), ((), ()))
TN = (((0,), (0,)), ((), ()))


def _cparams(*sem):
    return pltpu.CompilerParams(dimension_semantics=sem, vmem_limit_bytes=VMEM_LIMIT)


def _rms(x, g):
    return x * lax.rsqrt(jnp.mean(x * x, axis=-1, keepdims=True) + NORM_EPS) * g


def _bdot(a, b, dims=None):
    a = a.astype(bf16)
    b = b.astype(bf16)
    if dims is None:
        return jnp.dot(a, b, preferred_element_type=f32)
    return lax.dot_general(a, b, dims, preferred_element_type=f32)


def _split_dot(a, b, parts):
    out = None
    rem = a
    for p in range(parts):
        piece = rem.astype(bf16)
        term = jnp.dot(piece, b, preferred_element_type=f32)
        out = term if out is None else out + term
        if p + 1 < parts:
            rem = rem - piece.astype(f32)
    return out


def _head_sum(x, hred, hexp):
    red = jnp.dot(x.astype(bf16), hred, preferred_element_type=f32)
    return _split_dot(red, hexp, 2)


def _const_spec(shape):
    return pl.BlockSpec(shape, lambda *_: (0,) * len(shape))


MLP_TM = 512
MLP_FCH = 512


def _mlp_kernel(x_ref, g_ref, wu_ref, wd_ref, fg_ref, o_ref, *, final):
    x = x_ref[...]
    xn = _rms(x, g_ref[...]).astype(bf16)
    o_ref[...] = x
    for f in range(0, D_FF, MLP_FCH):
        u = jnp.dot(xn, wu_ref[:, f:f + MLP_FCH], preferred_element_type=f32)
        u = jnp.maximum(u, 0.0)
        o_ref[...] += jnp.dot((u * u).astype(bf16), wd_ref[f:f + MLP_FCH, :],
                              preferred_element_type=f32)
    if final:
        o_ref[...] = _rms(o_ref[...], fg_ref[...])


def _mlp(x2, g, wu, wd, fg, final):
    m = x2.shape[0]
    return pl.pallas_call(
        functools.partial(_mlp_kernel, final=final),
        grid=(m // MLP_TM,),
        in_specs=[pl.BlockSpec((MLP_TM, D_MODEL), lambda i: (i, 0)),
                  _const_spec((1, D_MODEL)),
                  _const_spec((D_MODEL, D_FF)),
                  _const_spec((D_FF, D_MODEL)),
                  _const_spec((1, D_MODEL))],
        out_specs=pl.BlockSpec((MLP_TM, D_MODEL), lambda i: (i, 0)),
        out_shape=jax.ShapeDtypeStruct((m, D_MODEL), f32),
        compiler_params=_cparams("parallel"),
        name="mlp",
    )(x2, g, wu, wd, fg)


PROJ_TM = 512


def _resproj_kernel(a_ref, w_ref, r_ref, o_ref):
    o_ref[...] = r_ref[...] + jnp.dot(a_ref[...], w_ref[...], preferred_element_type=f32)


def _resproj(a, w, res):
    m, k = a.shape
    n = w.shape[1]
    return pl.pallas_call(
        _resproj_kernel,
        grid=(m // PROJ_TM,),
        in_specs=[pl.BlockSpec((PROJ_TM, k), lambda i: (i, 0)),
                  _const_spec((k, n)),
                  pl.BlockSpec((PROJ_TM, n), lambda i: (i, 0))],
        out_specs=pl.BlockSpec((PROJ_TM, n), lambda i: (i, 0)),
        out_shape=jax.ShapeDtypeStruct((m, n), f32),
        compiler_params=_cparams("parallel"),
        name="resproj",
    )(a, w, res)


DSA_TM = 256


def _rope(x, cos, sin_signed, first_half):
    fwd = pltpu.roll(x, 32, 1)
    bwd = pltpu.roll(x, 96, 1)
    return x * cos + jnp.where(first_half, bwd, fwd) * sin_signed


LOG2E = 1.4426950408889634


VT_ROWS = 80


def _dsa_proj_kernel(x_ref, g_ref, wq_ref, wk_ref, wv_ref, wqi_ref, wki_ref, wwi_ref,
                     cos_ref, sin_ref,
                     q_ref, k_ref, vt_ref, qi_ref, ki_ref, wit_ref):
    h = _rms(x_ref[...], g_ref[...]).astype(bf16)
    tm = h.shape[0]
    cos = cos_ref[...]
    sin = sin_ref[...]
    lane = lax.broadcasted_iota(jnp.int32, (1, LANES), 1)
    first_half = (lane % HEAD_DIM) < (HEAD_DIM // 2)
    low = lane < HEAD_DIM

    def per_head(x, out_ref, pair, scale):
        xc = _rope(x[:, pair * LANES:(pair + 1) * LANES], cos, sin, first_half) * scale
        swapped = pltpu.roll(xc, HEAD_DIM, 1)
        out_ref[:, (2 * pair) * LANES:(2 * pair + 1) * LANES] = jnp.where(low, xc, 0.0).astype(bf16)
        out_ref[:, (2 * pair + 1) * LANES:(2 * pair + 2) * LANES] = jnp.where(low, swapped, 0.0).astype(bf16)

    q = jnp.dot(h, wq_ref[...], preferred_element_type=f32)
    for pair in range(ATT_HEADS // 2):
        per_head(q, q_ref, pair, HEAD_DIM ** -0.5 * LOG2E)
    qi = jnp.dot(h, wqi_ref[...], preferred_element_type=f32)
    for pair in range(IDX_HEADS // 2):
        per_head(qi, qi_ref, pair, 1.0)

    k2 = jnp.dot(h, wk_ref[...], preferred_element_type=f32)
    for c in range(0, ATT_KV_HEADS * LANES, LANES):
        kc = _rope(k2[:, c:c + LANES], cos, sin, first_half)
        k_ref[0, :, c:c + LANES] = jnp.where(low, kc, 0.0).astype(bf16)
    ki2 = _rope(jnp.dot(h, wki_ref[...], preferred_element_type=f32), cos, sin, first_half)
    ki_ref[0] = jnp.where(low, ki2, 0.0).astype(bf16)

    vt = lax.dot_general(wv_ref[...], h, NT, preferred_element_type=f32)
    extra = (lax.broadcasted_iota(jnp.int32, (VT_ROWS - HEAD_DIM, tm), 0) == 0).astype(f32)
    for g in range(ATT_KV_HEADS):
        vt_ref[0, g * VT_ROWS:(g + 1) * VT_ROWS, :] = jnp.concatenate(
            [vt[g * HEAD_DIM:(g + 1) * HEAD_DIM], extra], axis=0).astype(bf16)
    wit_ref[...] = (lax.dot_general(wwi_ref[...], h, NT, preferred_element_type=f32)
                    * (IDX_HEADS ** -0.5 * HEAD_DIM ** -0.5))


def _dsa_proj(x2, g, wq, wk2, wv, wqi, wki2, wwi, cos, sin, batch, seq):
    m = x2.shape[0]
    nblk = seq // DSA_TM
    row = lambda n: pl.BlockSpec((DSA_TM, n), lambda b, i: (b * nblk + i, 0))
    seqrow = lambda n: pl.BlockSpec((1, DSA_TM, n), lambda b, i: (b, i, 0))
    tab = pl.BlockSpec((DSA_TM, LANES), lambda b, i: (i, 0))
    kvw = ATT_KV_HEADS * LANES
    vtr = ATT_KV_HEADS * VT_ROWS
    return pl.pallas_call(
        _dsa_proj_kernel,
        grid=(batch, nblk),
        in_specs=[row(D_MODEL), _const_spec((1, D_MODEL)),
                  _const_spec(wq.shape), _const_spec(wk2.shape), _const_spec(wv.shape),
                  _const_spec(wqi.shape), _const_spec(wki2.shape), _const_spec(wwi.shape),
                  tab, tab],
        out_specs=[row(ATT_HEADS * LANES), seqrow(kvw),
                   pl.BlockSpec((1, vtr, DSA_TM), lambda b, i: (b, 0, i)),
                   row(IDX_HEADS * LANES), seqrow(LANES),
                   pl.BlockSpec((LANES, DSA_TM), lambda b, i: (0, b * nblk + i))],
        out_shape=[jax.ShapeDtypeStruct((m, ATT_HEADS * LANES), bf16),
                   jax.ShapeDtypeStruct((batch, seq, kvw), bf16),
                   jax.ShapeDtypeStruct((batch, vtr, seq), bf16),
                   jax.ShapeDtypeStruct((m, IDX_HEADS * LANES), bf16),
                   jax.ShapeDtypeStruct((batch, seq, LANES), bf16),
                   jax.ShapeDtypeStruct((LANES, m), f32)],
        compiler_params=_cparams("parallel", "parallel"),
        name="dsa_proj",
    )(x2, g, wq, wk2, wv, wqi, wki2, wwi, cos, sin)


DSA_TQ = 128
DSA_CLS = 512
IDX_GROUP = 4
DSA_ROWS = 256
INT_MIN = -2 ** 31
NEG_INF_KEY = -2139095041


FOLD_ROWS = 64


def _fold_rows(x, op):
    rows = x.shape[0]
    if rows > FOLD_ROWS and rows % FOLD_ROWS == 0:
        acc = x[:FOLD_ROWS]
        for r in range(FOLD_ROWS, rows, FOLD_ROWS):
            acc = op(acc, x[r:r + FOLD_ROWS])
        x = acc
    while x.shape[0] % 16 == 0:
        h = x.shape[0] // 2
        x = op(x[:h], x[h:])
    return x


def _col_count(mask):
    return jnp.sum(_fold_rows(jnp.where(mask, 1.0, 0.0), jnp.add), axis=0, keepdims=True).astype(jnp.int32)


def _stack_heads(ref, first, count):
    return jnp.concatenate([ref[:, (first + j) * LANES:(first + j + 1) * LANES] for j in range(count)],
                           axis=0)


def _dsa_attn_body(q_ref, qi_ref, wit_ref, ki_ref, k_ref, vt_ref, o_ref,
                   okey_ref, bias_ref, logit_ref, *, width, top_k, t0):
    col_t = t0 + lax.broadcasted_iota(jnp.int32, (1, DSA_TQ), 1)
    row_s = lax.broadcasted_iota(jnp.int32, (width, 1), 0)
    causal = row_s <= col_t

    wit = wit_ref[...]
    ki = ki_ref[0, 0:width, :]
    score = jnp.zeros((width, DSA_TQ), f32)
    for first in range(0, IDX_HEADS, IDX_GROUP):
        rel = lax.dot_general(ki, _stack_heads(qi_ref, first, IDX_GROUP), NT,
                              preferred_element_type=f32)
        for j in range(IDX_GROUP):
            hd = first + j
            score = score + wit[hd:hd + 1, :] * jnp.maximum(rel[:, j * DSA_TQ:(j + 1) * DSA_TQ], 0.0)
    score = jnp.where(causal, score, -jnp.inf)

    def key_to_float(k):
        k = jnp.maximum(k, NEG_INF_KEY)
        return lax.bitcast_convert_type(k ^ ((k >> 31) & jnp.int32(0x7FFFFFFF)), f32)

    n_nonneg = _col_count(score >= 0.0)
    start_high = n_nonneg >= top_k
    thr = jnp.where(start_high, jnp.int32(0), jnp.int32(INT_MIN))
    n_ge = jnp.where(start_high, n_nonneg, jnp.int32(width))

    def value_step(i, state):
        thr, n_ge = state
        cand = thr | (jnp.int32(1) << (30 - i))
        cnt = _col_count(score >= key_to_float(cand))
        take = cnt >= top_k
        return jnp.where(take, cand, thr), jnp.where(take, cnt, n_ge)

    thr, n_ge = lax.fori_loop(0, 31, value_step, (thr, n_ge))

    thr_f = key_to_float(thr)
    bias_ref[0:width, :] = jnp.where((score >= thr_f) & causal, 0.0, -jnp.inf)
    overfull = (n_ge != top_k) & (thr > NEG_INF_KEY)

    @pl.when(jnp.max(overfull.astype(f32)) > 0.0)
    def _():
        real_thr = thr > NEG_INF_KEY
        next_f = key_to_float(thr + 1)
        beyond = score >= next_f
        in_bin = (score >= thr_f) & jnp.logical_not(beyond) & real_thr
        offset = jnp.where(in_bin, score - jnp.where(real_thr, thr_f, 0.0), -1.0)
        need = top_k - _col_count(beyond)

        okey_ref[...] = jnp.zeros(okey_ref.shape, jnp.int32)

        @pl.when(jnp.max(offset) > 0.0)
        def _():
            def offset_step(i, okey):
                cand = okey | (jnp.int32(1) << (30 - i))
                cnt = _col_count(offset >= lax.bitcast_convert_type(cand, f32))
                return jnp.where(cnt >= need, cand, okey)

            okey = lax.fori_loop(0, 31, offset_step, jnp.zeros((1, DSA_TQ), jnp.int32))
            okey_ref[...] = jnp.broadcast_to(okey, okey_ref.shape)

        othr = lax.bitcast_convert_type(okey_ref[0:1, :], f32)
        above = offset > othr
        tied = offset == othr
        need = need - _col_count(above)
        nbits = (width - 1).bit_length()

        def index_step(i, pos):
            cand = pos | (jnp.int32(1) << (nbits - 1 - i))
            return jnp.where(_col_count(tied & (row_s < cand)) < need, cand, pos)

        pos = lax.fori_loop(0, nbits, index_step, jnp.zeros((1, DSA_TQ), jnp.int32))
        keep = (beyond | above | (tied & (row_s <= pos))) & causal
        bias_ref[0:width, :] = jnp.where(keep, 0.0, -jnp.inf)

    group = ATT_HEADS // ATT_KV_HEADS
    for g in range(ATT_KV_HEADS):
        rq = _stack_heads(q_ref, g * group, group)
        maxes = [None] * group
        for rc in range(0, width, DSA_ROWS):
            logits4 = lax.dot_general(k_ref[0, rc:rc + DSA_ROWS, g * LANES:(g + 1) * LANES], rq, NT,
                                      preferred_element_type=f32)
            b = bias_ref[rc:rc + DSA_ROWS, :]
            for j in range(group):
                logits = logits4[:, j * DSA_TQ:(j + 1) * DSA_TQ] + b
                logit_ref[j, rc:rc + DSA_ROWS, :] = logits
                m = _fold_rows(logits, jnp.maximum)
                maxes[j] = m if maxes[j] is None else jnp.maximum(maxes[j], m)
        outs = []
        for j in range(group):
            mx = jnp.max(maxes[j], axis=0, keepdims=True)
            pv = None
            for rc in range(0, width, DSA_ROWS):
                p = jnp.exp2(logit_ref[j, rc:rc + DSA_ROWS, :] - mx).astype(bf16)
                part = jnp.dot(vt_ref[0, g * VT_ROWS:(g + 1) * VT_ROWS, rc:rc + DSA_ROWS], p,
                               preferred_element_type=f32)
                pv = part if pv is None else pv + part
            outs.append(pv[:HEAD_DIM] * (1.0 / pv[HEAD_DIM:HEAD_DIM + 1]))
        for pair in range(group // 2):
            both = jnp.concatenate(outs[2 * pair:2 * pair + 2], axis=0).T
            col = (g * group // 2 + pair) * LANES
            o_ref[:, col:col + LANES] = both.astype(bf16)


def _dsa_attn_kernel(q_ref, qi_ref, wit_ref, ki_ref, k_ref, vt_ref, o_ref,
                     okey_ref, bias_ref, logit_ref, *, seq, top_k):
    i = pl.program_id(1)
    blocks_per_class = DSA_CLS // DSA_TQ
    for cls in range(seq // DSA_CLS):
        @pl.when(i // blocks_per_class == cls)
        def _(cls=cls):
            _dsa_attn_body(q_ref, qi_ref, wit_ref, ki_ref, k_ref, vt_ref, o_ref,
                           okey_ref, bias_ref, logit_ref, width=DSA_CLS * (cls + 1), top_k=top_k,
                           t0=i * DSA_TQ)


def _dsa_attn(q, qi, wit, ki, k, vt, batch, seq):
    nq = seq // DSA_TQ
    top_k = min(TOPK_MAX, seq // 4)
    qrow = lambda n: pl.BlockSpec((DSA_TQ, n), lambda b, i: (b * nq + i, 0))
    full = lambda n: pl.BlockSpec((1, seq, n), lambda b, i: (b, 0, 0))
    return pl.pallas_call(
        functools.partial(_dsa_attn_kernel, seq=seq, top_k=top_k),
        grid=(batch, nq),
        in_specs=[qrow(ATT_HEADS * LANES), qrow(IDX_HEADS * LANES),
                  pl.BlockSpec((8, DSA_TQ), lambda b, i: (0, b * nq + i)),
                  full(LANES), full(ATT_KV_HEADS * LANES),
                  pl.BlockSpec((1, ATT_KV_HEADS * VT_ROWS, seq), lambda b, i: (b, 0, 0))],
        out_specs=qrow(ATT_HEADS * HEAD_DIM),
        out_shape=jax.ShapeDtypeStruct((batch * seq, ATT_HEADS * HEAD_DIM), bf16),
        scratch_shapes=[pltpu.VMEM((8, DSA_TQ), jnp.int32),
                        pltpu.VMEM((seq, DSA_TQ), f32),
                        pltpu.VMEM((ATT_HEADS // ATT_KV_HEADS, seq, DSA_TQ), f32)],
        compiler_params=_cparams("parallel", "arbitrary"),
        name="dsa_attn",
    )(q, qi, wit, ki, k, vt)


RW_TM = 256
LORA_PAD = 128
GATE_PAD = 256


def _rwkv_pre_kernel(x_ref, xp_ref, g_ref, mu_ref, wr_ref, wk_ref, wv_ref,
                     w1_ref, w2_ref, a1_ref, a2_ref, g1_ref, g2_ref,
                     w0_ref, a0_ref, kk_ref, ka_ref, rk_ref, tri_ref, hred_ref, hexp_ref,
                     at_ref, bt_ref, kt_ref, rt_ref, v_ref, gc_ref, bonus_ref, gate_ref,
                     lc_scr, *, seq):
    i = pl.program_id(0)
    g = g_ref[...]
    h = _rms(x_ref[...], g)
    hp = _rms(xp_ref[...], g)[7:8, :]
    hp = jnp.where((i * RW_TM) % seq == 0, 0.0, hp)
    rowi = lax.broadcasted_iota(jnp.int32, (RW_TM, 1), 0)
    hs = jnp.where(rowi == 0, hp, pltpu.roll(h, 1, 0))
    xx = hs - h
    mu = mu_ref[...]

    def mix(c):
        return (h + xx * mu[c:c + 1, :]).astype(bf16)

    r = jnp.dot(mix(0), wr_ref[...], preferred_element_type=f32)
    k = jnp.dot(mix(1), wk_ref[...], preferred_element_type=f32)
    v = jnp.dot(mix(2), wv_ref[...], preferred_element_type=f32)

    wl = w0_ref[...] + _bdot(jnp.tanh(jnp.dot(mix(3), w1_ref[...], preferred_element_type=f32)),
                             w2_ref[...])
    nwl = -wl
    softplus = jnp.maximum(nwl, 0.0) + jnp.log1p(jnp.exp(-jnp.abs(nwl)))
    ld = -jnp.exp(-softplus - 0.5)
    a = jax.nn.sigmoid(a0_ref[...] + _bdot(jnp.dot(mix(4), a1_ref[...],
                                                   preferred_element_type=f32), a2_ref[...]))
    gate_ref[...] = _bdot(jax.nn.sigmoid(jnp.dot(mix(5), g1_ref[...],
                                                 preferred_element_type=f32)), g2_ref[...])

    hred = hred_ref[...]
    hexp = hexp_ref[...]
    z = k * kk_ref[...]
    kk = z * lax.rsqrt(jnp.maximum(_head_sum(z * z, hred, hexp), 1e-24))
    k2 = k * (1.0 + (a - 1.0) * ka_ref[...])
    bonus_ref[...] = _head_sum(r * k2 * rk_ref[...], hred, hexp) * v

    lc = _split_dot_left(tri_ref[...], ld, 2)
    lc_scr[...] = lc
    for c in range(RW_TM // CHUNK):
        last = lc_scr[c * CHUNK + CHUNK - 1:c * CHUNK + CHUNK, :]
        gc_ref[c * 8:(c + 1) * 8, :] = jnp.broadcast_to(jnp.exp(last), (8, D_MODEL))
    einv = jnp.exp(-lc)
    at_ref[...] = -kk * jnp.exp(lc - ld)
    bt_ref[...] = kk * a * einv
    kt_ref[...] = k2 * einv
    rt_ref[...] = r * jnp.exp(lc)
    v_ref[...] = v


def _split_dot_left(a, b, parts):
    out = None
    rem = b
    for p in range(parts):
        piece = rem.astype(bf16)
        term = jnp.dot(a, piece, preferred_element_type=f32)
        out = term if out is None else out + term
        if p + 1 < parts:
            rem = rem - piece.astype(f32)
    return out


def _rwkv_pre(x2, g, mu8, wr, wk, wv, w1, w2, a1, a2, g1, g2, w0, a0, k_k, k_a, r_k, tri, hred, hexp,
              seq):
    m = x2.shape[0]
    row = pl.BlockSpec((RW_TM, D_MODEL), lambda i: (i, 0))
    prev = pl.BlockSpec((8, D_MODEL), lambda i: (jnp.maximum(i * (RW_TM // 8) - 1, 0), 0))
    vec = _const_spec((1, D_MODEL))
    gcrow = pl.BlockSpec((RW_TM // CHUNK * 8, D_MODEL), lambda i: (i, 0))
    act = jax.ShapeDtypeStruct((m, D_MODEL), f32)
    return pl.pallas_call(
        functools.partial(_rwkv_pre_kernel, seq=seq),
        grid=(m // RW_TM,),
        in_specs=[row, prev, vec, _const_spec((8, D_MODEL)),
                  _const_spec(wr.shape), _const_spec(wk.shape), _const_spec(wv.shape),
                  _const_spec(w1.shape), _const_spec(w2.shape),
                  _const_spec(a1.shape), _const_spec(a2.shape),
                  _const_spec(g1.shape), _const_spec(g2.shape),
                  vec, vec, vec, vec, vec,
                  _const_spec(tri.shape), _const_spec(hred.shape), _const_spec(hexp.shape)],
        out_specs=[row, row, row, row, row, gcrow, row, row],
        out_shape=[act, act, act, act, act,
                   jax.ShapeDtypeStruct((m // CHUNK * 8, D_MODEL), f32), act, act],
        scratch_shapes=[pltpu.VMEM((RW_TM, D_MODEL), f32)],
        compiler_params=_cparams("parallel"),
        name="rwkv_pre",
    )(x2, x2, g, mu8, wr, wk, wv, w1, w2, a1, a2, g1, g2, w0, a0, k_k, k_a, r_k, tri, hred, hexp)


RA_GROUP = 128
RA_ROWS = 1024


def _rwkv_chunk_kernel(at_ref, bt_ref, kt_ref, rt_ref, v_ref, gc_ref,
                       q_ref, yl_ref, m_ref, g_ref):
    n = RA_GROUP
    ri = lax.broadcasted_iota(jnp.int32, (n, n), 0)
    ci = lax.broadcasted_iota(jnp.int32, (n, n), 1)
    same = (ri // CHUNK) == (ci // CHUNK)
    strict = same & (ci < ri)
    incl = same & (ci <= ri)
    eye = (ri == ci).astype(f32)
    lane = lax.broadcasted_iota(jnp.int32, (1, LANES), 1)
    low = lane < RWKV_HEAD
    rk = lax.broadcasted_iota(jnp.int32, (RWKV_HEAD, LANES), 0)
    diag = rk == (lax.broadcasted_iota(jnp.int32, (RWKV_HEAD, LANES), 1) % RWKV_HEAD)

    groups = []
    for gi in range(RA_ROWS // n):
        rows = slice(gi * n, (gi + 1) * n)
        at, bt, kt, rt, v = (r[rows, :] for r in (at_ref, bt_ref, kt_ref, rt_ref, v_ref))
        groups.append(dict(at=at, bt=bt, kt=kt, rt=rt, v=v, vb=v.astype(bf16),
                           ar=jnp.concatenate([at, rt], axis=0),
                           bk=jnp.concatenate([bt, kt], axis=0).astype(bf16)))
    probs = [(gr, hl) for gr in groups for hl in (low, jnp.logical_not(low))]

    gs = [lax.dot_general(jnp.where(hl, gr["ar"], 0.0).astype(bf16), gr["bk"], NT,
                          preferred_element_type=f32) for gr, hl in probs]
    aab = [jnp.where(strict, g[:n, :n], 0.0) for g in gs]
    aak = [jnp.where(strict, g[:n, n:], 0.0) for g in gs]
    arb = [jnp.where(incl, g[n:, :n], 0.0) for g in gs]
    ark = [jnp.where(incl, g[n:, n:], 0.0) for g in gs]

    ts = [eye + a for a in aab]
    ps = aab
    for _ in range((CHUNK - 1).bit_length() - 1):
        ps = [_bdot(p, p) for p in ps]
        ts = [t + _bdot(t, p) for t, p in zip(ts, ps)]

    akv = [_bdot(a, gr["vb"]) for a, (gr, _) in zip(aak, probs)]
    tw = [_bdot(t, jnp.concatenate([gr["at"], x], axis=1))
          for t, x, (gr, _) in zip(ts, akv, probs)]
    qy = [_bdot(a, w) for a, w in zip(arb, tw)]
    rkv = [_bdot(a, gr["vb"]) for a, (gr, _) in zip(ark, probs)]

    for gi, gr in enumerate(groups):
        lo, hi = 2 * gi, 2 * gi + 1
        w_hat = jnp.where(low, tw[lo][:, :LANES], tw[hi][:, :LANES])
        u_loc = jnp.where(low, tw[lo][:, LANES:], tw[hi][:, LANES:])
        base = gi * n
        q_ref[base:base + n, :] = gr["rt"] + jnp.where(low, qy[lo][:, :LANES], qy[hi][:, :LANES])
        yl_ref[base:base + n, :] = (jnp.where(low, qy[lo][:, LANES:], qy[hi][:, LANES:])
                                    + jnp.where(low, rkv[lo], rkv[hi]))
        for c in range(n // CHUNK):
            rows = slice(c * CHUNK, (c + 1) * CHUNK)
            ch = base // CHUNK + c
            gc = gc_ref[ch * 8:ch * 8 + 1, :]
            bh = gr["bt"][rows] * gc
            kh = gr["kt"][rows] * gc
            pm = _bdot(w_hat[rows], bh, TN)
            pg = _bdot(jnp.concatenate([u_loc[rows], gr["v"][rows]], axis=0),
                       jnp.concatenate([bh, kh], axis=0), TN)
            out = slice(base + c * CHUNK, base + (c + 1) * CHUNK)
            m_ref[out, :] = (jnp.where(low, pm[:RWKV_HEAD], pm[RWKV_HEAD:])
                             + jnp.where(diag, gc, 0.0))
            g_ref[out, :] = jnp.where(low, pg[:RWKV_HEAD], pg[RWKV_HEAD:])


def _rwkv_chunk(at, bt, kt, rt, v, gc):
    m = at.shape[0]
    blk = pl.BlockSpec((RA_ROWS, LANES), lambda i, p: (i, p))
    gblk = pl.BlockSpec((RA_ROWS // CHUNK * 8, LANES), lambda i, p: (i, p))
    act = jax.ShapeDtypeStruct((m, D_MODEL), f32)
    return pl.pallas_call(
        _rwkv_chunk_kernel,
        grid=(m // RA_ROWS, D_MODEL // LANES),
        in_specs=[blk, blk, blk, blk, blk, gblk],
        out_specs=[blk, blk, blk, blk],
        out_shape=[act, act, act, act],
        compiler_params=_cparams("parallel", "parallel"),
        name="rwkv_chunk",
    )(at, bt, kt, rt, v, gc)


SCAN_LANES = 512


def _rwkv_scan_kernel(q_ref, yl_ref, m_ref, g_ref, y_ref, *, seq):
    pairs = SCAN_LANES // LANES
    low = lax.broadcasted_iota(jnp.int32, (1, LANES), 1) < RWKV_HEAD

    def blockdiag(x):
        return jnp.concatenate([jnp.where(low, x, 0.0), jnp.where(low, 0.0, x)], axis=0)

    def body(c, states):
        rows = pl.ds(pl.multiple_of(c * CHUNK, CHUNK), CHUNK)
        new_states = []
        for p in range(pairs):
            cols = slice(p * LANES, (p + 1) * LANES)
            s = states[p]
            y_ref[rows, cols] = yl_ref[rows, cols] + _bdot(q_ref[rows, cols], s, NT)
            mc = blockdiag(m_ref[rows, cols]).astype(bf16)
            s_hi = s.astype(bf16)
            s_lo = (s - s_hi.astype(f32)).astype(bf16)
            sm = (jnp.dot(s_hi, mc, preferred_element_type=f32)
                  + jnp.dot(s_lo, mc, preferred_element_type=f32))
            new_states.append(sm + blockdiag(g_ref[rows, cols]))
        return tuple(new_states)

    init = tuple(jnp.zeros((LANES, LANES), f32) for _ in range(pairs))
    lax.fori_loop(0, seq // CHUNK, body, init)


def _rwkv_scan(q, yl, mm, gg, batch, seq):
    blk = pl.BlockSpec((seq, SCAN_LANES), lambda b, p: (b, p))
    return pl.pallas_call(
        functools.partial(_rwkv_scan_kernel, seq=seq),
        grid=(batch, D_MODEL // SCAN_LANES),
        in_specs=[blk, blk, blk, blk],
        out_specs=blk,
        out_shape=jax.ShapeDtypeStruct((batch * seq, D_MODEL), f32),
        compiler_params=_cparams("parallel", "parallel"),
        name="rwkv_scan",
    )(q, yl, mm, gg)


def _rwkv_post_kernel(y_ref, bonus_ref, gate_ref, x_ref, lw_ref, lb_ref, hred_ref, hexp_ref,
                      wo_ref, o_ref):
    y = y_ref[...]
    hred = hred_ref[...]
    hexp = hexp_ref[...]
    mean = _head_sum(y, hred, hexp) * (1.0 / RWKV_HEAD)
    yc = y - mean
    var = _head_sum(yc * yc, hred, hexp) * (1.0 / RWKV_HEAD)
    yn = yc * lax.rsqrt(var + LNX_EPS) * lw_ref[...] + lb_ref[...]
    out = (yn + bonus_ref[...]) * gate_ref[...]
    o_ref[...] = x_ref[...] + jnp.dot(out.astype(bf16), wo_ref[...], preferred_element_type=f32)


def _rwkv_post(y, bonus, gate, x2, lw, lb, hred, hexp, wo):
    m = y.shape[0]
    row = pl.BlockSpec((RW_TM, D_MODEL), lambda i: (i, 0))
    vec = _const_spec((1, D_MODEL))
    return pl.pallas_call(
        _rwkv_post_kernel,
        grid=(m // RW_TM,),
        in_specs=[row, row, row, row, vec, vec, _const_spec(hred.shape), _const_spec(hexp.shape),
                  _const_spec(wo.shape)],
        out_specs=row,
        out_shape=jax.ShapeDtypeStruct((m, D_MODEL), f32),
        compiler_params=_cparams("parallel"),
        name="rwkv_post",
    )(y, bonus, gate, x2, lw, lb, hred, hexp, wo)


def _dup_heads(w, n_heads):
    w = w.reshape(w.shape[0], n_heads, 1, HEAD_DIM)
    return jnp.broadcast_to(w, (w.shape[0], n_heads, 2, HEAD_DIM)).reshape(w.shape[0], n_heads * LANES)


def _pad_cols(w, n):
    return jnp.pad(w, ((0, 0), (0, n - w.shape[1])))


def _pad_rows(w, n):
    return jnp.pad(w, ((0, n - w.shape[0]), (0, 0)))


def _rope_tables(seq):
    inv = 1.0 / (ROPE_THETA ** (jnp.arange(0, HEAD_DIM, 2, dtype=f32) / HEAD_DIM))
    ang = jnp.arange(seq, dtype=f32)[:, None] * inv[None, :]
    cos, sin = jnp.cos(ang), jnp.sin(ang)
    return jnp.tile(jnp.concatenate([cos, cos], 1), (1, 2)), jnp.tile(jnp.concatenate([-sin, sin], 1), (1, 2))


def _dsa_layer(x2, g, w_in, w_o, batch, seq):
    o0 = ATT_HEADS * HEAD_DIM
    o1 = o0 + ATT_KV_HEADS * HEAD_DIM
    o2 = o1 + ATT_KV_HEADS * HEAD_DIM
    o3 = o2 + IDX_HEADS * HEAD_DIM
    o4 = o3 + HEAD_DIM
    w_in = w_in.astype(bf16)
    wq = w_in[:, :o0]
    wk2 = _dup_heads(w_in[:, o0:o1], ATT_KV_HEADS)
    wv = w_in[:, o1:o2].T
    wqi = w_in[:, o2:o3]
    wki2 = _dup_heads(w_in[:, o3:o4], 1)
    wwi = _pad_cols(w_in[:, o4:], LANES).T
    cos, sin = _rope_tables(seq)
    q, k, vt, qi, ki, wit = _dsa_proj(x2, g, wq, wk2, wv, wqi, wki2, wwi, cos, sin, batch, seq)
    o = _dsa_attn(q, qi, wit, ki, k, vt, batch, seq)
    return _resproj(o, w_o.astype(bf16), x2)


def _rwkv_layer(x2, g, mu, w_rkv, w0, w1, w2, a0, a1, a2, g1, g2, k_k, k_a, r_k, lnx_w, lnx_b, w_o,
                batch, seq):
    vec = lambda p: p.reshape(1, D_MODEL)
    w_rkv = w_rkv.astype(bf16)
    ri = jnp.arange(RW_TM)
    tri = ((ri[:, None] >= ri[None, :]) & (ri[:, None] // CHUNK == ri[None, :] // CHUNK)).astype(bf16)
    di = jnp.arange(D_MODEL)
    hred = (di[:, None] // RWKV_HEAD == jnp.arange(LANES)[None, :]).astype(bf16)
    hexp = hred.T
    at, bt, kt, rt, v, gc, bonus, gate = _rwkv_pre(
        x2, g, _pad_rows(mu, 8), w_rkv[0], w_rkv[1], w_rkv[2],
        _pad_cols(w1, LORA_PAD).astype(bf16), _pad_rows(w2, LORA_PAD).astype(bf16),
        _pad_cols(a1, LORA_PAD).astype(bf16), _pad_rows(a2, LORA_PAD).astype(bf16),
        _pad_cols(g1, GATE_PAD).astype(bf16), _pad_rows(g2, GATE_PAD).astype(bf16),
        vec(w0), vec(a0), vec(k_k), vec(k_a), vec(r_k), tri, hred, hexp, seq)
    q, yl, mm, gg = _rwkv_chunk(at, bt, kt, rt, v, gc)
    y = _rwkv_scan(q, yl, mm, gg, batch, seq)
    return _rwkv_post(y, bonus, gate, x2, vec(lnx_w), vec(lnx_b), hred, hexp, w_o.astype(bf16))


def kernel(x, mixer_norm, mlp_norm, mlp_w_up, mlp_w_down, final_norm, dsa_w_in, dsa_w_o, rwkv_mu, rwkv_w_rkv, rwkv_w0, rwkv_w1, rwkv_w2, rwkv_a0, rwkv_a1, rwkv_a2, rwkv_g1, rwkv_g2, rwkv_k_k, rwkv_k_a, rwkv_r_k, rwkv_lnx_w, rwkv_lnx_b, rwkv_w_o):
    batch, seq, _ = x.shape
    depth = mixer_norm.shape[0]
    x2 = x.reshape(batch * seq, D_MODEL)
    fg = final_norm.reshape(1, D_MODEL)
    for i in range(depth):
        g = mixer_norm[i].reshape(1, D_MODEL)
        j = i // 2
        if i % 2 == 0:
            x2 = _dsa_layer(x2, g, dsa_w_in[j], dsa_w_o[j], batch, seq)
        else:
            x2 = _rwkv_layer(x2, g, rwkv_mu[j], rwkv_w_rkv[j], rwkv_w0[j], rwkv_w1[j], rwkv_w2[j],
                             rwkv_a0[j], rwkv_a1[j], rwkv_a2[j], rwkv_g1[j], rwkv_g2[j],
                             rwkv_k_k[j], rwkv_k_a[j], rwkv_r_k[j], rwkv_lnx_w[j], rwkv_lnx_b[j],
                             rwkv_w_o[j], batch, seq)
        x2 = _mlp(x2, mlp_norm[i].reshape(1, D_MODEL), mlp_w_up[i].astype(bf16),
                  mlp_w_down[i].astype(bf16), fg, final=(i == depth - 1))
    return x2.reshape(batch, seq, D_MODEL)
```

```python
import functools

import jax
import jax.numpy as jnp
from jax import lax
from jax.experimental import pallas as pl
from jax.experimental.pallas import tpu as pltpu

f32 = jnp.float32
bf16 = jnp.bfloat16

D_MODEL = 1024
D_FF = 4 * D_MODEL
NORM_EPS = 1e-6

ATT_HEADS = 16
ATT_KV_HEADS = 4
HEAD_DIM = 64
IDX_HEADS = 8
TOPK_MAX = 256
ROPE_THETA = 10000.0

RWKV_HEAD = 64
RWKV_HEADS = 16
LNX_EPS = 64e-5
CHUNK = 64

LANES = 128
VMEM_LIMIT = 56 * 1024 * 1024

NT = (((1,), (1,)), ((), ()))
TN = (((0,), (0,)), ((), ()))


def _cparams(*sem):
    return pltpu.CompilerParams(dimension_semantics=sem, vmem_limit_bytes=VMEM_LIMIT)


def _rms(x, g):
    return x * lax.rsqrt(jnp.mean(x * x, axis=-1, keepdims=True) + NORM_EPS) * g


def _bdot(a, b, dims=None):
    a = a.astype(bf16)
    b = b.astype(bf16)
    if dims is None:
        return jnp.dot(a, b, preferred_element_type=f32)
    return lax.dot_general(a, b, dims, preferred_element_type=f32)


def _split_dot(a, b, parts):
    out = None
    rem = a
    for p in range(parts):
        piece = rem.astype(bf16)
        term = jnp.dot(piece, b, preferred_element_type=f32)
        out = term if out is None else out + term
        if p + 1 < parts:
            rem = rem - piece.astype(f32)
    return out


def _head_sum(x, hred, hexp):
    red = jnp.dot(x.astype(bf16), hred, preferred_element_type=f32)
    return _split_dot(red, hexp, 2)


def _const_spec(shape):
    return pl.BlockSpec(shape, lambda *_: (0,) * len(shape))


MLP_TM = 512
MLP_FCH = 512


def _mlp_tail(x, g_ref, wu_ref, wd_ref, fg_ref, o_ref, final):
    xn = _rms(x, g_ref[...]).astype(bf16)
    o_ref[...] = x
    for f in range(0, D_FF, MLP_FCH):
        u = jnp.dot(xn, wu_ref[:, f:f + MLP_FCH], preferred_element_type=f32)
        u = jnp.maximum(u, 0.0)
        o_ref[...] += jnp.dot((u * u).astype(bf16), wd_ref[f:f + MLP_FCH, :],
                              preferred_element_type=f32)
    if final:
        o_ref[...] = _rms(o_ref[...], fg_ref[...])


def _attn_mlp_kernel(a_ref, x_ref, wo_ref, g_ref, wu_ref, wd_ref, fg_ref, o_ref, *, final):
    x = x_ref[...] + jnp.dot(a_ref[...], wo_ref[...], preferred_element_type=f32)
    _mlp_tail(x, g_ref, wu_ref, wd_ref, fg_ref, o_ref, final)


def _rwkv_mlp_kernel(y_ref, bonus_ref, gate_ref, x_ref, lw_ref, lb_ref, hred_ref, hexp_ref, wo_ref,
                     g_ref, wu_ref, wd_ref, fg_ref, o_ref, *, final):
    y = y_ref[...]
    hred = hred_ref[...]
    hexp = hexp_ref[...]
    mean = _head_sum(y, hred, hexp) * (1.0 / RWKV_HEAD)
    yc = y - mean
    var = _head_sum(yc * yc, hred, hexp) * (1.0 / RWKV_HEAD)
    yn = yc * lax.rsqrt(var + LNX_EPS) * lw_ref[...] + lb_ref[...]
    out = (yn + bonus_ref[...]) * gate_ref[...]
    x = x_ref[...] + jnp.dot(out.astype(bf16), wo_ref[...], preferred_element_type=f32)
    _mlp_tail(x, g_ref, wu_ref, wd_ref, fg_ref, o_ref, final)


def _resident(shape):
    return pl.BlockSpec(shape, lambda *_: (0,) * len(shape), pipeline_mode=pl.Buffered(1))


def _mixer_mlp(kernel, name, rows, consts, g, wu, wd, fg, final):
    m = rows[0].shape[0]
    row = pl.BlockSpec((MLP_TM, D_MODEL), lambda i: (i, 0))
    operands = list(rows) + list(consts) + [g, wu, wd, fg]
    return pl.pallas_call(
        functools.partial(kernel, final=final),
        grid=(m // MLP_TM,),
        in_specs=[row] * len(rows) + [_resident(c.shape) for c in operands[len(rows):]],
        out_specs=row,
        out_shape=jax.ShapeDtypeStruct((m, D_MODEL), f32),
        compiler_params=_cparams("parallel"),
        name=name,
    )(*operands)


DSA_TM = 256


def _rope(x, cos, sin_signed, first_half):
    fwd = pltpu.roll(x, 32, 1)
    bwd = pltpu.roll(x, 96, 1)
    return x * cos + jnp.where(first_half, bwd, fwd) * sin_signed


LOG2E = 1.4426950408889634


VT_ROWS = 80


def _dsa_proj_kernel(x_ref, g_ref, wq_ref, wk_ref, wv_ref, wqi_ref, wki_ref, wwi_ref,
                     cos_ref, sin_ref,
                     q_ref, k_ref, vt_ref, qi_ref, ki_ref, wit_ref):
    h = _rms(x_ref[...], g_ref[...]).astype(bf16)
    tm = h.shape[0]
    cos = cos_ref[...]
    sin = sin_ref[...]
    lane = lax.broadcasted_iota(jnp.int32, (1, LANES), 1)
    first_half = (lane % HEAD_DIM) < (HEAD_DIM // 2)
    low = lane < HEAD_DIM

    def per_head(x, out_ref, pair, scale):
        xc = _rope(x[:, pair * LANES:(pair + 1) * LANES], cos, sin, first_half) * scale
        swapped = pltpu.roll(xc, HEAD_DIM, 1)
        out_ref[:, (2 * pair) * LANES:(2 * pair + 1) * LANES] = jnp.where(low, xc, 0.0).astype(bf16)
        out_ref[:, (2 * pair + 1) * LANES:(2 * pair + 2) * LANES] = jnp.where(low, swapped, 0.0).astype(bf16)

    q = jnp.dot(h, wq_ref[...], preferred_element_type=f32)
    for pair in range(ATT_HEADS // 2):
        per_head(q, q_ref, pair, HEAD_DIM ** -0.5 * LOG2E)
    qi = jnp.dot(h, wqi_ref[...], preferred_element_type=f32)
    for pair in range(IDX_HEADS // 2):
        per_head(qi, qi_ref, pair, 1.0)

    k2 = jnp.dot(h, wk_ref[...], preferred_element_type=f32)
    for c in range(0, ATT_KV_HEADS * LANES, LANES):
        kc = _rope(k2[:, c:c + LANES], cos, sin, first_half)
        k_ref[0, :, c:c + LANES] = jnp.where(low, kc, 0.0).astype(bf16)
    ki2 = _rope(jnp.dot(h, wki_ref[...], preferred_element_type=f32), cos, sin, first_half)
    ki_ref[0] = jnp.where(low, ki2, 0.0).astype(bf16)

    vt = lax.dot_general(wv_ref[...], h, NT, preferred_element_type=f32)
    extra = (lax.broadcasted_iota(jnp.int32, (VT_ROWS - HEAD_DIM, tm), 0) == 0).astype(f32)
    for g in range(ATT_KV_HEADS):
        vt_ref[0, g * VT_ROWS:(g + 1) * VT_ROWS, :] = jnp.concatenate(
            [vt[g * HEAD_DIM:(g + 1) * HEAD_DIM], extra], axis=0).astype(bf16)
    wit_ref[...] = (lax.dot_general(wwi_ref[...], h, NT, preferred_element_type=f32)
                    * (IDX_HEADS ** -0.5 * HEAD_DIM ** -0.5))


def _dsa_proj(x2, g, wq, wk2, wv, wqi, wki2, wwi, cos, sin, batch, seq):
    m = x2.shape[0]
    nblk = seq // DSA_TM
    row = lambda n: pl.BlockSpec((DSA_TM, n), lambda b, i: (b * nblk + i, 0))
    seqrow = lambda n: pl.BlockSpec((1, DSA_TM, n), lambda b, i: (b, i, 0))
    tab = pl.BlockSpec((DSA_TM, LANES), lambda b, i: (i, 0))
    kvw = ATT_KV_HEADS * LANES
    vtr = ATT_KV_HEADS * VT_ROWS
    return pl.pallas_call(
        _dsa_proj_kernel,
        grid=(batch, nblk),
        in_specs=[row(D_MODEL), _const_spec((1, D_MODEL)),
                  _const_spec(wq.shape), _const_spec(wk2.shape), _const_spec(wv.shape),
                  _const_spec(wqi.shape), _const_spec(wki2.shape), _const_spec(wwi.shape),
                  tab, tab],
        out_specs=[row(ATT_HEADS * LANES), seqrow(kvw),
                   pl.BlockSpec((1, vtr, DSA_TM), lambda b, i: (b, 0, i)),
                   row(IDX_HEADS * LANES), seqrow(LANES),
                   pl.BlockSpec((LANES, DSA_TM), lambda b, i: (0, b * nblk + i))],
        out_shape=[jax.ShapeDtypeStruct((m, ATT_HEADS * LANES), bf16),
                   jax.ShapeDtypeStruct((batch, seq, kvw), bf16),
                   jax.ShapeDtypeStruct((batch, vtr, seq), bf16),
                   jax.ShapeDtypeStruct((m, IDX_HEADS * LANES), bf16),
                   jax.ShapeDtypeStruct((batch, seq, LANES), bf16),
                   jax.ShapeDtypeStruct((LANES, m), f32)],
        compiler_params=_cparams("parallel", "parallel"),
        name="dsa_proj",
    )(x2, g, wq, wk2, wv, wqi, wki2, wwi, cos, sin)


DSA_TQ = 128
DSA_CLS = 512
IDX_GROUP = 4
DSA_ROWS = 256
INT_MIN = -2 ** 31
NEG_INF_KEY = -2139095041


FOLD_ROWS = 64


def _fold_rows(x, op):
    rows = x.shape[0]
    if rows > FOLD_ROWS and rows % FOLD_ROWS == 0:
        acc = x[:FOLD_ROWS]
        for r in range(FOLD_ROWS, rows, FOLD_ROWS):
            acc = op(acc, x[r:r + FOLD_ROWS])
        x = acc
    while x.shape[0] % 16 == 0:
        h = x.shape[0] // 2
        x = op(x[:h], x[h:])
    return x


def _col_count(mask):
    return jnp.sum(_fold_rows(jnp.where(mask, 1.0, 0.0), jnp.add), axis=0, keepdims=True).astype(jnp.int32)


def _stack_heads(ref, first, count):
    return jnp.concatenate([ref[:, (first + j) * LANES:(first + j + 1) * LANES] for j in range(count)],
                           axis=0)


def _dsa_attn_body(q_ref, qi_ref, wit_ref, ki_ref, k_ref, vt_ref, o_ref,
                   okey_ref, bias_ref, logit_ref, *, width, top_k, t0):
    col_t = t0 + lax.broadcasted_iota(jnp.int32, (1, DSA_TQ), 1)
    row_s = lax.broadcasted_iota(jnp.int32, (width, 1), 0)
    causal = row_s <= col_t

    wit = wit_ref[...]
    ki = ki_ref[0, 0:width, :]
    score = jnp.zeros((width, DSA_TQ), f32)
    for first in range(0, IDX_HEADS, IDX_GROUP):
        rel = lax.dot_general(ki, _stack_heads(qi_ref, first, IDX_GROUP), NT,
                              preferred_element_type=f32)
        for j in range(IDX_GROUP):
            hd = first + j
            score = score + wit[hd:hd + 1, :] * jnp.maximum(rel[:, j * DSA_TQ:(j + 1) * DSA_TQ], 0.0)
    score = jnp.where(causal, score, -jnp.inf)

    def key_to_float(k):
        k = jnp.maximum(k, NEG_INF_KEY)
        return lax.bitcast_convert_type(k ^ ((k >> 31) & jnp.int32(0x7FFFFFFF)), f32)

    n_nonneg = _col_count(score >= 0.0)
    start_high = n_nonneg >= top_k
    thr = jnp.where(start_high, jnp.int32(0), jnp.int32(INT_MIN))
    n_ge = jnp.where(start_high, n_nonneg, jnp.int32(width))

    def value_step(i, state):
        thr, n_ge = state
        cand = thr | (jnp.int32(1) << (30 - i))
        cnt = _col_count(score >= key_to_float(cand))
        take = cnt >= top_k
        return jnp.where(take, cand, thr), jnp.where(take, cnt, n_ge)

    thr, n_ge = lax.fori_loop(0, 31, value_step, (thr, n_ge))

    thr_f = key_to_float(thr)
    bias_ref[0:width, :] = jnp.where((score >= thr_f) & causal, 0.0, -jnp.inf)
    overfull = (n_ge != top_k) & (thr > NEG_INF_KEY)

    @pl.when(jnp.max(overfull.astype(f32)) > 0.0)
    def _():
        real_thr = thr > NEG_INF_KEY
        next_f = key_to_float(thr + 1)
        beyond = score >= next_f
        in_bin = (score >= thr_f) & jnp.logical_not(beyond) & real_thr
        offset = jnp.where(in_bin, score - jnp.where(real_thr, thr_f, 0.0), -1.0)
        need = top_k - _col_count(beyond)

        okey_ref[...] = jnp.zeros(okey_ref.shape, jnp.int32)

        @pl.when(jnp.max(offset) > 0.0)
        def _():
            def offset_step(i, okey):
                cand = okey | (jnp.int32(1) << (30 - i))
                cnt = _col_count(offset >= lax.bitcast_convert_type(cand, f32))
                return jnp.where(cnt >= need, cand, okey)

            okey = lax.fori_loop(0, 31, offset_step, jnp.zeros((1, DSA_TQ), jnp.int32))
            okey_ref[...] = jnp.broadcast_to(okey, okey_ref.shape)

        othr = lax.bitcast_convert_type(okey_ref[0:1, :], f32)
        above = offset > othr
        tied = offset == othr
        need = need - _col_count(above)
        nbits = (width - 1).bit_length()

        def index_step(i, pos):
            cand = pos | (jnp.int32(1) << (nbits - 1 - i))
            return jnp.where(_col_count(tied & (row_s < cand)) < need, cand, pos)

        pos = lax.fori_loop(0, nbits, index_step, jnp.zeros((1, DSA_TQ), jnp.int32))
        keep = (beyond | above | (tied & (row_s <= pos))) & causal
        bias_ref[0:width, :] = jnp.where(keep, 0.0, -jnp.inf)

    group = ATT_HEADS // ATT_KV_HEADS
    for g in range(ATT_KV_HEADS):
        rq = _stack_heads(q_ref, g * group, group)
        maxes = [None] * group
        for rc in range(0, width, DSA_ROWS):
            logits4 = lax.dot_general(k_ref[0, rc:rc + DSA_ROWS, g * LANES:(g + 1) * LANES], rq, NT,
                                      preferred_element_type=f32)
            b = bias_ref[rc:rc + DSA_ROWS, :]
            for j in range(group):
                logits = logits4[:, j * DSA_TQ:(j + 1) * DSA_TQ] + b
                logit_ref[j, rc:rc + DSA_ROWS, :] = logits
                m = _fold_rows(logits, jnp.maximum)
                maxes[j] = m if maxes[j] is None else jnp.maximum(maxes[j], m)
        outs = []
        for j in range(group):
            mx = jnp.max(maxes[j], axis=0, keepdims=True)
            pv = None
            for rc in range(0, width, DSA_ROWS):
                p = jnp.exp2(logit_ref[j, rc:rc + DSA_ROWS, :] - mx).astype(bf16)
                part = jnp.dot(vt_ref[0, g * VT_ROWS:(g + 1) * VT_ROWS, rc:rc + DSA_ROWS], p,
                               preferred_element_type=f32)
                pv = part if pv is None else pv + part
            outs.append(pv[:HEAD_DIM] * (1.0 / pv[HEAD_DIM:HEAD_DIM + 1]))
        for pair in range(group // 2):
            both = jnp.concatenate(outs[2 * pair:2 * pair + 2], axis=0).T
            col = (g * group // 2 + pair) * LANES
            o_ref[:, col:col + LANES] = both.astype(bf16)


def _dsa_attn_kernel(q_ref, qi_ref, wit_ref, ki_ref, k_ref, vt_ref, o_ref,
                     okey_ref, bias_ref, logit_ref, *, seq, top_k):
    i = pl.program_id(1)
    blocks_per_class = DSA_CLS // DSA_TQ
    for cls in range(seq // DSA_CLS):
        @pl.when(i // blocks_per_class == cls)
        def _(cls=cls):
            _dsa_attn_body(q_ref, qi_ref, wit_ref, ki_ref, k_ref, vt_ref, o_ref,
                           okey_ref, bias_ref, logit_ref, width=DSA_CLS * (cls + 1), top_k=top_k,
                           t0=i * DSA_TQ)


def _dsa_attn(q, qi, wit, ki, k, vt, batch, seq):
    nq = seq // DSA_TQ
    top_k = min(TOPK_MAX, seq // 4)
    qrow = lambda n: pl.BlockSpec((DSA_TQ, n), lambda b, i: (b * nq + i, 0))
    full = lambda n: pl.BlockSpec((1, seq, n), lambda b, i: (b, 0, 0))
    return pl.pallas_call(
        functools.partial(_dsa_attn_kernel, seq=seq, top_k=top_k),
        grid=(batch, nq),
        in_specs=[qrow(ATT_HEADS * LANES), qrow(IDX_HEADS * LANES),
                  pl.BlockSpec((8, DSA_TQ), lambda b, i: (0, b * nq + i)),
                  full(LANES), full(ATT_KV_HEADS * LANES),
                  pl.BlockSpec((1, ATT_KV_HEADS * VT_ROWS, seq), lambda b, i: (b, 0, 0))],
        out_specs=qrow(ATT_HEADS * HEAD_DIM),
        out_shape=jax.ShapeDtypeStruct((batch * seq, ATT_HEADS * HEAD_DIM), bf16),
        scratch_shapes=[pltpu.VMEM((8, DSA_TQ), jnp.int32),
                        pltpu.VMEM((seq, DSA_TQ), f32),
                        pltpu.VMEM((ATT_HEADS // ATT_KV_HEADS, seq, DSA_TQ), f32)],
        compiler_params=_cparams("parallel", "arbitrary"),
        name="dsa_attn",
    )(q, qi, wit, ki, k, vt)


RW_TM = 256
LORA_PAD = 128
GATE_PAD = 256


def _rwkv_pre_kernel(x_ref, xp_ref, g_ref, mu_ref, wr_ref, wk_ref, wv_ref,
                     w1_ref, w2_ref, a1_ref, a2_ref, g1_ref, g2_ref,
                     w0_ref, a0_ref, kk_ref, ka_ref, rk_ref, tri_ref, hred_ref, hexp_ref,
                     at_ref, bt_ref, kt_ref, rt_ref, v_ref, gc_ref, bonus_ref, gate_ref,
                     lc_scr, *, seq):
    i = pl.program_id(0)
    g = g_ref[...]
    h = _rms(x_ref[...], g)
    hp = _rms(xp_ref[...], g)[7:8, :]
    hp = jnp.where((i * RW_TM) % seq == 0, 0.0, hp)
    rowi = lax.broadcasted_iota(jnp.int32, (RW_TM, 1), 0)
    hs = jnp.where(rowi == 0, hp, pltpu.roll(h, 1, 0))
    xx = hs - h
    mu = mu_ref[...]

    def mix(c):
        return (h + xx * mu[c:c + 1, :]).astype(bf16)

    r = jnp.dot(mix(0), wr_ref[...], preferred_element_type=f32)
    k = jnp.dot(mix(1), wk_ref[...], preferred_element_type=f32)
    v = jnp.dot(mix(2), wv_ref[...], preferred_element_type=f32)

    wl = w0_ref[...] + _bdot(jnp.tanh(jnp.dot(mix(3), w1_ref[...], preferred_element_type=f32)),
                             w2_ref[...])
    nwl = -wl
    softplus = jnp.maximum(nwl, 0.0) + jnp.log1p(jnp.exp(-jnp.abs(nwl)))
    ld = -jnp.exp(-softplus - 0.5)
    a = jax.nn.sigmoid(a0_ref[...] + _bdot(jnp.dot(mix(4), a1_ref[...],
                                                   preferred_element_type=f32), a2_ref[...]))
    gate_ref[...] = _bdot(jax.nn.sigmoid(jnp.dot(mix(5), g1_ref[...],
                                                 preferred_element_type=f32)), g2_ref[...]).astype(bf16)

    hred = hred_ref[...]
    hexp = hexp_ref[...]
    z = k * kk_ref[...]
    kk = z * lax.rsqrt(jnp.maximum(_head_sum(z * z, hred, hexp), 1e-24))
    k2 = k * (1.0 + (a - 1.0) * ka_ref[...])
    bonus_ref[...] = (_head_sum(r * k2 * rk_ref[...], hred, hexp) * v).astype(bf16)

    lc = _split_dot_left(tri_ref[...], ld, 2)
    lc_scr[...] = lc
    for c in range(RW_TM // CHUNK):
        last = lc_scr[c * CHUNK + CHUNK - 1:c * CHUNK + CHUNK, :]
        gc_ref[c * 8:(c + 1) * 8, :] = jnp.broadcast_to(jnp.exp(last), (8, D_MODEL))
    einv = jnp.exp(-lc)
    at_ref[...] = -kk * jnp.exp(lc - ld)
    bt_ref[...] = kk * a * einv
    kt_ref[...] = k2 * einv
    rt_ref[...] = r * jnp.exp(lc)
    v_ref[...] = v


def _split_dot_left(a, b, parts):
    out = None
    rem = b
    for p in range(parts):
        piece = rem.astype(bf16)
        term = jnp.dot(a, piece, preferred_element_type=f32)
        out = term if out is None else out + term
        if p + 1 < parts:
            rem = rem - piece.astype(f32)
    return out


def _rwkv_pre(x2, g, mu8, wr, wk, wv, w1, w2, a1, a2, g1, g2, w0, a0, k_k, k_a, r_k, tri, hred, hexp,
              seq):
    m = x2.shape[0]
    row = pl.BlockSpec((RW_TM, D_MODEL), lambda i: (i, 0))
    prev = pl.BlockSpec((8, D_MODEL), lambda i: (jnp.maximum(i * (RW_TM // 8) - 1, 0), 0))
    vec = _const_spec((1, D_MODEL))
    gcrow = pl.BlockSpec((RW_TM // CHUNK * 8, D_MODEL), lambda i: (i, 0))
    act = jax.ShapeDtypeStruct((m, D_MODEL), f32)
    return pl.pallas_call(
        functools.partial(_rwkv_pre_kernel, seq=seq),
        grid=(m // RW_TM,),
        in_specs=[row, prev, vec, _const_spec((8, D_MODEL)),
                  _const_spec(wr.shape), _const_spec(wk.shape), _const_spec(wv.shape),
                  _const_spec(w1.shape), _const_spec(w2.shape),
                  _const_spec(a1.shape), _const_spec(a2.shape),
                  _const_spec(g1.shape), _const_spec(g2.shape),
                  vec, vec, vec, vec, vec,
                  _const_spec(tri.shape), _const_spec(hred.shape), _const_spec(hexp.shape)],
        out_specs=[row, row, row, row, row, gcrow, row, row],
        out_shape=[act, act, act, act, act,
                   jax.ShapeDtypeStruct((m // CHUNK * 8, D_MODEL), f32),
                   jax.ShapeDtypeStruct((m, D_MODEL), bf16), jax.ShapeDtypeStruct((m, D_MODEL), bf16)],
        scratch_shapes=[pltpu.VMEM((RW_TM, D_MODEL), f32)],
        compiler_params=_cparams("parallel"),
        name="rwkv_pre",
    )(x2, x2, g, mu8, wr, wk, wv, w1, w2, a1, a2, g1, g2, w0, a0, k_k, k_a, r_k, tri, hred, hexp)


RA_GROUP = 2 * CHUNK
RA_ROWS = 1024


def _rwkv_chunk_kernel(at_ref, bt_ref, kt_ref, rt_ref, v_ref, gc_ref,
                       q_ref, yl_ref, m_ref, g_ref):
    n = RA_GROUP
    ri = lax.broadcasted_iota(jnp.int32, (n, n), 0)
    ci = lax.broadcasted_iota(jnp.int32, (n, n), 1)
    same = (ri // CHUNK) == (ci // CHUNK)
    strict = same & (ci < ri)
    incl = same & (ci <= ri)
    eye = (ri == ci).astype(f32)
    lane = lax.broadcasted_iota(jnp.int32, (1, LANES), 1)
    low = lane < RWKV_HEAD
    rk = lax.broadcasted_iota(jnp.int32, (RWKV_HEAD, LANES), 0)
    diag = rk == (lax.broadcasted_iota(jnp.int32, (RWKV_HEAD, LANES), 1) % RWKV_HEAD)

    groups = []
    for gi in range(RA_ROWS // n):
        rows = slice(gi * n, (gi + 1) * n)
        at, bt, kt, rt, v = (r[rows, :] for r in (at_ref, bt_ref, kt_ref, rt_ref, v_ref))
        groups.append(dict(at=at, bt=bt, kt=kt, rt=rt, v=v, vb=v.astype(bf16),
                           ar=jnp.concatenate([at, rt], axis=0),
                           bk=jnp.concatenate([bt, kt], axis=0).astype(bf16)))
    probs = [(gr, hl) for gr in groups for hl in (low, jnp.logical_not(low))]

    gs = [lax.dot_general(jnp.where(hl, gr["ar"], 0.0).astype(bf16), gr["bk"], NT,
                          preferred_element_type=f32) for gr, hl in probs]
    aab = [jnp.where(strict, g[:n, :n], 0.0) for g in gs]
    aak = [jnp.where(strict, g[:n, n:], 0.0) for g in gs]
    arb = [jnp.where(incl, g[n:, :n], 0.0) for g in gs]
    ark = [jnp.where(incl, g[n:, n:], 0.0) for g in gs]

    ts = [eye + a for a in aab]
    ps = aab
    for _ in range((CHUNK - 1).bit_length() - 1):
        ps = [_bdot(p, p) for p in ps]
        ts = [t + _bdot(t, p) for t, p in zip(ts, ps)]

    akv = [_bdot(a, gr["vb"]) for a, (gr, _) in zip(aak, probs)]
    tw = [_bdot(t, jnp.concatenate([gr["at"], x], axis=1))
          for t, x, (gr, _) in zip(ts, akv, probs)]
    qy = [_bdot(a, w) for a, w in zip(arb, tw)]
    rkv = [_bdot(a, gr["vb"]) for a, (gr, _) in zip(ark, probs)]

    for gi, gr in enumerate(groups):
        lo, hi = 2 * gi, 2 * gi + 1
        w_hat = jnp.where(low, tw[lo][:, :LANES], tw[hi][:, :LANES])
        u_loc = jnp.where(low, tw[lo][:, LANES:], tw[hi][:, LANES:])
        base = gi * n
        q_ref[base:base + n, :] = gr["rt"] + jnp.where(low, qy[lo][:, :LANES], qy[hi][:, :LANES])
        yl_ref[base:base + n, :] = (jnp.where(low, qy[lo][:, LANES:], qy[hi][:, LANES:])
                                    + jnp.where(low, rkv[lo], rkv[hi]))
        for c in range(n // CHUNK):
            rows = slice(c * CHUNK, (c + 1) * CHUNK)
            ch = base // CHUNK + c
            gc = gc_ref[ch * 8:ch * 8 + 1, :]
            bh = gr["bt"][rows] * gc
            kh = gr["kt"][rows] * gc
            pm = _bdot(w_hat[rows], bh, TN)
            pg = _bdot(jnp.concatenate([u_loc[rows], gr["v"][rows]], axis=0),
                       jnp.concatenate([bh, kh], axis=0), TN)
            out = slice(base + c * CHUNK, base + (c + 1) * CHUNK)
            m_ref[out, :] = (jnp.where(low, pm[:RWKV_HEAD], pm[RWKV_HEAD:])
                             + jnp.where(diag, gc, 0.0))
            g_ref[out, :] = jnp.where(low, pg[:RWKV_HEAD], pg[RWKV_HEAD:])


def _rwkv_chunk(at, bt, kt, rt, v, gc):
    m = at.shape[0]
    blk = pl.BlockSpec((RA_ROWS, LANES), lambda i, p: (i, p))
    gblk = pl.BlockSpec((RA_ROWS // CHUNK * 8, LANES), lambda i, p: (i, p))
    act = jax.ShapeDtypeStruct((m, D_MODEL), f32)
    return pl.pallas_call(
        _rwkv_chunk_kernel,
        grid=(m // RA_ROWS, D_MODEL // LANES),
        in_specs=[blk, blk, blk, blk, blk, gblk],
        out_specs=[blk, blk, blk, blk],
        out_shape=[act, act, act, act],
        compiler_params=_cparams("parallel", "parallel"),
        name="rwkv_chunk",
    )(at, bt, kt, rt, v, gc)


SCAN_LANES = 512


def _rwkv_scan_kernel(q_ref, yl_ref, m_ref, g_ref, y_ref, *, seq):
    pairs = SCAN_LANES // LANES
    low = lax.broadcasted_iota(jnp.int32, (1, LANES), 1) < RWKV_HEAD

    def blockdiag(x):
        return jnp.concatenate([jnp.where(low, x, 0.0), jnp.where(low, 0.0, x)], axis=0)

    def body(c, states):
        rows = pl.ds(pl.multiple_of(c * CHUNK, CHUNK), CHUNK)
        new_states = []
        for p in range(pairs):
            cols = slice(p * LANES, (p + 1) * LANES)
            s = states[p]
            y_ref[rows, cols] = yl_ref[rows, cols] + _bdot(q_ref[rows, cols], s, NT)
            mc = blockdiag(m_ref[rows, cols]).astype(bf16)
            s_hi = s.astype(bf16)
            s_lo = (s - s_hi.astype(f32)).astype(bf16)
            sm = (jnp.dot(s_hi, mc, preferred_element_type=f32)
                  + jnp.dot(s_lo, mc, preferred_element_type=f32))
            new_states.append(sm + blockdiag(g_ref[rows, cols]))
        return tuple(new_states)

    init = tuple(jnp.zeros((LANES, LANES), f32) for _ in range(pairs))
    lax.fori_loop(0, seq // CHUNK, body, init)


def _rwkv_scan(q, yl, mm, gg, batch, seq):
    blk = pl.BlockSpec((seq, SCAN_LANES), lambda b, p: (b, p))
    return pl.pallas_call(
        functools.partial(_rwkv_scan_kernel, seq=seq),
        grid=(batch, D_MODEL // SCAN_LANES),
        in_specs=[blk, blk, blk, blk],
        out_specs=blk,
        out_shape=jax.ShapeDtypeStruct((batch * seq, D_MODEL), f32),
        compiler_params=_cparams("parallel", "parallel"),
        name="rwkv_scan",
    )(q, yl, mm, gg)


def _dup_heads(w, n_heads):
    w = w.reshape(w.shape[0], n_heads, 1, HEAD_DIM)
    return jnp.broadcast_to(w, (w.shape[0], n_heads, 2, HEAD_DIM)).reshape(w.shape[0], n_heads * LANES)


def _pad_cols(w, n):
    return jnp.pad(w, ((0, 0), (0, n - w.shape[1])))


def _pad_rows(w, n):
    return jnp.pad(w, ((0, n - w.shape[0]), (0, 0)))


def _rope_tables(seq):
    inv = 1.0 / (ROPE_THETA ** (jnp.arange(0, HEAD_DIM, 2, dtype=f32) / HEAD_DIM))
    ang = jnp.arange(seq, dtype=f32)[:, None] * inv[None, :]
    cos, sin = jnp.cos(ang), jnp.sin(ang)
    return jnp.tile(jnp.concatenate([cos, cos], 1), (1, 2)), jnp.tile(jnp.concatenate([-sin, sin], 1), (1, 2))


def _dsa_layer(x2, g, w_in, w_o, batch, seq):
    o0 = ATT_HEADS * HEAD_DIM
    o1 = o0 + ATT_KV_HEADS * HEAD_DIM
    o2 = o1 + ATT_KV_HEADS * HEAD_DIM
    o3 = o2 + IDX_HEADS * HEAD_DIM
    o4 = o3 + HEAD_DIM
    w_in = w_in.astype(bf16)
    wq = w_in[:, :o0]
    wk2 = _dup_heads(w_in[:, o0:o1], ATT_KV_HEADS)
    wv = w_in[:, o1:o2].T
    wqi = w_in[:, o2:o3]
    wki2 = _dup_heads(w_in[:, o3:o4], 1)
    wwi = _pad_cols(w_in[:, o4:], LANES).T
    cos, sin = _rope_tables(seq)
    q, k, vt, qi, ki, wit = _dsa_proj(x2, g, wq, wk2, wv, wqi, wki2, wwi, cos, sin, batch, seq)
    o = _dsa_attn(q, qi, wit, ki, k, vt, batch, seq)
    return _attn_mlp_kernel, "attn_mlp", [o, x2], [w_o.astype(bf16)]


def _rwkv_layer(x2, g, mu, w_rkv, w0, w1, w2, a0, a1, a2, g1, g2, k_k, k_a, r_k, lnx_w, lnx_b, w_o,
                batch, seq):
    vec = lambda p: p.reshape(1, D_MODEL)
    w_rkv = w_rkv.astype(bf16)
    ri = jnp.arange(RW_TM)
    tri = ((ri[:, None] >= ri[None, :]) & (ri[:, None] // CHUNK == ri[None, :] // CHUNK)).astype(bf16)
    di = jnp.arange(D_MODEL)
    hred = (di[:, None] // RWKV_HEAD == jnp.arange(LANES)[None, :]).astype(bf16)
    hexp = hred.T
    at, bt, kt, rt, v, gc, bonus, gate = _rwkv_pre(
        x2, g, _pad_rows(mu, 8), w_rkv[0], w_rkv[1], w_rkv[2],
        _pad_cols(w1, LORA_PAD).astype(bf16), _pad_rows(w2, LORA_PAD).astype(bf16),
        _pad_cols(a1, LORA_PAD).astype(bf16), _pad_rows(a2, LORA_PAD).astype(bf16),
        _pad_cols(g1, GATE_PAD).astype(bf16), _pad_rows(g2, GATE_PAD).astype(bf16),
        vec(w0), vec(a0), vec(k_k), vec(k_a), vec(r_k), tri, hred, hexp, seq)
    q, yl, mm, gg = _rwkv_chunk(at, bt, kt, rt, v, gc)
    y = _rwkv_scan(q, yl, mm, gg, batch, seq)
    return (_rwkv_mlp_kernel, "rwkv_mlp", [y, bonus, gate, x2],
            [vec(lnx_w), vec(lnx_b), hred, hexp, w_o.astype(bf16)])


def kernel(x, mixer_norm, mlp_norm, mlp_w_up, mlp_w_down, final_norm, dsa_w_in, dsa_w_o, rwkv_mu, rwkv_w_rkv, rwkv_w0, rwkv_w1, rwkv_w2, rwkv_a0, rwkv_a1, rwkv_a2, rwkv_g1, rwkv_g2, rwkv_k_k, rwkv_k_a, rwkv_r_k, rwkv_lnx_w, rwkv_lnx_b, rwkv_w_o):
    batch, seq, _ = x.shape
    depth = mixer_norm.shape[0]
    x2 = x.reshape(batch * seq, D_MODEL)
    fg = final_norm.reshape(1, D_MODEL)
    for i in range(depth):
        g = mixer_norm[i].reshape(1, D_MODEL)
        j = i // 2
        if i % 2 == 0:
            tail = _dsa_layer(x2, g, dsa_w_in[j], dsa_w_o[j], batch, seq)
        else:
            tail = _rwkv_layer(x2, g, rwkv_mu[j], rwkv_w_rkv[j], rwkv_w0[j], rwkv_w1[j], rwkv_w2[j],
                               rwkv_a0[j], rwkv_a1[j], rwkv_a2[j], rwkv_g1[j], rwkv_g2[j],
                               rwkv_k_k[j], rwkv_k_a[j], rwkv_r_k[j], rwkv_lnx_w[j], rwkv_lnx_b[j],
                               rwkv_w_o[j], batch, seq)
        x2 = _mixer_mlp(*tail, mlp_norm[i].reshape(1, D_MODEL), mlp_w_up[i].astype(bf16),
                        mlp_w_down[i].astype(bf16), fg, final=(i == depth - 1))
    return x2.reshape(batch, seq, D_MODEL)
```

```python
import functools

import jax
import jax.numpy as jnp
from jax import lax
from jax.experimental import pallas as pl
from jax.experimental.pallas import tpu as pltpu

f32 = jnp.float32
bf16 = jnp.bfloat16

D_MODEL = 1024
D_FF = 4 * D_MODEL
NORM_EPS = 1e-6

ATT_HEADS = 16
ATT_KV_HEADS = 4
HEAD_DIM = 64
IDX_HEADS = 8
TOPK_MAX = 256
ROPE_THETA = 10000.0

RWKV_HEAD = 64
RWKV_HEADS = 16
LNX_EPS = 64e-5
CHUNK = 64

LANES = 128
VMEM_LIMIT = 56 * 1024 * 1024

NT = (((1,), (1,)), ((), ()))
TN = (((0,), (0,)), ((), ()))


def _cparams(*sem):
    return pltpu.CompilerParams(dimension_semantics=sem, vmem_limit_bytes=VMEM_LIMIT)


def _rms(x, g):
    return x * lax.rsqrt(jnp.mean(x * x, axis=-1, keepdims=True) + NORM_EPS) * g


def _bdot(a, b, dims=None):
    a = a.astype(bf16)
    b = b.astype(bf16)
    if dims is None:
        return jnp.dot(a, b, preferred_element_type=f32)
    return lax.dot_general(a, b, dims, preferred_element_type=f32)


def _split_dot(a, b, parts):
    out = None
    rem = a
    for p in range(parts):
        piece = rem.astype(bf16)
        term = jnp.dot(piece, b, preferred_element_type=f32)
        out = term if out is None else out + term
        if p + 1 < parts:
            rem = rem - piece.astype(f32)
    return out


def _head_sum(x, hred, hexp):
    red = jnp.dot(x.astype(bf16), hred, preferred_element_type=f32)
    return _split_dot(red, hexp, 2)


def _const_spec(shape):
    return pl.BlockSpec(shape, lambda *_: (0,) * len(shape))


MLP_TM = 512
MLP_FCH = 512


def _mlp_tail(x, g_ref, wu_ref, wd_ref, fg_ref, o_ref, final):
    xn = _rms(x, g_ref[...]).astype(bf16)
    o_ref[...] = x
    for f in range(0, D_FF, MLP_FCH):
        u = jnp.dot(xn, wu_ref[:, f:f + MLP_FCH], preferred_element_type=f32)
        u = jnp.maximum(u, 0.0)
        o_ref[...] += jnp.dot((u * u).astype(bf16), wd_ref[f:f + MLP_FCH, :],
                              preferred_element_type=f32)
    if final:
        o_ref[...] = _rms(o_ref[...], fg_ref[...])


def _attn_mlp_kernel(a_ref, x_ref, wo_ref, g_ref, wu_ref, wd_ref, fg_ref, o_ref, *, final):
    x = x_ref[...] + jnp.dot(a_ref[...], wo_ref[...], preferred_element_type=f32)
    _mlp_tail(x, g_ref, wu_ref, wd_ref, fg_ref, o_ref, final)


def _rwkv_mlp_kernel(y_ref, bonus_ref, gate_ref, x_ref, lw_ref, lb_ref, hred_ref, hexp_ref, wo_ref,
                     g_ref, wu_ref, wd_ref, fg_ref, o_ref, *, final):
    y = y_ref[...]
    hred = hred_ref[...]
    hexp = hexp_ref[...]
    mean = _head_sum(y, hred, hexp) * (1.0 / RWKV_HEAD)
    yc = y - mean
    var = _head_sum(yc * yc, hred, hexp) * (1.0 / RWKV_HEAD)
    yn = yc * lax.rsqrt(var + LNX_EPS) * lw_ref[...] + lb_ref[...]
    out = (yn + bonus_ref[...]) * gate_ref[...]
    x = x_ref[...] + jnp.dot(out.astype(bf16), wo_ref[...], preferred_element_type=f32)
    _mlp_tail(x, g_ref, wu_ref, wd_ref, fg_ref, o_ref, final)


def _resident(shape):
    return pl.BlockSpec(shape, lambda *_: (0,) * len(shape), pipeline_mode=pl.Buffered(1))


def _mixer_mlp(kernel, name, rows, consts, g, wu, wd, fg, final):
    m = rows[0].shape[0]
    row = pl.BlockSpec((MLP_TM, D_MODEL), lambda i: (i, 0))
    operands = list(rows) + list(consts) + [g, wu, wd, fg]
    return pl.pallas_call(
        functools.partial(kernel, final=final),
        grid=(m // MLP_TM,),
        in_specs=[row] * len(rows) + [_resident(c.shape) for c in operands[len(rows):]],
        out_specs=row,
        out_shape=jax.ShapeDtypeStruct((m, D_MODEL), f32),
        compiler_params=_cparams("parallel"),
        name=name,
    )(*operands)


DSA_TM = 256


def _rope(x, cos, sin_signed, first_half):
    fwd = pltpu.roll(x, 32, 1)
    bwd = pltpu.roll(x, 96, 1)
    return x * cos + jnp.where(first_half, bwd, fwd) * sin_signed


LOG2E = 1.4426950408889634


VT_ROWS = 80


def _dsa_proj_kernel(x_ref, g_ref, wq_ref, wk_ref, wv_ref, wqi_ref, wki_ref, wwi_ref,
                     cos_ref, sin_ref,
                     q_ref, k_ref, vt_ref, qi_ref, ki_ref, wit_ref):
    h = _rms(x_ref[...], g_ref[...]).astype(bf16)
    tm = h.shape[0]
    cos = cos_ref[...]
    sin = sin_ref[...]
    lane = lax.broadcasted_iota(jnp.int32, (1, LANES), 1)
    first_half = (lane % HEAD_DIM) < (HEAD_DIM // 2)
    low = lane < HEAD_DIM

    def per_head(x, out_ref, pair, scale):
        xc = _rope(x[:, pair * LANES:(pair + 1) * LANES], cos, sin, first_half) * scale
        swapped = pltpu.roll(xc, HEAD_DIM, 1)
        out_ref[:, (2 * pair) * LANES:(2 * pair + 1) * LANES] = jnp.where(low, xc, 0.0).astype(bf16)
        out_ref[:, (2 * pair + 1) * LANES:(2 * pair + 2) * LANES] = jnp.where(low, swapped, 0.0).astype(bf16)

    q = jnp.dot(h, wq_ref[...], preferred_element_type=f32)
    for pair in range(ATT_HEADS // 2):
        per_head(q, q_ref, pair, HEAD_DIM ** -0.5 * LOG2E)
    qi = jnp.dot(h, wqi_ref[...], preferred_element_type=f32)
    for pair in range(IDX_HEADS // 2):
        per_head(qi, qi_ref, pair, 1.0)

    k2 = jnp.dot(h, wk_ref[...], preferred_element_type=f32)
    for c in range(0, ATT_KV_HEADS * LANES, LANES):
        kc = _rope(k2[:, c:c + LANES], cos, sin, first_half)
        k_ref[0, :, c:c + LANES] = jnp.where(low, kc, 0.0).astype(bf16)
    ki2 = _rope(jnp.dot(h, wki_ref[...], preferred_element_type=f32), cos, sin, first_half)
    ki_ref[0] = jnp.where(low, ki2, 0.0).astype(bf16)

    vt = lax.dot_general(wv_ref[...], h, NT, preferred_element_type=f32)
    extra = (lax.broadcasted_iota(jnp.int32, (VT_ROWS - HEAD_DIM, tm), 0) == 0).astype(f32)
    for g in range(ATT_KV_HEADS):
        vt_ref[0, g * VT_ROWS:(g + 1) * VT_ROWS, :] = jnp.concatenate(
            [vt[g * HEAD_DIM:(g + 1) * HEAD_DIM], extra], axis=0).astype(bf16)
    wit_ref[...] = (lax.dot_general(wwi_ref[...], h, NT, preferred_element_type=f32)
                    * (IDX_HEADS ** -0.5 * HEAD_DIM ** -0.5))


def _dsa_proj(x2, g, wq, wk2, wv, wqi, wki2, wwi, cos, sin, batch, seq):
    m = x2.shape[0]
    nblk = seq // DSA_TM
    row = lambda n: pl.BlockSpec((DSA_TM, n), lambda b, i: (b * nblk + i, 0))
    seqrow = lambda n: pl.BlockSpec((1, DSA_TM, n), lambda b, i: (b, i, 0))
    tab = pl.BlockSpec((DSA_TM, LANES), lambda b, i: (i, 0))
    kvw = ATT_KV_HEADS * LANES
    vtr = ATT_KV_HEADS * VT_ROWS
    return pl.pallas_call(
        _dsa_proj_kernel,
        grid=(batch, nblk),
        in_specs=[row(D_MODEL), _const_spec((1, D_MODEL)),
                  _const_spec(wq.shape), _const_spec(wk2.shape), _const_spec(wv.shape),
                  _const_spec(wqi.shape), _const_spec(wki2.shape), _const_spec(wwi.shape),
                  tab, tab],
        out_specs=[row(ATT_HEADS * LANES), seqrow(kvw),
                   pl.BlockSpec((1, vtr, DSA_TM), lambda b, i: (b, 0, i)),
                   row(IDX_HEADS * LANES), seqrow(LANES),
                   pl.BlockSpec((LANES, DSA_TM), lambda b, i: (0, b * nblk + i))],
        out_shape=[jax.ShapeDtypeStruct((m, ATT_HEADS * LANES), bf16),
                   jax.ShapeDtypeStruct((batch, seq, kvw), bf16),
                   jax.ShapeDtypeStruct((batch, vtr, seq), bf16),
                   jax.ShapeDtypeStruct((m, IDX_HEADS * LANES), bf16),
                   jax.ShapeDtypeStruct((batch, seq, LANES), bf16),
                   jax.ShapeDtypeStruct((LANES, m), f32)],
        compiler_params=_cparams("parallel", "parallel"),
        name="dsa_proj",
    )(x2, g, wq, wk2, wv, wqi, wki2, wwi, cos, sin)


DSA_TQ = 128
DSA_CLS = 512
IDX_GROUP = 4
DSA_ROWS = 256
INT_MIN = -2 ** 31
NEG_INF_KEY = -2139095041


FOLD_ROWS = 64


def _fold_rows(x, op):
    rows = x.shape[0]
    if rows > FOLD_ROWS and rows % FOLD_ROWS == 0:
        acc = x[:FOLD_ROWS]
        for r in range(FOLD_ROWS, rows, FOLD_ROWS):
            acc = op(acc, x[r:r + FOLD_ROWS])
        x = acc
    while x.shape[0] % 16 == 0:
        h = x.shape[0] // 2
        x = op(x[:h], x[h:])
    return x


def _col_count(mask):
    return jnp.sum(_fold_rows(jnp.where(mask, 1.0, 0.0), jnp.add), axis=0, keepdims=True).astype(jnp.int32)


def _stack_heads(ref, first, count):
    return jnp.concatenate([ref[:, (first + j) * LANES:(first + j + 1) * LANES] for j in range(count)],
                           axis=0)


def _dsa_attn_body(q_ref, qi_ref, wit_ref, ki_ref, k_ref, vt_ref, o_ref,
                   okey_ref, bias_ref, logit_ref, *, width, top_k, t0):
    col_t = t0 + lax.broadcasted_iota(jnp.int32, (1, DSA_TQ), 1)
    row_s = lax.broadcasted_iota(jnp.int32, (width, 1), 0)
    causal = row_s <= col_t

    wit = wit_ref[...]
    ki = ki_ref[0, 0:width, :]
    score = jnp.zeros((width, DSA_TQ), f32)
    for first in range(0, IDX_HEADS, IDX_GROUP):
        rel = lax.dot_general(ki, _stack_heads(qi_ref, first, IDX_GROUP), NT,
                              preferred_element_type=f32)
        for j in range(IDX_GROUP):
            hd = first + j
            score = score + wit[hd:hd + 1, :] * jnp.maximum(rel[:, j * DSA_TQ:(j + 1) * DSA_TQ], 0.0)
    score = jnp.where(causal, score, -jnp.inf)

    def key_to_float(k):
        k = jnp.maximum(k, NEG_INF_KEY)
        return lax.bitcast_convert_type(k ^ ((k >> 31) & jnp.int32(0x7FFFFFFF)), f32)

    n_nonneg = _col_count(score >= 0.0)
    start_high = n_nonneg >= top_k
    thr = jnp.where(start_high, jnp.int32(0), jnp.int32(INT_MIN))
    n_ge = jnp.where(start_high, n_nonneg, jnp.int32(width))

    def value_step(i, state):
        thr, n_ge = state
        cand = thr | (jnp.int32(1) << (30 - i))
        cnt = _col_count(score >= key_to_float(cand))
        take = cnt >= top_k
        return jnp.where(take, cand, thr), jnp.where(take, cnt, n_ge)

    thr, n_ge = lax.fori_loop(0, 31, value_step, (thr, n_ge))

    thr_f = key_to_float(thr)
    bias_ref[0:width, :] = jnp.where((score >= thr_f) & causal, 0.0, -jnp.inf)
    overfull = (n_ge != top_k) & (thr > NEG_INF_KEY)

    @pl.when(jnp.max(overfull.astype(f32)) > 0.0)
    def _():
        real_thr = thr > NEG_INF_KEY
        next_f = key_to_float(thr + 1)
        beyond = score >= next_f
        in_bin = (score >= thr_f) & jnp.logical_not(beyond) & real_thr
        offset = jnp.where(in_bin, score - jnp.where(real_thr, thr_f, 0.0), -1.0)
        need = top_k - _col_count(beyond)

        okey_ref[...] = jnp.zeros(okey_ref.shape, jnp.int32)

        @pl.when(jnp.max(offset) > 0.0)
        def _():
            def offset_step(i, okey):
                cand = okey | (jnp.int32(1) << (30 - i))
                cnt = _col_count(offset >= lax.bitcast_convert_type(cand, f32))
                return jnp.where(cnt >= need, cand, okey)

            okey = lax.fori_loop(0, 31, offset_step, jnp.zeros((1, DSA_TQ), jnp.int32))
            okey_ref[...] = jnp.broadcast_to(okey, okey_ref.shape)

        othr = lax.bitcast_convert_type(okey_ref[0:1, :], f32)
        above = offset > othr
        tied = offset == othr
        need = need - _col_count(above)
        nbits = (width - 1).bit_length()

        def index_step(i, pos):
            cand = pos | (jnp.int32(1) << (nbits - 1 - i))
            return jnp.where(_col_count(tied & (row_s < cand)) < need, cand, pos)

        pos = lax.fori_loop(0, nbits, index_step, jnp.zeros((1, DSA_TQ), jnp.int32))
        keep = (beyond | above | (tied & (row_s <= pos))) & causal
        bias_ref[0:width, :] = jnp.where(keep, 0.0, -jnp.inf)

    group = ATT_HEADS // ATT_KV_HEADS
    for g in range(ATT_KV_HEADS):
        rq = _stack_heads(q_ref, g * group, group)
        maxes = [None] * group
        for rc in range(0, width, DSA_ROWS):
            logits4 = lax.dot_general(k_ref[0, rc:rc + DSA_ROWS, g * LANES:(g + 1) * LANES], rq, NT,
                                      preferred_element_type=f32)
            b = bias_ref[rc:rc + DSA_ROWS, :]
            for j in range(group):
                logits = logits4[:, j * DSA_TQ:(j + 1) * DSA_TQ] + b
                logit_ref[j, rc:rc + DSA_ROWS, :] = logits
                m = _fold_rows(logits, jnp.maximum)
                maxes[j] = m if maxes[j] is None else jnp.maximum(maxes[j], m)
        outs = []
        for j in range(group):
            mx = jnp.max(maxes[j], axis=0, keepdims=True)
            pv = None
            for rc in range(0, width, DSA_ROWS):
                p = jnp.exp2(logit_ref[j, rc:rc + DSA_ROWS, :] - mx).astype(bf16)
                part = jnp.dot(vt_ref[0, g * VT_ROWS:(g + 1) * VT_ROWS, rc:rc + DSA_ROWS], p,
                               preferred_element_type=f32)
                pv = part if pv is None else pv + part
            outs.append(pv[:HEAD_DIM] * (1.0 / pv[HEAD_DIM:HEAD_DIM + 1]))
        for pair in range(group // 2):
            both = jnp.concatenate(outs[2 * pair:2 * pair + 2], axis=0).T
            col = (g * group // 2 + pair) * LANES
            o_ref[:, col:col + LANES] = both.astype(bf16)


def _dsa_attn_kernel(q_ref, qi_ref, wit_ref, ki_ref, k_ref, vt_ref, o_ref,
                     okey_ref, bias_ref, logit_ref, *, seq, top_k):
    i = pl.program_id(1)
    blocks_per_class = DSA_CLS // DSA_TQ
    for cls in range(seq // DSA_CLS):
        @pl.when(i // blocks_per_class == cls)
        def _(cls=cls):
            _dsa_attn_body(q_ref, qi_ref, wit_ref, ki_ref, k_ref, vt_ref, o_ref,
                           okey_ref, bias_ref, logit_ref, width=DSA_CLS * (cls + 1), top_k=top_k,
                           t0=i * DSA_TQ)


def _dsa_attn(q, qi, wit, ki, k, vt, batch, seq):
    nq = seq // DSA_TQ
    top_k = min(TOPK_MAX, seq // 4)
    qrow = lambda n: pl.BlockSpec((DSA_TQ, n), lambda b, i: (b * nq + i, 0))
    full = lambda n: pl.BlockSpec((1, seq, n), lambda b, i: (b, 0, 0))
    return pl.pallas_call(
        functools.partial(_dsa_attn_kernel, seq=seq, top_k=top_k),
        grid=(batch, nq),
        in_specs=[qrow(ATT_HEADS * LANES), qrow(IDX_HEADS * LANES),
                  pl.BlockSpec((8, DSA_TQ), lambda b, i: (0, b * nq + i)),
                  full(LANES), full(ATT_KV_HEADS * LANES),
                  pl.BlockSpec((1, ATT_KV_HEADS * VT_ROWS, seq), lambda b, i: (b, 0, 0))],
        out_specs=qrow(ATT_HEADS * HEAD_DIM),
        out_shape=jax.ShapeDtypeStruct((batch * seq, ATT_HEADS * HEAD_DIM), bf16),
        scratch_shapes=[pltpu.VMEM((8, DSA_TQ), jnp.int32),
                        pltpu.VMEM((seq, DSA_TQ), f32),
                        pltpu.VMEM((ATT_HEADS // ATT_KV_HEADS, seq, DSA_TQ), f32)],
        compiler_params=_cparams("parallel", "arbitrary"),
        name="dsa_attn",
    )(q, qi, wit, ki, k, vt)


RW_TM = 256
LORA_PAD = 128
GATE_PAD = 256


def _rwkv_pre_kernel(x_ref, xp_ref, g_ref, mu_ref, wr_ref, wk_ref, wv_ref,
                     w1_ref, w2_ref, a1_ref, a2_ref, g1_ref, g2_ref,
                     w0_ref, a0_ref, kk_ref, ka_ref, rk_ref, tri_ref, hred_ref, hexp_ref,
                     at_ref, bt_ref, kt_ref, rt_ref, v_ref, gc_ref, bonus_ref, gate_ref,
                     lc_scr, *, seq):
    i = pl.program_id(0)
    g = g_ref[...]
    h = _rms(x_ref[...], g)
    hp = _rms(xp_ref[...], g)[7:8, :]
    hp = jnp.where((i * RW_TM) % seq == 0, 0.0, hp)
    rowi = lax.broadcasted_iota(jnp.int32, (RW_TM, 1), 0)
    hs = jnp.where(rowi == 0, hp, pltpu.roll(h, 1, 0))
    xx = hs - h
    mu = mu_ref[...]

    def mix(c):
        return (h + xx * mu[c:c + 1, :]).astype(bf16)

    r = jnp.dot(mix(0), wr_ref[...], preferred_element_type=f32)
    k = jnp.dot(mix(1), wk_ref[...], preferred_element_type=f32)
    v = jnp.dot(mix(2), wv_ref[...], preferred_element_type=f32)

    wl = w0_ref[...] + _bdot(jnp.tanh(jnp.dot(mix(3), w1_ref[...], preferred_element_type=f32)),
                             w2_ref[...])
    nwl = -wl
    softplus = jnp.maximum(nwl, 0.0) + jnp.log1p(jnp.exp(-jnp.abs(nwl)))
    ld = -jnp.exp(-softplus - 0.5)
    a = jax.nn.sigmoid(a0_ref[...] + _bdot(jnp.dot(mix(4), a1_ref[...],
                                                   preferred_element_type=f32), a2_ref[...]))
    gate_ref[...] = _bdot(jax.nn.sigmoid(jnp.dot(mix(5), g1_ref[...],
                                                 preferred_element_type=f32)), g2_ref[...]).astype(bf16)

    hred = hred_ref[...]
    hexp = hexp_ref[...]
    z = k * kk_ref[...]
    kk = z * lax.rsqrt(jnp.maximum(_head_sum(z * z, hred, hexp), 1e-24))
    k2 = k * (1.0 + (a - 1.0) * ka_ref[...])
    bonus_ref[...] = (_head_sum(r * k2 * rk_ref[...], hred, hexp) * v).astype(bf16)

    lc = _split_dot_left(tri_ref[...], ld, 2)
    lc_scr[...] = lc
    for c in range(RW_TM // CHUNK):
        last = lc_scr[c * CHUNK + CHUNK - 1:c * CHUNK + CHUNK, :]
        gc_ref[c * 8:(c + 1) * 8, :] = jnp.broadcast_to(jnp.exp(last), (8, D_MODEL))
    einv = jnp.exp(-lc)
    at_ref[...] = -kk * jnp.exp(lc - ld)
    bt_ref[...] = kk * a * einv
    kt_ref[...] = k2 * einv
    rt_ref[...] = r * jnp.exp(lc)
    v_ref[...] = v


def _split_dot_left(a, b, parts):
    out = None
    rem = b
    for p in range(parts):
        piece = rem.astype(bf16)
        term = jnp.dot(a, piece, preferred_element_type=f32)
        out = term if out is None else out + term
        if p + 1 < parts:
            rem = rem - piece.astype(f32)
    return out


def _rwkv_pre(x2, g, mu8, wr, wk, wv, w1, w2, a1, a2, g1, g2, w0, a0, k_k, k_a, r_k, tri, hred, hexp,
              seq):
    m = x2.shape[0]
    row = pl.BlockSpec((RW_TM, D_MODEL), lambda i: (i, 0))
    prev = pl.BlockSpec((8, D_MODEL), lambda i: (jnp.maximum(i * (RW_TM // 8) - 1, 0), 0))
    vec = _const_spec((1, D_MODEL))
    gcrow = pl.BlockSpec((RW_TM // CHUNK * 8, D_MODEL), lambda i: (i, 0))
    act = jax.ShapeDtypeStruct((m, D_MODEL), f32)
    return pl.pallas_call(
        functools.partial(_rwkv_pre_kernel, seq=seq),
        grid=(m // RW_TM,),
        in_specs=[row, prev, vec, _const_spec((8, D_MODEL)),
                  _const_spec(wr.shape), _const_spec(wk.shape), _const_spec(wv.shape),
                  _const_spec(w1.shape), _const_spec(w2.shape),
                  _const_spec(a1.shape), _const_spec(a2.shape),
                  _const_spec(g1.shape), _const_spec(g2.shape),
                  vec, vec, vec, vec, vec,
                  _const_spec(tri.shape), _const_spec(hred.shape), _const_spec(hexp.shape)],
        out_specs=[row, row, row, row, row, gcrow, row, row],
        out_shape=[act, act, act, act, act,
                   jax.ShapeDtypeStruct((m // CHUNK * 8, D_MODEL), f32),
                   jax.ShapeDtypeStruct((m, D_MODEL), bf16), jax.ShapeDtypeStruct((m, D_MODEL), bf16)],
        scratch_shapes=[pltpu.VMEM((RW_TM, D_MODEL), f32)],
        compiler_params=_cparams("parallel"),
        name="rwkv_pre",
    )(x2, x2, g, mu8, wr, wk, wv, w1, w2, a1, a2, g1, g2, w0, a0, k_k, k_a, r_k, tri, hred, hexp)


RA_GROUP = 2 * CHUNK
RA_ROWS = 1024


def _rwkv_chunk_kernel(at_ref, bt_ref, kt_ref, rt_ref, v_ref, gc_ref,
                       q_ref, yl_ref, m_ref, g_ref):
    n = RA_GROUP
    ri = lax.broadcasted_iota(jnp.int32, (n, n), 0)
    ci = lax.broadcasted_iota(jnp.int32, (n, n), 1)
    same = (ri // CHUNK) == (ci // CHUNK)
    strict = same & (ci < ri)
    incl = same & (ci <= ri)
    eye = (ri == ci).astype(f32)
    lane = lax.broadcasted_iota(jnp.int32, (1, LANES), 1)
    low = lane < RWKV_HEAD
    rk = lax.broadcasted_iota(jnp.int32, (RWKV_HEAD, LANES), 0)
    diag = rk == (lax.broadcasted_iota(jnp.int32, (RWKV_HEAD, LANES), 1) % RWKV_HEAD)

    groups = []
    for gi in range(RA_ROWS // n):
        rows = slice(gi * n, (gi + 1) * n)
        at, bt, kt, rt, v = (r[rows, :] for r in (at_ref, bt_ref, kt_ref, rt_ref, v_ref))
        groups.append(dict(at=at, bt=bt, kt=kt, rt=rt, v=v, vb=v.astype(bf16),
                           ar=jnp.concatenate([at, rt], axis=0),
                           bk=jnp.concatenate([bt, kt], axis=0).astype(bf16)))
    probs = [(gr, hl) for gr in groups for hl in (low, jnp.logical_not(low))]

    gs = [lax.dot_general(jnp.where(hl, gr["ar"], 0.0).astype(bf16), gr["bk"], NT,
                          preferred_element_type=f32) for gr, hl in probs]
    aab = [jnp.where(strict, g[:n, :n], 0.0) for g in gs]
    aak = [jnp.where(strict, g[:n, n:], 0.0) for g in gs]
    arb = [jnp.where(incl, g[n:, :n], 0.0) for g in gs]
    ark = [jnp.where(incl, g[n:, n:], 0.0) for g in gs]

    ts = [eye + a for a in aab]
    ps = aab
    for _ in range((CHUNK - 1).bit_length() - 1):
        ps = [_bdot(p, p) for p in ps]
        ts = [t + _bdot(t, p) for t, p in zip(ts, ps)]

    akv = [_bdot(a, gr["vb"]) for a, (gr, _) in zip(aak, probs)]
    tw = [_bdot(t, jnp.concatenate([gr["at"], x], axis=1))
          for t, x, (gr, _) in zip(ts, akv, probs)]
    qy = [_bdot(a, w) for a, w in zip(arb, tw)]
    rkv = [_bdot(a, gr["vb"]) for a, (gr, _) in zip(ark, probs)]

    for gi, gr in enumerate(groups):
        lo, hi = 2 * gi, 2 * gi + 1
        w_hat = jnp.where(low, tw[lo][:, :LANES], tw[hi][:, :LANES])
        u_loc = jnp.where(low, tw[lo][:, LANES:], tw[hi][:, LANES:])
        base = gi * n
        q_ref[base:base + n, :] = gr["rt"] + jnp.where(low, qy[lo][:, :LANES], qy[hi][:, :LANES])
        yl_ref[base:base + n, :] = (jnp.where(low, qy[lo][:, LANES:], qy[hi][:, LANES:])
                                    + jnp.where(low, rkv[lo], rkv[hi]))
        for c in range(n // CHUNK):
            rows = slice(c * CHUNK, (c + 1) * CHUNK)
            ch = base // CHUNK + c
            gc = gc_ref[ch * 8:ch * 8 + 1, :]
            bh = gr["bt"][rows] * gc
            kh = gr["kt"][rows] * gc
            pm = _bdot(w_hat[rows], bh, TN)
            pg = _bdot(jnp.concatenate([u_loc[rows], gr["v"][rows]], axis=0),
                       jnp.concatenate([bh, kh], axis=0), TN)
            out = slice(base + c * CHUNK, base + (c + 1) * CHUNK)
            m_ref[out, :] = (jnp.where(low, pm[:RWKV_HEAD], pm[RWKV_HEAD:])
                             + jnp.where(diag, gc, 0.0))
            g_ref[out, :] = jnp.where(low, pg[:RWKV_HEAD], pg[RWKV_HEAD:])


def _rwkv_chunk(at, bt, kt, rt, v, gc):
    m = at.shape[0]
    blk = pl.BlockSpec((RA_ROWS, LANES), lambda i, p: (i, p))
    gblk = pl.BlockSpec((RA_ROWS // CHUNK * 8, LANES), lambda i, p: (i, p))
    act = jax.ShapeDtypeStruct((m, D_MODEL), f32)
    return pl.pallas_call(
        _rwkv_chunk_kernel,
        grid=(m // RA_ROWS, D_MODEL // LANES),
        in_specs=[blk, blk, blk, blk, blk, gblk],
        out_specs=[blk, blk, blk, blk],
        out_shape=[act, act, act, act],
        compiler_params=_cparams("parallel", "parallel"),
        name="rwkv_chunk",
    )(at, bt, kt, rt, v, gc)


SCAN_LANES = D_MODEL
SCAN_SPLIT = 2


def _rwkv_scan_kernel(q_ref, yl_ref, m_ref, g_ref, y_ref, state_ref, *, seq):
    pairs = SCAN_LANES // LANES
    low = lax.broadcasted_iota(jnp.int32, (1, LANES), 1) < RWKV_HEAD

    def blockdiag(x):
        return jnp.concatenate([jnp.where(low, x, 0.0), jnp.where(low, 0.0, x)], axis=0)

    def body(c, states):
        rows = pl.ds(pl.multiple_of(c * CHUNK, CHUNK), CHUNK)
        new_states = []
        for p in range(pairs):
            cols = slice(p * LANES, (p + 1) * LANES)
            s = states[p]
            mc = blockdiag(m_ref[rows, cols]).astype(bf16)
            s_hi = s.astype(bf16)
            s_lo = (s - s_hi.astype(f32)).astype(bf16)
            sm = (jnp.dot(s_hi, mc, preferred_element_type=f32)
                  + jnp.dot(s_lo, mc, preferred_element_type=f32))
            new_states.append(sm + g_ref[rows, cols])
        for p in range(pairs):
            cols = slice(p * LANES, (p + 1) * LANES)
            y_ref[rows, cols] = (yl_ref[rows, cols]
                                 + _bdot(q_ref[rows, cols], blockdiag(states[p]), NT))
        return tuple(new_states)

    @pl.when(pl.program_id(1) == 0)
    def _():
        state_ref[...] = jnp.zeros(state_ref.shape, f32)

    init = tuple(state_ref[p] for p in range(pairs))
    final = lax.fori_loop(0, seq // SCAN_SPLIT // CHUNK, body, init)
    for p in range(pairs):
        state_ref[p] = final[p]


def _rwkv_scan(q, yl, mm, gg, batch, seq):
    rows = seq // SCAN_SPLIT
    blk = pl.BlockSpec((rows, SCAN_LANES), lambda b, s: (b * SCAN_SPLIT + s, 0))
    return pl.pallas_call(
        functools.partial(_rwkv_scan_kernel, seq=seq),
        grid=(batch, SCAN_SPLIT),
        in_specs=[blk, blk, blk, blk],
        out_specs=blk,
        out_shape=jax.ShapeDtypeStruct((batch * seq, D_MODEL), f32),
        scratch_shapes=[pltpu.VMEM((SCAN_LANES // LANES, RWKV_HEAD, LANES), f32)],
        compiler_params=_cparams("parallel", "arbitrary"),
        name="rwkv_scan",
    )(q, yl, mm, gg)


def _dup_heads(w, n_heads):
    w = w.reshape(w.shape[0], n_heads, 1, HEAD_DIM)
    return jnp.broadcast_to(w, (w.shape[0], n_heads, 2, HEAD_DIM)).reshape(w.shape[0], n_heads * LANES)


def _pad_cols(w, n):
    return jnp.pad(w, ((0, 0), (0, n - w.shape[1])))


def _pad_rows(w, n):
    return jnp.pad(w, ((0, n - w.shape[0]), (0, 0)))


def _rope_tables(seq):
    inv = 1.0 / (ROPE_THETA ** (jnp.arange(0, HEAD_DIM, 2, dtype=f32) / HEAD_DIM))
    ang = jnp.arange(seq, dtype=f32)[:, None] * inv[None, :]
    cos, sin = jnp.cos(ang), jnp.sin(ang)
    return jnp.tile(jnp.concatenate([cos, cos], 1), (1, 2)), jnp.tile(jnp.concatenate([-sin, sin], 1), (1, 2))


def _dsa_layer(x2, g, w_in, w_o, batch, seq):
    o0 = ATT_HEADS * HEAD_DIM
    o1 = o0 + ATT_KV_HEADS * HEAD_DIM
    o2 = o1 + ATT_KV_HEADS * HEAD_DIM
    o3 = o2 + IDX_HEADS * HEAD_DIM
    o4 = o3 + HEAD_DIM
    w_in = w_in.astype(bf16)
    wq = w_in[:, :o0]
    wk2 = _dup_heads(w_in[:, o0:o1], ATT_KV_HEADS)
    wv = w_in[:, o1:o2].T
    wqi = w_in[:, o2:o3]
    wki2 = _dup_heads(w_in[:, o3:o4], 1)
    wwi = _pad_cols(w_in[:, o4:], LANES).T
    cos, sin = _rope_tables(seq)
    q, k, vt, qi, ki, wit = _dsa_proj(x2, g, wq, wk2, wv, wqi, wki2, wwi, cos, sin, batch, seq)
    o = _dsa_attn(q, qi, wit, ki, k, vt, batch, seq)
    return _attn_mlp_kernel, "attn_mlp", [o, x2], [w_o.astype(bf16)]


def _rwkv_layer(x2, g, mu, w_rkv, w0, w1, w2, a0, a1, a2, g1, g2, k_k, k_a, r_k, lnx_w, lnx_b, w_o,
                batch, seq):
    vec = lambda p: p.reshape(1, D_MODEL)
    w_rkv = w_rkv.astype(bf16)
    ri = jnp.arange(RW_TM)
    tri = ((ri[:, None] >= ri[None, :]) & (ri[:, None] // CHUNK == ri[None, :] // CHUNK)).astype(bf16)
    di = jnp.arange(D_MODEL)
    hred = (di[:, None] // RWKV_HEAD == jnp.arange(LANES)[None, :]).astype(bf16)
    hexp = hred.T
    at, bt, kt, rt, v, gc, bonus, gate = _rwkv_pre(
        x2, g, _pad_rows(mu, 8), w_rkv[0], w_rkv[1], w_rkv[2],
        _pad_cols(w1, LORA_PAD).astype(bf16), _pad_rows(w2, LORA_PAD).astype(bf16),
        _pad_cols(a1, LORA_PAD).astype(bf16), _pad_rows(a2, LORA_PAD).astype(bf16),
        _pad_cols(g1, GATE_PAD).astype(bf16), _pad_rows(g2, GATE_PAD).astype(bf16),
        vec(w0), vec(a0), vec(k_k), vec(k_a), vec(r_k), tri, hred, hexp, seq)
    q, yl, mm, gg = _rwkv_chunk(at, bt, kt, rt, v, gc)
    y = _rwkv_scan(q, yl, mm, gg, batch, seq)
    return (_rwkv_mlp_kernel, "rwkv_mlp", [y, bonus, gate, x2],
            [vec(lnx_w), vec(lnx_b), hred, hexp, w_o.astype(bf16)])


def kernel(x, mixer_norm, mlp_norm, mlp_w_up, mlp_w_down, final_norm, dsa_w_in, dsa_w_o, rwkv_mu, rwkv_w_rkv, rwkv_w0, rwkv_w1, rwkv_w2, rwkv_a0, rwkv_a1, rwkv_a2, rwkv_g1, rwkv_g2, rwkv_k_k, rwkv_k_a, rwkv_r_k, rwkv_lnx_w, rwkv_lnx_b, rwkv_w_o):
    batch, seq, _ = x.shape
    depth = mixer_norm.shape[0]
    x2 = x.reshape(batch * seq, D_MODEL)
    fg = final_norm.reshape(1, D_MODEL)
    for i in range(depth):
        g = mixer_norm[i].reshape(1, D_MODEL)
        j = i // 2
        if i % 2 == 0:
            tail = _dsa_layer(x2, g, dsa_w_in[j], dsa_w_o[j], batch, seq)
        else:
            tail = _rwkv_layer(x2, g, rwkv_mu[j], rwkv_w_rkv[j], rwkv_w0[j], rwkv_w1[j], rwkv_w2[j],
                               rwkv_a0[j], rwkv_a1[j], rwkv_a2[j], rwkv_g1[j], rwkv_g2[j],
                               rwkv_k_k[j], rwkv_k_a[j], rwkv_r_k[j], rwkv_lnx_w[j], rwkv_lnx_b[j],
                               rwkv_w_o[j], batch, seq)
        x2 = _mixer_mlp(*tail, mlp_norm[i].reshape(1, D_MODEL), mlp_w_up[i].astype(bf16),
                        mlp_w_down[i].astype(bf16), fg, final=(i == depth - 1))
    return x2.reshape(batch, seq, D_MODEL)
```

```python
import functools

import jax
import jax.numpy as jnp
from jax import lax
from jax.experimental import pallas as pl
from jax.experimental.pallas import tpu as pltpu

f32 = jnp.float32
bf16 = jnp.bfloat16

D_MODEL = 1024
D_FF = 4 * D_MODEL
NORM_EPS = 1e-6

ATT_HEADS = 16
ATT_KV_HEADS = 4
HEAD_DIM = 64
IDX_HEADS = 8
TOPK_MAX = 256
ROPE_THETA = 10000.0

RWKV_HEAD = 64
RWKV_HEADS = 16
LNX_EPS = 64e-5
CHUNK = 64

LANES = 128
VMEM_LIMIT = 56 * 1024 * 1024

NT = (((1,), (1,)), ((), ()))
TN = (((0,), (0,)), ((), ()))


def _cparams(*sem):
    return pltpu.CompilerParams(dimension_semantics=sem, vmem_limit_bytes=VMEM_LIMIT)


def _rms(x, g):
    return x * lax.rsqrt(jnp.mean(x * x, axis=-1, keepdims=True) + NORM_EPS) * g


def _bdot(a, b, dims=None):
    a = a.astype(bf16)
    b = b.astype(bf16)
    if dims is None:
        return jnp.dot(a, b, preferred_element_type=f32)
    return lax.dot_general(a, b, dims, preferred_element_type=f32)


def _split_dot(a, b, parts):
    out = None
    rem = a
    for p in range(parts):
        piece = rem.astype(bf16)
        term = jnp.dot(piece, b, preferred_element_type=f32)
        out = term if out is None else out + term
        if p + 1 < parts:
            rem = rem - piece.astype(f32)
    return out


def _head_sum(x, hred, hexp):
    red = jnp.dot(x.astype(bf16), hred, preferred_element_type=f32)
    hi = red.astype(bf16)
    lo = (red - hi.astype(f32)).astype(bf16)
    return jnp.dot(jnp.concatenate([hi, lo], axis=1), hexp, preferred_element_type=f32)


def _const_spec(shape):
    return pl.BlockSpec(shape, lambda *_: (0,) * len(shape))


MLP_TM = 512
MLP_FCH = 512


def _mlp_tail(x, g_ref, wu_ref, wd_ref, fg_ref, o_ref, final):
    xn = _rms(x, g_ref[...]).astype(bf16)
    o_ref[...] = x
    for f in range(0, D_FF, MLP_FCH):
        u = jnp.dot(xn, wu_ref[:, f:f + MLP_FCH], preferred_element_type=f32)
        u = jnp.maximum(u, 0.0)
        o_ref[...] += jnp.dot((u * u).astype(bf16), wd_ref[f:f + MLP_FCH, :],
                              preferred_element_type=f32)
    if final:
        o_ref[...] = _rms(o_ref[...], fg_ref[...])


def _attn_mlp_kernel(a_ref, x_ref, wo_ref, g_ref, wu_ref, wd_ref, fg_ref, o_ref, *, final):
    x = x_ref[...] + jnp.dot(a_ref[...], wo_ref[...], preferred_element_type=f32)
    _mlp_tail(x, g_ref, wu_ref, wd_ref, fg_ref, o_ref, final)


def _rwkv_mlp_kernel(y_ref, bonus_ref, gate_ref, x_ref, lw_ref, lb_ref, hred_ref, hexp_ref, wo_ref,
                     g_ref, wu_ref, wd_ref, fg_ref, o_ref, *, final):
    y = y_ref[...]
    hred = hred_ref[...]
    hexp = hexp_ref[...]
    mean = _head_sum(y, hred, hexp) * (1.0 / RWKV_HEAD)
    yc = y - mean
    var = _head_sum(yc * yc, hred, hexp) * (1.0 / RWKV_HEAD)
    yn = yc * lax.rsqrt(var + LNX_EPS) * lw_ref[...] + lb_ref[...]
    out = (yn + bonus_ref[...]) * gate_ref[...]
    x = x_ref[...] + jnp.dot(out.astype(bf16), wo_ref[...], preferred_element_type=f32)
    _mlp_tail(x, g_ref, wu_ref, wd_ref, fg_ref, o_ref, final)


def _resident(shape):
    return pl.BlockSpec(shape, lambda *_: (0,) * len(shape), pipeline_mode=pl.Buffered(1))


def _resident_layer(shape, layer):
    return pl.BlockSpec((None,) + tuple(shape[1:]), lambda *_: (layer,) + (0,) * (len(shape) - 1),
                        pipeline_mode=pl.Buffered(1))


def _mixer_mlp(kernel, name, rows, consts, g, wu_all, wd_all, layer, fg, final):
    m = rows[0].shape[0]
    row = pl.BlockSpec((MLP_TM, D_MODEL), lambda i: (i, 0))
    operands = list(rows) + list(consts) + [g, wu_all, wd_all, fg]
    return pl.pallas_call(
        functools.partial(kernel, final=final),
        grid=(m // MLP_TM,),
        in_specs=([row] * len(rows) + [_resident(c.shape) for c in list(consts) + [g]]
                  + [_resident_layer(wu_all.shape, layer), _resident_layer(wd_all.shape, layer),
                     _resident(fg.shape)]),
        out_specs=row,
        out_shape=jax.ShapeDtypeStruct((m, D_MODEL), f32),
        compiler_params=_cparams("parallel"),
        name=name,
    )(*operands)


DSA_TM = 256


def _rope(x, cos, sin_signed, first_half):
    fwd = pltpu.roll(x, 32, 1)
    bwd = pltpu.roll(x, 96, 1)
    return x * cos + jnp.where(first_half, bwd, fwd) * sin_signed


LOG2E = 1.4426950408889634


VT_ROWS = 80


def _dsa_proj_kernel(x_ref, g_ref, wq_ref, wk_ref, wv_ref, wqi_ref, wki_ref, wwi_ref,
                     cos_ref, sin_ref,
                     q_ref, k_ref, vt_ref, qi_ref, ki_ref, wit_ref):
    h = _rms(x_ref[...], g_ref[...]).astype(bf16)
    tm = h.shape[0]
    cos = cos_ref[...]
    sin = sin_ref[...]
    lane = lax.broadcasted_iota(jnp.int32, (1, LANES), 1)
    first_half = (lane % HEAD_DIM) < (HEAD_DIM // 2)
    low = lane < HEAD_DIM

    def per_head(x, out_ref, pair, scale):
        xc = _rope(x[:, pair * LANES:(pair + 1) * LANES], cos, sin, first_half) * scale
        swapped = pltpu.roll(xc, HEAD_DIM, 1)
        out_ref[:, (2 * pair) * LANES:(2 * pair + 1) * LANES] = jnp.where(low, xc, 0.0).astype(bf16)
        out_ref[:, (2 * pair + 1) * LANES:(2 * pair + 2) * LANES] = jnp.where(low, swapped, 0.0).astype(bf16)

    q = jnp.dot(h, wq_ref[...], preferred_element_type=f32)
    for pair in range(ATT_HEADS // 2):
        per_head(q, q_ref, pair, HEAD_DIM ** -0.5 * LOG2E)
    qi = jnp.dot(h, wqi_ref[...], preferred_element_type=f32)
    for pair in range(IDX_HEADS // 2):
        per_head(qi, qi_ref, pair, 1.0)

    k2 = jnp.dot(h, wk_ref[...], preferred_element_type=f32)
    for c in range(0, ATT_KV_HEADS * LANES, LANES):
        kc = _rope(k2[:, c:c + LANES], cos, sin, first_half)
        k_ref[0, :, c:c + LANES] = jnp.where(low, kc, 0.0).astype(bf16)
    ki2 = _rope(jnp.dot(h, wki_ref[...], preferred_element_type=f32), cos, sin, first_half)
    ki_ref[0] = jnp.where(low, ki2, 0.0).astype(bf16)

    vt = lax.dot_general(wv_ref[...], h, NT, preferred_element_type=f32)
    extra = (lax.broadcasted_iota(jnp.int32, (VT_ROWS - HEAD_DIM, tm), 0) == 0).astype(f32)
    for g in range(ATT_KV_HEADS):
        vt_ref[0, g * VT_ROWS:(g + 1) * VT_ROWS, :] = jnp.concatenate(
            [vt[g * HEAD_DIM:(g + 1) * HEAD_DIM], extra], axis=0).astype(bf16)
    wit_ref[...] = (lax.dot_general(wwi_ref[...], h, NT, preferred_element_type=f32)
                    * (IDX_HEADS ** -0.5 * HEAD_DIM ** -0.5))


def _dsa_proj(x2, g, wq, wk2, wv, wqi, wki2, wwi, cos, sin, batch, seq):
    m = x2.shape[0]
    nblk = seq // DSA_TM
    row = lambda n: pl.BlockSpec((DSA_TM, n), lambda b, i: (b * nblk + i, 0))
    seqrow = lambda n: pl.BlockSpec((1, DSA_TM, n), lambda b, i: (b, i, 0))
    tab = pl.BlockSpec((DSA_TM, LANES), lambda b, i: (i, 0))
    kvw = ATT_KV_HEADS * LANES
    vtr = ATT_KV_HEADS * VT_ROWS
    return pl.pallas_call(
        _dsa_proj_kernel,
        grid=(batch, nblk),
        in_specs=[row(D_MODEL), _const_spec((1, D_MODEL)),
                  _const_spec(wq.shape), _const_spec(wk2.shape), _const_spec(wv.shape),
                  _const_spec(wqi.shape), _const_spec(wki2.shape), _const_spec(wwi.shape),
                  tab, tab],
        out_specs=[row(ATT_HEADS * LANES), seqrow(kvw),
                   pl.BlockSpec((1, vtr, DSA_TM), lambda b, i: (b, 0, i)),
                   row(IDX_HEADS * LANES), seqrow(LANES),
                   pl.BlockSpec((LANES, DSA_TM), lambda b, i: (0, b * nblk + i))],
        out_shape=[jax.ShapeDtypeStruct((m, ATT_HEADS * LANES), bf16),
                   jax.ShapeDtypeStruct((batch, seq, kvw), bf16),
                   jax.ShapeDtypeStruct((batch, vtr, seq), bf16),
                   jax.ShapeDtypeStruct((m, IDX_HEADS * LANES), bf16),
                   jax.ShapeDtypeStruct((batch, seq, LANES), bf16),
                   jax.ShapeDtypeStruct((LANES, m), f32)],
        compiler_params=_cparams("parallel", "parallel"),
        name="dsa_proj",
    )(x2, g, wq, wk2, wv, wqi, wki2, wwi, cos, sin)


DSA_TQ = 128
DSA_CLS = 512
IDX_GROUP = 4
DSA_ROWS = 256
INT_MIN = -2 ** 31
NEG_INF_KEY = -2139095041


FOLD_ROWS = 64


def _fold_rows(x, op):
    rows = x.shape[0]
    if rows > FOLD_ROWS and rows % FOLD_ROWS == 0:
        acc = x[:FOLD_ROWS]
        for r in range(FOLD_ROWS, rows, FOLD_ROWS):
            acc = op(acc, x[r:r + FOLD_ROWS])
        x = acc
    while x.shape[0] % 16 == 0:
        h = x.shape[0] // 2
        x = op(x[:h], x[h:])
    return x


def _col_count(mask):
    return jnp.sum(_fold_rows(jnp.where(mask, 1.0, 0.0), jnp.add), axis=0, keepdims=True).astype(jnp.int32)


def _stack_heads(ref, first, count):
    return jnp.concatenate([ref[:, (first + j) * LANES:(first + j + 1) * LANES] for j in range(count)],
                           axis=0)


def _dsa_attn_body(q_ref, qi_ref, wit_ref, ki_ref, k_ref, vt_ref, o_ref,
                   okey_ref, bias_ref, logit_ref, *, width, top_k, t0):
    col_t = t0 + lax.broadcasted_iota(jnp.int32, (1, DSA_TQ), 1)
    row_s = lax.broadcasted_iota(jnp.int32, (width, 1), 0)
    causal = row_s <= col_t

    wit = wit_ref[...]
    ki = ki_ref[0, 0:width, :]
    score = jnp.zeros((width, DSA_TQ), f32)
    for first in range(0, IDX_HEADS, IDX_GROUP):
        rel = lax.dot_general(ki, _stack_heads(qi_ref, first, IDX_GROUP), NT,
                              preferred_element_type=f32)
        for j in range(IDX_GROUP):
            hd = first + j
            score = score + wit[hd:hd + 1, :] * jnp.maximum(rel[:, j * DSA_TQ:(j + 1) * DSA_TQ], 0.0)
    score = jnp.where(causal, score, -jnp.inf)

    def key_to_float(k):
        k = jnp.maximum(k, NEG_INF_KEY)
        return lax.bitcast_convert_type(k ^ ((k >> 31) & jnp.int32(0x7FFFFFFF)), f32)

    n_nonneg = _col_count(score >= 0.0)
    start_high = n_nonneg >= top_k
    thr = jnp.where(start_high, jnp.int32(0), jnp.int32(INT_MIN))
    n_ge = jnp.where(start_high, n_nonneg, jnp.int32(width))

    def value_step(i, state):
        thr, n_ge = state
        cand = thr | (jnp.int32(1) << (30 - i))
        cnt = _col_count(score >= key_to_float(cand))
        take = cnt >= top_k
        return jnp.where(take, cand, thr), jnp.where(take, cnt, n_ge)

    thr, n_ge = lax.fori_loop(0, 31, value_step, (thr, n_ge))

    thr_f = key_to_float(thr)
    bias_ref[0:width, :] = jnp.where((score >= thr_f) & causal, 0.0, -jnp.inf)
    overfull = (n_ge != top_k) & (thr > NEG_INF_KEY)

    @pl.when(jnp.max(overfull.astype(f32)) > 0.0)
    def _():
        real_thr = thr > NEG_INF_KEY
        next_f = key_to_float(thr + 1)
        beyond = score >= next_f
        in_bin = (score >= thr_f) & jnp.logical_not(beyond) & real_thr
        offset = jnp.where(in_bin, score - jnp.where(real_thr, thr_f, 0.0), -1.0)
        need = top_k - _col_count(beyond)

        okey_ref[...] = jnp.zeros(okey_ref.shape, jnp.int32)

        @pl.when(jnp.max(offset) > 0.0)
        def _():
            def offset_step(i, okey):
                cand = okey | (jnp.int32(1) << (30 - i))
                cnt = _col_count(offset >= lax.bitcast_convert_type(cand, f32))
                return jnp.where(cnt >= need, cand, okey)

            okey = lax.fori_loop(0, 31, offset_step, jnp.zeros((1, DSA_TQ), jnp.int32))
            okey_ref[...] = jnp.broadcast_to(okey, okey_ref.shape)

        othr = lax.bitcast_convert_type(okey_ref[0:1, :], f32)
        above = offset > othr
        tied = offset == othr
        need = need - _col_count(above)
        nbits = (width - 1).bit_length()

        def index_step(i, pos):
            cand = pos | (jnp.int32(1) << (nbits - 1 - i))
            return jnp.where(_col_count(tied & (row_s < cand)) < need, cand, pos)

        pos = lax.fori_loop(0, nbits, index_step, jnp.zeros((1, DSA_TQ), jnp.int32))
        keep = (beyond | above | (tied & (row_s <= pos))) & causal
        bias_ref[0:width, :] = jnp.where(keep, 0.0, -jnp.inf)

    group = ATT_HEADS // ATT_KV_HEADS
    for g in range(ATT_KV_HEADS):
        rq = _stack_heads(q_ref, g * group, group)
        maxes = [None] * group
        for rc in range(0, width, DSA_ROWS):
            logits4 = lax.dot_general(k_ref[0, rc:rc + DSA_ROWS, g * LANES:(g + 1) * LANES], rq, NT,
                                      preferred_element_type=f32)
            b = bias_ref[rc:rc + DSA_ROWS, :]
            for j in range(group):
                logits = logits4[:, j * DSA_TQ:(j + 1) * DSA_TQ] + b
                logit_ref[j, rc:rc + DSA_ROWS, :] = logits
                m = _fold_rows(logits, jnp.maximum)
                maxes[j] = m if maxes[j] is None else jnp.maximum(maxes[j], m)
        outs = []
        for j in range(group):
            mx = jnp.max(maxes[j], axis=0, keepdims=True)
            pv = None
            for rc in range(0, width, DSA_ROWS):
                p = jnp.exp2(logit_ref[j, rc:rc + DSA_ROWS, :] - mx).astype(bf16)
                part = jnp.dot(vt_ref[0, g * VT_ROWS:(g + 1) * VT_ROWS, rc:rc + DSA_ROWS], p,
                               preferred_element_type=f32)
                pv = part if pv is None else pv + part
            outs.append(pv[:HEAD_DIM] * (1.0 / pv[HEAD_DIM:HEAD_DIM + 1]))
        for pair in range(group // 2):
            both = jnp.concatenate(outs[2 * pair:2 * pair + 2], axis=0).T
            col = (g * group // 2 + pair) * LANES
            o_ref[:, col:col + LANES] = both.astype(bf16)


def _dsa_attn_kernel(q_ref, qi_ref, wit_ref, ki_ref, k_ref, vt_ref, o_ref,
                     okey_ref, bias_ref, logit_ref, *, seq, top_k):
    i = pl.program_id(1)
    blocks_per_class = DSA_CLS // DSA_TQ
    for cls in range(seq // DSA_CLS):
        @pl.when(i // blocks_per_class == cls)
        def _(cls=cls):
            _dsa_attn_body(q_ref, qi_ref, wit_ref, ki_ref, k_ref, vt_ref, o_ref,
                           okey_ref, bias_ref, logit_ref, width=DSA_CLS * (cls + 1), top_k=top_k,
                           t0=i * DSA_TQ)


def _dsa_attn(q, qi, wit, ki, k, vt, batch, seq):
    nq = seq // DSA_TQ
    top_k = min(TOPK_MAX, seq // 4)
    qrow = lambda n: pl.BlockSpec((DSA_TQ, n), lambda b, i: (b * nq + i, 0))
    full = lambda n: pl.BlockSpec((1, seq, n), lambda b, i: (b, 0, 0))
    return pl.pallas_call(
        functools.partial(_dsa_attn_kernel, seq=seq, top_k=top_k),
        grid=(batch, nq),
        in_specs=[qrow(ATT_HEADS * LANES), qrow(IDX_HEADS * LANES),
                  pl.BlockSpec((8, DSA_TQ), lambda b, i: (0, b * nq + i)),
                  full(LANES), full(ATT_KV_HEADS * LANES),
                  pl.BlockSpec((1, ATT_KV_HEADS * VT_ROWS, seq), lambda b, i: (b, 0, 0))],
        out_specs=qrow(ATT_HEADS * HEAD_DIM),
        out_shape=jax.ShapeDtypeStruct((batch * seq, ATT_HEADS * HEAD_DIM), bf16),
        scratch_shapes=[pltpu.VMEM((8, DSA_TQ), jnp.int32),
                        pltpu.VMEM((seq, DSA_TQ), f32),
                        pltpu.VMEM((ATT_HEADS // ATT_KV_HEADS, seq, DSA_TQ), f32)],
        compiler_params=_cparams("parallel", "arbitrary"),
        name="dsa_attn",
    )(q, qi, wit, ki, k, vt)


RW_TM = 256
LORA_PAD = 128
GATE_PAD = 256


def _rwkv_pre_kernel(x_ref, xp_ref, g_ref, mu_ref, wr_ref, wk_ref, wv_ref,
                     w1_ref, w2_ref, a1_ref, a2_ref, g1_ref, g2_ref,
                     w0_ref, a0_ref, kk_ref, ka_ref, rk_ref, tri_ref, hred_ref, hexp_ref,
                     at_ref, bt_ref, kt_ref, rt_ref, v_ref, gc_ref, bonus_ref, gate_ref,
                     lc_scr, *, seq):
    i = pl.program_id(0)
    g = g_ref[...]
    h = _rms(x_ref[...], g)
    hp = _rms(xp_ref[...], g)[7:8, :]
    hp = jnp.where((i * RW_TM) % seq == 0, 0.0, hp)
    rowi = lax.broadcasted_iota(jnp.int32, (RW_TM, 1), 0)
    hs = jnp.where(rowi == 0, hp, pltpu.roll(h, 1, 0))
    xx = hs - h
    mu = mu_ref[...]

    def mix(c):
        return (h + xx * mu[c:c + 1, :]).astype(bf16)

    r = jnp.dot(mix(0), wr_ref[...], preferred_element_type=f32)
    k = jnp.dot(mix(1), wk_ref[...], preferred_element_type=f32)
    v = jnp.dot(mix(2), wv_ref[...], preferred_element_type=f32)

    wl = w0_ref[...] + _bdot(jnp.tanh(jnp.dot(mix(3), w1_ref[...], preferred_element_type=f32)),
                             w2_ref[...])
    nwl = -wl
    softplus = jnp.maximum(nwl, 0.0) + jnp.log1p(jnp.exp(-jnp.abs(nwl)))
    ld = -jnp.exp(-softplus - 0.5)
    a = jax.nn.sigmoid(a0_ref[...] + _bdot(jnp.dot(mix(4), a1_ref[...],
                                                   preferred_element_type=f32), a2_ref[...]))
    gate_ref[...] = _bdot(jax.nn.sigmoid(jnp.dot(mix(5), g1_ref[...],
                                                 preferred_element_type=f32)), g2_ref[...]).astype(bf16)

    hred = hred_ref[...]
    hexp = hexp_ref[...]
    z = k * kk_ref[...]
    kk = z * lax.rsqrt(jnp.maximum(_head_sum(z * z, hred, hexp), 1e-24))
    k2 = k * (1.0 + (a - 1.0) * ka_ref[...])
    bonus_ref[...] = (_head_sum(r * k2 * rk_ref[...], hred, hexp) * v).astype(bf16)

    lc = _split_dot_left(tri_ref[...], ld, 2)
    lc_scr[...] = lc
    for c in range(RW_TM // CHUNK):
        last = lc_scr[c * CHUNK + CHUNK - 1:c * CHUNK + CHUNK, :]
        gc_ref[c * 8:(c + 1) * 8, :] = jnp.broadcast_to(jnp.exp(last), (8, D_MODEL))
    einv = jnp.exp(-lc)
    at_ref[...] = -kk * jnp.exp(lc - ld)
    bt_ref[...] = kk * a * einv
    kt_ref[...] = k2 * einv
    rt_ref[...] = r * jnp.exp(lc)
    v_ref[...] = v


def _split_dot_left(a, b, parts):
    out = None
    rem = b
    for p in range(parts):
        piece = rem.astype(bf16)
        term = jnp.dot(a, piece, preferred_element_type=f32)
        out = term if out is None else out + term
        if p + 1 < parts:
            rem = rem - piece.astype(f32)
    return out


def _rwkv_pre(x2, g, mu8, wr, wk, wv, w1, w2, a1, a2, g1, g2, w0, a0, k_k, k_a, r_k, tri, hred, hexp,
              seq):
    m = x2.shape[0]
    row = pl.BlockSpec((RW_TM, D_MODEL), lambda i: (i, 0))
    prev = pl.BlockSpec((8, D_MODEL), lambda i: (jnp.maximum(i * (RW_TM // 8) - 1, 0), 0))
    vec = _const_spec((1, D_MODEL))
    gcrow = pl.BlockSpec((RW_TM // CHUNK * 8, D_MODEL), lambda i: (i, 0))
    act = jax.ShapeDtypeStruct((m, D_MODEL), f32)
    return pl.pallas_call(
        functools.partial(_rwkv_pre_kernel, seq=seq),
        grid=(m // RW_TM,),
        in_specs=[row, prev, vec, _const_spec((8, D_MODEL)),
                  _const_spec(wr.shape), _const_spec(wk.shape), _const_spec(wv.shape),
                  _const_spec(w1.shape), _const_spec(w2.shape),
                  _const_spec(a1.shape), _const_spec(a2.shape),
                  _const_spec(g1.shape), _const_spec(g2.shape),
                  vec, vec, vec, vec, vec,
                  _const_spec(tri.shape), _const_spec(hred.shape), _const_spec(hexp.shape)],
        out_specs=[row, row, row, row, row, gcrow, row, row],
        out_shape=[act, act, act, act, act,
                   jax.ShapeDtypeStruct((m // CHUNK * 8, D_MODEL), f32),
                   jax.ShapeDtypeStruct((m, D_MODEL), bf16), jax.ShapeDtypeStruct((m, D_MODEL), bf16)],
        scratch_shapes=[pltpu.VMEM((RW_TM, D_MODEL), f32)],
        compiler_params=_cparams("parallel"),
        name="rwkv_pre",
    )(x2, x2, g, mu8, wr, wk, wv, w1, w2, a1, a2, g1, g2, w0, a0, k_k, k_a, r_k, tri, hred, hexp)


RA_GROUP = 2 * CHUNK
RA_ROWS = 1024


def _rwkv_chunk_kernel(at_ref, bt_ref, kt_ref, rt_ref, v_ref, gc_ref,
                       q_ref, yl_ref, m_ref, g_ref):
    n = RA_GROUP
    ri = lax.broadcasted_iota(jnp.int32, (n, n), 0)
    ci = lax.broadcasted_iota(jnp.int32, (n, n), 1)
    same = (ri // CHUNK) == (ci // CHUNK)
    strict = same & (ci < ri)
    incl = same & (ci <= ri)
    eye = (ri == ci).astype(f32)
    lane = lax.broadcasted_iota(jnp.int32, (1, LANES), 1)
    low = lane < RWKV_HEAD
    rk = lax.broadcasted_iota(jnp.int32, (RWKV_HEAD, LANES), 0)
    diag = rk == (lax.broadcasted_iota(jnp.int32, (RWKV_HEAD, LANES), 1) % RWKV_HEAD)

    groups = []
    for gi in range(RA_ROWS // n):
        rows = slice(gi * n, (gi + 1) * n)
        at, bt, kt, rt, v = (r[rows, :] for r in (at_ref, bt_ref, kt_ref, rt_ref, v_ref))
        groups.append(dict(at=at, bt=bt, kt=kt, rt=rt, v=v, vb=v.astype(bf16),
                           ar=jnp.concatenate([at, rt], axis=0),
                           bk=jnp.concatenate([bt, kt], axis=0).astype(bf16)))
    probs = [(gr, hl) for gr in groups for hl in (low, jnp.logical_not(low))]

    gs = [lax.dot_general(jnp.where(hl, gr["ar"], 0.0).astype(bf16), gr["bk"], NT,
                          preferred_element_type=f32) for gr, hl in probs]
    aab = [jnp.where(strict, g[:n, :n], 0.0) for g in gs]
    aak = [jnp.where(strict, g[:n, n:], 0.0) for g in gs]
    arb = [jnp.where(incl, g[n:, :n], 0.0) for g in gs]
    ark = [jnp.where(incl, g[n:, n:], 0.0) for g in gs]

    ts = [eye + a for a in aab]
    ps = aab
    for _ in range((CHUNK - 1).bit_length() - 1):
        ps = [_bdot(p, p) for p in ps]
        ts = [t + _bdot(t, p) for t, p in zip(ts, ps)]

    akv = [_bdot(a, gr["vb"]) for a, (gr, _) in zip(aak, probs)]
    tw = [_bdot(t, jnp.concatenate([gr["at"], x], axis=1))
          for t, x, (gr, _) in zip(ts, akv, probs)]
    qy = [_bdot(a, w) for a, w in zip(arb, tw)]
    rkv = [_bdot(a, gr["vb"]) for a, (gr, _) in zip(ark, probs)]

    for gi, gr in enumerate(groups):
        lo, hi = 2 * gi, 2 * gi + 1
        w_hat = jnp.where(low, tw[lo][:, :LANES], tw[hi][:, :LANES])
        u_loc = jnp.where(low, tw[lo][:, LANES:], tw[hi][:, LANES:])
        base = gi * n
        q_ref[base:base + n, :] = gr["rt"] + jnp.where(low, qy[lo][:, :LANES], qy[hi][:, :LANES])
        yl_ref[base:base + n, :] = (jnp.where(low, qy[lo][:, LANES:], qy[hi][:, LANES:])
                                    + jnp.where(low, rkv[lo], rkv[hi]))
        for c in range(n // CHUNK):
            rows = slice(c * CHUNK, (c + 1) * CHUNK)
            ch = base // CHUNK + c
            gc = gc_ref[ch * 8:ch * 8 + 1, :]
            bh = gr["bt"][rows] * gc
            kh = gr["kt"][rows] * gc
            pm = _bdot(w_hat[rows], bh, TN)
            pg = _bdot(jnp.concatenate([u_loc[rows], gr["v"][rows]], axis=0),
                       jnp.concatenate([bh, kh], axis=0), TN)
            out = slice(base + c * CHUNK, base + (c + 1) * CHUNK)
            m_ref[out, :] = (jnp.where(low, pm[:RWKV_HEAD], pm[RWKV_HEAD:])
                             + jnp.where(diag, gc, 0.0))
            g_ref[out, :] = jnp.where(low, pg[:RWKV_HEAD], pg[RWKV_HEAD:])


def _rwkv_chunk(at, bt, kt, rt, v, gc):
    m = at.shape[0]
    blk = pl.BlockSpec((RA_ROWS, LANES), lambda i, p: (i, p))
    gblk = pl.BlockSpec((RA_ROWS // CHUNK * 8, LANES), lambda i, p: (i, p))
    act = jax.ShapeDtypeStruct((m, D_MODEL), f32)
    return pl.pallas_call(
        _rwkv_chunk_kernel,
        grid=(m // RA_ROWS, D_MODEL // LANES),
        in_specs=[blk, blk, blk, blk, blk, gblk],
        out_specs=[blk, blk, blk, blk],
        out_shape=[act, act, act, act],
        compiler_params=_cparams("parallel", "parallel"),
        name="rwkv_chunk",
    )(at, bt, kt, rt, v, gc)


SCAN_LANES = D_MODEL
SCAN_SPLIT = 2


def _rwkv_scan_kernel(q_ref, yl_ref, m_ref, g_ref, y_ref, state_ref, *, seq):
    pairs = SCAN_LANES // LANES
    low = lax.broadcasted_iota(jnp.int32, (1, LANES), 1) < RWKV_HEAD

    def blockdiag(x):
        return jnp.concatenate([jnp.where(low, x, 0.0), jnp.where(low, 0.0, x)], axis=0)

    def body(c, states):
        rows = pl.ds(pl.multiple_of(c * CHUNK, CHUNK), CHUNK)
        new_states = []
        for p in range(pairs):
            cols = slice(p * LANES, (p + 1) * LANES)
            s = states[p]
            mc = blockdiag(m_ref[rows, cols]).astype(bf16)
            s_hi = s.astype(bf16)
            s_lo = (s - s_hi.astype(f32)).astype(bf16)
            sm = (jnp.dot(s_hi, mc, preferred_element_type=f32)
                  + jnp.dot(s_lo, mc, preferred_element_type=f32))
            new_states.append(sm + g_ref[rows, cols])
        for p in range(pairs):
            cols = slice(p * LANES, (p + 1) * LANES)
            y_ref[rows, cols] = (yl_ref[rows, cols]
                                 + _bdot(q_ref[rows, cols], blockdiag(states[p]), NT))
        return tuple(new_states)

    @pl.when(pl.program_id(1) == 0)
    def _():
        state_ref[...] = jnp.zeros(state_ref.shape, f32)

    init = tuple(state_ref[p] for p in range(pairs))
    final = lax.fori_loop(0, seq // SCAN_SPLIT // CHUNK, body, init)
    for p in range(pairs):
        state_ref[p] = final[p]


def _rwkv_scan(q, yl, mm, gg, batch, seq):
    rows = seq // SCAN_SPLIT
    blk = pl.BlockSpec((rows, SCAN_LANES), lambda b, s: (b * SCAN_SPLIT + s, 0))
    return pl.pallas_call(
        functools.partial(_rwkv_scan_kernel, seq=seq),
        grid=(batch, SCAN_SPLIT),
        in_specs=[blk, blk, blk, blk],
        out_specs=blk,
        out_shape=jax.ShapeDtypeStruct((batch * seq, D_MODEL), f32),
        scratch_shapes=[pltpu.VMEM((SCAN_LANES // LANES, RWKV_HEAD, LANES), f32)],
        compiler_params=_cparams("parallel", "arbitrary"),
        name="rwkv_scan",
    )(q, yl, mm, gg)


def _dup_heads(w, n_heads):
    w = w.reshape(w.shape[0], n_heads, 1, HEAD_DIM)
    return jnp.broadcast_to(w, (w.shape[0], n_heads, 2, HEAD_DIM)).reshape(w.shape[0], n_heads * LANES)


def _pad_cols(w, n):
    return jnp.pad(w, ((0, 0), (0, n - w.shape[1])))


def _pad_rows(w, n):
    return jnp.pad(w, ((0, n - w.shape[0]), (0, 0)))


def _rope_tables(seq):
    inv = 1.0 / (ROPE_THETA ** (jnp.arange(0, HEAD_DIM, 2, dtype=f32) / HEAD_DIM))
    ang = jnp.arange(seq, dtype=f32)[:, None] * inv[None, :]
    cos, sin = jnp.cos(ang), jnp.sin(ang)
    return jnp.tile(jnp.concatenate([cos, cos], 1), (1, 2)), jnp.tile(jnp.concatenate([-sin, sin], 1), (1, 2))


def _dsa_layer(x2, g, w_in, w_o, batch, seq):
    o0 = ATT_HEADS * HEAD_DIM
    o1 = o0 + ATT_KV_HEADS * HEAD_DIM
    o2 = o1 + ATT_KV_HEADS * HEAD_DIM
    o3 = o2 + IDX_HEADS * HEAD_DIM
    o4 = o3 + HEAD_DIM
    w_in = w_in.astype(bf16)
    wq = w_in[:, :o0]
    wk2 = _dup_heads(w_in[:, o0:o1], ATT_KV_HEADS)
    wv = w_in[:, o1:o2].T
    wqi = w_in[:, o2:o3]
    wki2 = _dup_heads(w_in[:, o3:o4], 1)
    wwi = _pad_cols(w_in[:, o4:], LANES).T
    cos, sin = _rope_tables(seq)
    q, k, vt, qi, ki, wit = _dsa_proj(x2, g, wq, wk2, wv, wqi, wki2, wwi, cos, sin, batch, seq)
    o = _dsa_attn(q, qi, wit, ki, k, vt, batch, seq)
    return _attn_mlp_kernel, "attn_mlp", [o, x2], [w_o.astype(bf16)]


def _rwkv_layer(x2, g, mu, w_rkv, w0, w1, w2, a0, a1, a2, g1, g2, k_k, k_a, r_k, lnx_w, lnx_b, w_o,
                batch, seq):
    vec = lambda p: p.reshape(1, D_MODEL)
    w_rkv = w_rkv.astype(bf16)
    ri = jnp.arange(RW_TM)
    tri = ((ri[:, None] >= ri[None, :]) & (ri[:, None] // CHUNK == ri[None, :] // CHUNK)).astype(bf16)
    di = jnp.arange(D_MODEL)
    hred = (di[:, None] // RWKV_HEAD == jnp.arange(LANES)[None, :]).astype(bf16)
    hexp = jnp.concatenate([hred.T, hred.T], axis=0)
    at, bt, kt, rt, v, gc, bonus, gate = _rwkv_pre(
        x2, g, _pad_rows(mu, 8), w_rkv[0], w_rkv[1], w_rkv[2],
        _pad_cols(w1, LORA_PAD).astype(bf16), _pad_rows(w2, LORA_PAD).astype(bf16),
        _pad_cols(a1, LORA_PAD).astype(bf16), _pad_rows(a2, LORA_PAD).astype(bf16),
        _pad_cols(g1, GATE_PAD).astype(bf16), _pad_rows(g2, GATE_PAD).astype(bf16),
        vec(w0), vec(a0), vec(k_k), vec(k_a), vec(r_k), tri, hred, hexp, seq)
    q, yl, mm, gg = _rwkv_chunk(at, bt, kt, rt, v, gc)
    y = _rwkv_scan(q, yl, mm, gg, batch, seq)
    return (_rwkv_mlp_kernel, "rwkv_mlp", [y, bonus, gate, x2],
            [vec(lnx_w), vec(lnx_b), hred, hexp, w_o.astype(bf16)])


def kernel(x, mixer_norm, mlp_norm, mlp_w_up, mlp_w_down, final_norm, dsa_w_in, dsa_w_o, rwkv_mu, rwkv_w_rkv, rwkv_w0, rwkv_w1, rwkv_w2, rwkv_a0, rwkv_a1, rwkv_a2, rwkv_g1, rwkv_g2, rwkv_k_k, rwkv_k_a, rwkv_r_k, rwkv_lnx_w, rwkv_lnx_b, rwkv_w_o):
    batch, seq, _ = x.shape
    depth = mixer_norm.shape[0]
    x2 = x.reshape(batch * seq, D_MODEL)
    fg = final_norm.reshape(1, D_MODEL)
    w_up = mlp_w_up.astype(bf16)
    w_down = mlp_w_down.astype(bf16)
    for i in range(depth):
        g = mixer_norm[i].reshape(1, D_MODEL)
        j = i // 2
        if i % 2 == 0:
            tail = _dsa_layer(x2, g, dsa_w_in[j], dsa_w_o[j], batch, seq)
        else:
            tail = _rwkv_layer(x2, g, rwkv_mu[j], rwkv_w_rkv[j], rwkv_w0[j], rwkv_w1[j], rwkv_w2[j],
                               rwkv_a0[j], rwkv_a1[j], rwkv_a2[j], rwkv_g1[j], rwkv_g2[j],
                               rwkv_k_k[j], rwkv_k_a[j], rwkv_r_k[j], rwkv_lnx_w[j], rwkv_lnx_b[j],
                               rwkv_w_o[j], batch, seq)
        x2 = _mixer_mlp(*tail, mlp_norm[i].reshape(1, D_MODEL), w_up, w_down, i, fg,
                        final=(i == depth - 1))
    return x2.reshape(batch, seq, D_MODEL)
```

```python
import functools

import jax
import jax.numpy as jnp
from jax import lax
from jax.experimental import pallas as pl
from jax.experimental.pallas import tpu as pltpu

f32 = jnp.float32
bf16 = jnp.bfloat16

D_MODEL = 1024
D_FF = 4 * D_MODEL
NORM_EPS = 1e-6

ATT_HEADS = 16
ATT_KV_HEADS = 4
HEAD_DIM = 64
IDX_HEADS = 8
TOPK_MAX = 256
ROPE_THETA = 10000.0

RWKV_HEAD = 64
RWKV_HEADS = 16
LNX_EPS = 64e-5
CHUNK = 64

LANES = 128
VMEM_LIMIT = 56 * 1024 * 1024

NT = (((1,), (1,)), ((), ()))
TN = (((0,), (0,)), ((), ()))


def _cparams(*sem):
    return pltpu.CompilerParams(dimension_semantics=sem, vmem_limit_bytes=VMEM_LIMIT)


def _rms(x, g):
    return x * lax.rsqrt(jnp.mean(x * x, axis=-1, keepdims=True) + NORM_EPS) * g


def _bdot(a, b, dims=None):
    a = a.astype(bf16)
    b = b.astype(bf16)
    if dims is None:
        return jnp.dot(a, b, preferred_element_type=f32)
    return lax.dot_general(a, b, dims, preferred_element_type=f32)


def _split_dot(a, b, parts):
    out = None
    rem = a
    for p in range(parts):
        piece = rem.astype(bf16)
        term = jnp.dot(piece, b, preferred_element_type=f32)
        out = term if out is None else out + term
        if p + 1 < parts:
            rem = rem - piece.astype(f32)
    return out


def _head_sum(x, hred, hexp):
    red = jnp.dot(x.astype(bf16), hred, preferred_element_type=f32)
    hi = red.astype(bf16)
    lo = (red - hi.astype(f32)).astype(bf16)
    return jnp.dot(jnp.concatenate([hi, lo], axis=1), hexp, preferred_element_type=f32)


def _const_spec(shape):
    return pl.BlockSpec(shape, lambda *_: (0,) * len(shape))


MLP_TM = 512
MLP_FCH = 512


def _mlp_tail(x, g_ref, wu_ref, wd_ref, fg_ref, o_ref, final):
    xn = _rms(x, g_ref[...]).astype(bf16)
    o_ref[...] = x
    for f in range(0, D_FF, MLP_FCH):
        u = jnp.dot(xn, wu_ref[:, f:f + MLP_FCH], preferred_element_type=f32)
        u = jnp.maximum(u, 0.0)
        o_ref[...] += jnp.dot((u * u).astype(bf16), wd_ref[f:f + MLP_FCH, :],
                              preferred_element_type=f32)
    if final:
        o_ref[...] = _rms(o_ref[...], fg_ref[...])


def _attn_mlp_kernel(a_ref, x_ref, wo_ref, g_ref, wu_ref, wd_ref, fg_ref, o_ref, *, final):
    x = x_ref[...] + jnp.dot(a_ref[...], wo_ref[...], preferred_element_type=f32)
    _mlp_tail(x, g_ref, wu_ref, wd_ref, fg_ref, o_ref, final)


def _rwkv_mlp_kernel(y_ref, bonus_ref, gate_ref, x_ref, lw_ref, lb_ref, hred_ref, hexp_ref, wo_ref,
                     g_ref, wu_ref, wd_ref, fg_ref, o_ref, *, final):
    y = y_ref[...]
    hred = hred_ref[...]
    hexp = hexp_ref[...]
    mean = _head_sum(y, hred, hexp) * (1.0 / RWKV_HEAD)
    yc = y - mean
    var = _head_sum(yc * yc, hred, hexp) * (1.0 / RWKV_HEAD)
    yn = yc * lax.rsqrt(var + LNX_EPS) * lw_ref[...] + lb_ref[...]
    out = (yn + bonus_ref[...]) * gate_ref[...]
    x = x_ref[...] + jnp.dot(out.astype(bf16), wo_ref[...], preferred_element_type=f32)
    _mlp_tail(x, g_ref, wu_ref, wd_ref, fg_ref, o_ref, final)


def _resident(shape):
    return pl.BlockSpec(shape, lambda *_: (0,) * len(shape), pipeline_mode=pl.Buffered(1))


def _resident_layer(shape, layer):
    return pl.BlockSpec((None,) + tuple(shape[1:]), lambda *_: (layer,) + (0,) * (len(shape) - 1),
                        pipeline_mode=pl.Buffered(1))


def _mixer_mlp(kernel, name, rows, consts, g, wu_all, wd_all, layer, fg, final):
    m = rows[0].shape[0]
    row = pl.BlockSpec((MLP_TM, D_MODEL), lambda i: (i, 0))
    operands = list(rows) + list(consts) + [g, wu_all, wd_all, fg]
    return pl.pallas_call(
        functools.partial(kernel, final=final),
        grid=(m // MLP_TM,),
        in_specs=([row] * len(rows) + [_resident(c.shape) for c in list(consts) + [g]]
                  + [_resident_layer(wu_all.shape, layer), _resident_layer(wd_all.shape, layer),
                     _resident(fg.shape)]),
        out_specs=row,
        out_shape=jax.ShapeDtypeStruct((m, D_MODEL), f32),
        compiler_params=_cparams("parallel"),
        name=name,
    )(*operands)


DSA_TM = 256


def _rope(x, cos, sin_signed, first_half):
    fwd = pltpu.roll(x, 32, 1)
    bwd = pltpu.roll(x, 96, 1)
    return x * cos + jnp.where(first_half, bwd, fwd) * sin_signed


LOG2E = 1.4426950408889634


VT_ROWS = 80


def _dsa_proj_kernel(x_ref, g_ref, wq_ref, wk_ref, wv_ref, wqi_ref, wki_ref, wwi_ref,
                     cos_ref, sin_ref,
                     q_ref, k_ref, vt_ref, qi_ref, ki_ref, wit_ref):
    h = _rms(x_ref[...], g_ref[...]).astype(bf16)
    tm = h.shape[0]
    cos = cos_ref[...]
    sin = sin_ref[...]
    lane = lax.broadcasted_iota(jnp.int32, (1, LANES), 1)
    first_half = (lane % HEAD_DIM) < (HEAD_DIM // 2)
    low = lane < HEAD_DIM

    def per_head(x, out_ref, pair, scale):
        xc = _rope(x[:, pair * LANES:(pair + 1) * LANES], cos, sin, first_half) * scale
        swapped = pltpu.roll(xc, HEAD_DIM, 1)
        out_ref[:, (2 * pair) * LANES:(2 * pair + 1) * LANES] = jnp.where(low, xc, 0.0).astype(bf16)
        out_ref[:, (2 * pair + 1) * LANES:(2 * pair + 2) * LANES] = jnp.where(low, swapped, 0.0).astype(bf16)

    q = jnp.dot(h, wq_ref[...], preferred_element_type=f32)
    for pair in range(ATT_HEADS // 2):
        per_head(q, q_ref, pair, HEAD_DIM ** -0.5 * LOG2E)
    qi = jnp.dot(h, wqi_ref[...], preferred_element_type=f32)
    for pair in range(IDX_HEADS // 2):
        per_head(qi, qi_ref, pair, 1.0)

    k2 = jnp.dot(h, wk_ref[...], preferred_element_type=f32)
    for c in range(0, ATT_KV_HEADS * LANES, LANES):
        kc = _rope(k2[:, c:c + LANES], cos, sin, first_half)
        k_ref[0, :, c:c + LANES] = jnp.where(low, kc, 0.0).astype(bf16)
    ki2 = _rope(jnp.dot(h, wki_ref[...], preferred_element_type=f32), cos, sin, first_half)
    ki_ref[0] = jnp.where(low, ki2, 0.0).astype(bf16)

    vt = lax.dot_general(wv_ref[...], h, NT, preferred_element_type=f32)
    extra = (lax.broadcasted_iota(jnp.int32, (VT_ROWS - HEAD_DIM, tm), 0) == 0).astype(f32)
    for g in range(ATT_KV_HEADS):
        vt_ref[0, g * VT_ROWS:(g + 1) * VT_ROWS, :] = jnp.concatenate(
            [vt[g * HEAD_DIM:(g + 1) * HEAD_DIM], extra], axis=0).astype(bf16)
    wit_ref[...] = (lax.dot_general(wwi_ref[...], h, NT, preferred_element_type=f32)
                    * (IDX_HEADS ** -0.5 * HEAD_DIM ** -0.5))


def _dsa_proj(x2, g, wq, wk2, wv, wqi, wki2, wwi, cos, sin, batch, seq):
    m = x2.shape[0]
    nblk = seq // DSA_TM
    row = lambda n: pl.BlockSpec((DSA_TM, n), lambda b, i: (b * nblk + i, 0))
    seqrow = lambda n: pl.BlockSpec((1, DSA_TM, n), lambda b, i: (b, i, 0))
    tab = pl.BlockSpec((DSA_TM, LANES), lambda b, i: (i, 0))
    kvw = ATT_KV_HEADS * LANES
    vtr = ATT_KV_HEADS * VT_ROWS
    return pl.pallas_call(
        _dsa_proj_kernel,
        grid=(batch, nblk),
        in_specs=[row(D_MODEL), _const_spec((1, D_MODEL)),
                  _const_spec(wq.shape), _const_spec(wk2.shape), _const_spec(wv.shape),
                  _const_spec(wqi.shape), _const_spec(wki2.shape), _const_spec(wwi.shape),
                  tab, tab],
        out_specs=[row(ATT_HEADS * LANES), seqrow(kvw),
                   pl.BlockSpec((1, vtr, DSA_TM), lambda b, i: (b, 0, i)),
                   row(IDX_HEADS * LANES), seqrow(LANES),
                   pl.BlockSpec((LANES, DSA_TM), lambda b, i: (0, b * nblk + i))],
        out_shape=[jax.ShapeDtypeStruct((m, ATT_HEADS * LANES), bf16),
                   jax.ShapeDtypeStruct((batch, seq, kvw), bf16),
                   jax.ShapeDtypeStruct((batch, vtr, seq), bf16),
                   jax.ShapeDtypeStruct((m, IDX_HEADS * LANES), bf16),
                   jax.ShapeDtypeStruct((batch, seq, LANES), bf16),
                   jax.ShapeDtypeStruct((LANES, m), f32)],
        compiler_params=_cparams("parallel", "parallel"),
        name="dsa_proj",
    )(x2, g, wq, wk2, wv, wqi, wki2, wwi, cos, sin)


DSA_TQ = 128
DSA_CLS = 512
IDX_GROUP = 4
DSA_ROWS = 256
INT_MIN = -2 ** 31
NEG_INF_KEY = -2139095041


FOLD_ROWS = 64


def _fold_rows(x, op):
    rows = x.shape[0]
    if rows > FOLD_ROWS and rows % FOLD_ROWS == 0:
        acc = x[:FOLD_ROWS]
        for r in range(FOLD_ROWS, rows, FOLD_ROWS):
            acc = op(acc, x[r:r + FOLD_ROWS])
        x = acc
    while x.shape[0] % 16 == 0:
        h = x.shape[0] // 2
        x = op(x[:h], x[h:])
    return x


def _col_count(mask):
    return jnp.sum(_fold_rows(jnp.where(mask, 1.0, 0.0), jnp.add), axis=0, keepdims=True).astype(jnp.int32)


def _stack_heads(ref, first, count):
    return jnp.concatenate([ref[:, (first + j) * LANES:(first + j + 1) * LANES] for j in range(count)],
                           axis=0)


def _write_topk_bias(qi_ref, wit_ref, ki_ref, okey_ref, bias_ref, causal, row_s, *, width, top_k):
    wit = wit_ref[...]
    ki = ki_ref[0, 0:width, :]
    score = jnp.zeros((width, DSA_TQ), f32)
    for first in range(0, IDX_HEADS, IDX_GROUP):
        rel = lax.dot_general(ki, _stack_heads(qi_ref, first, IDX_GROUP), NT,
                              preferred_element_type=f32)
        for j in range(IDX_GROUP):
            hd = first + j
            score = score + wit[hd:hd + 1, :] * jnp.maximum(rel[:, j * DSA_TQ:(j + 1) * DSA_TQ], 0.0)
    score = jnp.where(causal, score, -jnp.inf)

    def key_to_float(k):
        k = jnp.maximum(k, NEG_INF_KEY)
        return lax.bitcast_convert_type(k ^ ((k >> 31) & jnp.int32(0x7FFFFFFF)), f32)

    n_nonneg = _col_count(score >= 0.0)
    start_high = n_nonneg >= top_k
    thr = jnp.where(start_high, jnp.int32(0), jnp.int32(INT_MIN))
    n_ge = jnp.where(start_high, n_nonneg, jnp.int32(width))

    def value_step(i, state):
        thr, n_ge = state
        cand = thr | (jnp.int32(1) << (30 - i))
        cnt = _col_count(score >= key_to_float(cand))
        take = cnt >= top_k
        return jnp.where(take, cand, thr), jnp.where(take, cnt, n_ge)

    thr, n_ge = lax.fori_loop(0, 31, value_step, (thr, n_ge))

    thr_f = key_to_float(thr)
    bias_ref[0:width, :] = jnp.where((score >= thr_f) & causal, 0.0, -jnp.inf)
    overfull = (n_ge != top_k) & (thr > NEG_INF_KEY)

    @pl.when(jnp.max(overfull.astype(f32)) > 0.0)
    def _():
        real_thr = thr > NEG_INF_KEY
        next_f = key_to_float(thr + 1)
        beyond = score >= next_f
        in_bin = (score >= thr_f) & jnp.logical_not(beyond) & real_thr
        offset = jnp.where(in_bin, score - jnp.where(real_thr, thr_f, 0.0), -1.0)
        need = top_k - _col_count(beyond)

        okey_ref[...] = jnp.zeros(okey_ref.shape, jnp.int32)

        @pl.when(jnp.max(offset) > 0.0)
        def _():
            def offset_step(i, okey):
                cand = okey | (jnp.int32(1) << (30 - i))
                cnt = _col_count(offset >= lax.bitcast_convert_type(cand, f32))
                return jnp.where(cnt >= need, cand, okey)

            okey = lax.fori_loop(0, 31, offset_step, jnp.zeros((1, DSA_TQ), jnp.int32))
            okey_ref[...] = jnp.broadcast_to(okey, okey_ref.shape)

        othr = lax.bitcast_convert_type(okey_ref[0:1, :], f32)
        above = offset > othr
        tied = offset == othr
        need = need - _col_count(above)
        nbits = (width - 1).bit_length()

        def index_step(i, pos):
            cand = pos | (jnp.int32(1) << (nbits - 1 - i))
            return jnp.where(_col_count(tied & (row_s < cand)) < need, cand, pos)

        pos = lax.fori_loop(0, nbits, index_step, jnp.zeros((1, DSA_TQ), jnp.int32))
        keep = (beyond | above | (tied & (row_s <= pos))) & causal
        bias_ref[0:width, :] = jnp.where(keep, 0.0, -jnp.inf)


def _dsa_attn_body(q_ref, qi_ref, wit_ref, ki_ref, k_ref, vt_ref, o_ref,
                   okey_ref, bias_ref, logit_ref, *, width, top_k, t0, select):
    col_t = t0 + lax.broadcasted_iota(jnp.int32, (1, DSA_TQ), 1)
    row_s = lax.broadcasted_iota(jnp.int32, (width, 1), 0)
    causal = row_s <= col_t
    if select:
        _write_topk_bias(qi_ref, wit_ref, ki_ref, okey_ref, bias_ref, causal, row_s,
                         width=width, top_k=top_k)
    else:
        bias_ref[0:width, :] = jnp.where(causal, 0.0, -jnp.inf)

    group = ATT_HEADS // ATT_KV_HEADS
    rows = min(DSA_ROWS, width)
    for g in range(ATT_KV_HEADS):
        rq = _stack_heads(q_ref, g * group, group)
        maxes = [None] * group
        for rc in range(0, width, rows):
            logits4 = lax.dot_general(k_ref[0, rc:rc + rows, g * LANES:(g + 1) * LANES], rq, NT,
                                      preferred_element_type=f32)
            b = bias_ref[rc:rc + rows, :]
            for j in range(group):
                logits = logits4[:, j * DSA_TQ:(j + 1) * DSA_TQ] + b
                logit_ref[j, rc:rc + rows, :] = logits
                m = _fold_rows(logits, jnp.maximum)
                maxes[j] = m if maxes[j] is None else jnp.maximum(maxes[j], m)
        outs = []
        for j in range(group):
            mx = jnp.max(maxes[j], axis=0, keepdims=True)
            pv = None
            for rc in range(0, width, rows):
                p = jnp.exp2(logit_ref[j, rc:rc + rows, :] - mx).astype(bf16)
                part = jnp.dot(vt_ref[0, g * VT_ROWS:(g + 1) * VT_ROWS, rc:rc + rows], p,
                               preferred_element_type=f32)
                pv = part if pv is None else pv + part
            outs.append(pv[:HEAD_DIM] * (1.0 / pv[HEAD_DIM:HEAD_DIM + 1]))
        for pair in range(group // 2):
            both = jnp.concatenate(outs[2 * pair:2 * pair + 2], axis=0).T
            col = (g * group // 2 + pair) * LANES
            o_ref[:, col:col + LANES] = both.astype(bf16)


def _dsa_attn_kernel(q_ref, qi_ref, wit_ref, ki_ref, k_ref, vt_ref, o_ref,
                     okey_ref, bias_ref, logit_ref, *, seq, top_k):
    i = pl.program_id(1)
    refs = (q_ref, qi_ref, wit_ref, ki_ref, k_ref, vt_ref, o_ref, okey_ref, bias_ref, logit_ref)
    dense_blocks = top_k // DSA_TQ

    @pl.when(i < dense_blocks)
    def _():
        _dsa_attn_body(*refs, width=dense_blocks * DSA_TQ, top_k=top_k, t0=i * DSA_TQ, select=False)

    blocks_per_class = DSA_CLS // DSA_TQ
    for cls in range(seq // DSA_CLS):
        @pl.when((i >= dense_blocks) & (i // blocks_per_class == cls))
        def _(cls=cls):
            _dsa_attn_body(*refs, width=DSA_CLS * (cls + 1), top_k=top_k, t0=i * DSA_TQ, select=True)


def _dsa_attn(q, qi, wit, ki, k, vt, batch, seq):
    nq = seq // DSA_TQ
    top_k = min(TOPK_MAX, seq // 4)
    qrow = lambda n: pl.BlockSpec((DSA_TQ, n), lambda b, i: (b * nq + i, 0))
    full = lambda n: pl.BlockSpec((1, seq, n), lambda b, i: (b, 0, 0))
    return pl.pallas_call(
        functools.partial(_dsa_attn_kernel, seq=seq, top_k=top_k),
        grid=(batch, nq),
        in_specs=[qrow(ATT_HEADS * LANES), qrow(IDX_HEADS * LANES),
                  pl.BlockSpec((8, DSA_TQ), lambda b, i: (0, b * nq + i)),
                  full(LANES), full(ATT_KV_HEADS * LANES),
                  pl.BlockSpec((1, ATT_KV_HEADS * VT_ROWS, seq), lambda b, i: (b, 0, 0))],
        out_specs=qrow(ATT_HEADS * HEAD_DIM),
        out_shape=jax.ShapeDtypeStruct((batch * seq, ATT_HEADS * HEAD_DIM), bf16),
        scratch_shapes=[pltpu.VMEM((8, DSA_TQ), jnp.int32),
                        pltpu.VMEM((seq, DSA_TQ), f32),
                        pltpu.VMEM((ATT_HEADS // ATT_KV_HEADS, seq, DSA_TQ), f32)],
        compiler_params=_cparams("parallel", "arbitrary"),
        name="dsa_attn",
    )(q, qi, wit, ki, k, vt)


RW_TM = 256
LORA_PAD = 128
GATE_PAD = 256


def _rwkv_pre_kernel(x_ref, xp_ref, g_ref, mu_ref, wr_ref, wk_ref, wv_ref,
                     w1_ref, w2_ref, a1_ref, a2_ref, g1_ref, g2_ref,
                     w0_ref, a0_ref, kk_ref, ka_ref, rk_ref, tri_ref, hred_ref, hexp_ref,
                     at_ref, bt_ref, kt_ref, rt_ref, v_ref, gc_ref, bonus_ref, gate_ref,
                     lc_scr, *, seq):
    i = pl.program_id(0)
    g = g_ref[...]
    h = _rms(x_ref[...], g)
    hp = _rms(xp_ref[...], g)[7:8, :]
    hp = jnp.where((i * RW_TM) % seq == 0, 0.0, hp)
    rowi = lax.broadcasted_iota(jnp.int32, (RW_TM, 1), 0)
    hs = jnp.where(rowi == 0, hp, pltpu.roll(h, 1, 0))
    xx = hs - h
    mu = mu_ref[...]

    def mix(c):
        return (h + xx * mu[c:c + 1, :]).astype(bf16)

    r = jnp.dot(mix(0), wr_ref[...], preferred_element_type=f32)
    k = jnp.dot(mix(1), wk_ref[...], preferred_element_type=f32)
    v = jnp.dot(mix(2), wv_ref[...], preferred_element_type=f32)

    wl = w0_ref[...] + _bdot(jnp.tanh(jnp.dot(mix(3), w1_ref[...], preferred_element_type=f32)),
                             w2_ref[...])
    nwl = -wl
    softplus = jnp.maximum(nwl, 0.0) + jnp.log1p(jnp.exp(-jnp.abs(nwl)))
    ld = -jnp.exp(-softplus - 0.5)
    a = jax.nn.sigmoid(a0_ref[...] + _bdot(jnp.dot(mix(4), a1_ref[...],
                                                   preferred_element_type=f32), a2_ref[...]))
    gate_ref[...] = _bdot(jax.nn.sigmoid(jnp.dot(mix(5), g1_ref[...],
                                                 preferred_element_type=f32)), g2_ref[...]).astype(bf16)

    hred = hred_ref[...]
    hexp = hexp_ref[...]
    z = k * kk_ref[...]
    kk = z * lax.rsqrt(jnp.maximum(_head_sum(z * z, hred, hexp), 1e-24))
    k2 = k * (1.0 + (a - 1.0) * ka_ref[...])
    bonus_ref[...] = (_head_sum(r * k2 * rk_ref[...], hred, hexp) * v).astype(bf16)

    lc = _split_dot_left(tri_ref[...], ld, 2)
    lc_scr[...] = lc
    for c in range(RW_TM // CHUNK):
        last = lc_scr[c * CHUNK + CHUNK - 1:c * CHUNK + CHUNK, :]
        gc_ref[c * 8:(c + 1) * 8, :] = jnp.broadcast_to(jnp.exp(last), (8, D_MODEL))
    einv = jnp.exp(-lc)
    at_ref[...] = -kk * jnp.exp(lc - ld)
    bt_ref[...] = kk * a * einv
    kt_ref[...] = k2 * einv
    rt_ref[...] = r * jnp.exp(lc)
    v_ref[...] = v


def _split_dot_left(a, b, parts):
    out = None
    rem = b
    for p in range(parts):
        piece = rem.astype(bf16)
        term = jnp.dot(a, piece, preferred_element_type=f32)
        out = term if out is None else out + term
        if p + 1 < parts:
            rem = rem - piece.astype(f32)
    return out


def _rwkv_pre(x2, g, mu8, wr, wk, wv, w1, w2, a1, a2, g1, g2, w0, a0, k_k, k_a, r_k, tri, hred, hexp,
              seq):
    m = x2.shape[0]
    row = pl.BlockSpec((RW_TM, D_MODEL), lambda i: (i, 0))
    prev = pl.BlockSpec((8, D_MODEL), lambda i: (jnp.maximum(i * (RW_TM // 8) - 1, 0), 0))
    vec = _const_spec((1, D_MODEL))
    gcrow = pl.BlockSpec((RW_TM // CHUNK * 8, D_MODEL), lambda i: (i, 0))
    act = jax.ShapeDtypeStruct((m, D_MODEL), f32)
    return pl.pallas_call(
        functools.partial(_rwkv_pre_kernel, seq=seq),
        grid=(m // RW_TM,),
        in_specs=[row, prev, vec, _const_spec((8, D_MODEL)),
                  _const_spec(wr.shape), _const_spec(wk.shape), _const_spec(wv.shape),
                  _const_spec(w1.shape), _const_spec(w2.shape),
                  _const_spec(a1.shape), _const_spec(a2.shape),
                  _const_spec(g1.shape), _const_spec(g2.shape),
                  vec, vec, vec, vec, vec,
                  _const_spec(tri.shape), _const_spec(hred.shape), _const_spec(hexp.shape)],
        out_specs=[row, row, row, row, row, gcrow, row, row],
        out_shape=[act, act, act, act, act,
                   jax.ShapeDtypeStruct((m // CHUNK * 8, D_MODEL), f32),
                   jax.ShapeDtypeStruct((m, D_MODEL), bf16), jax.ShapeDtypeStruct((m, D_MODEL), bf16)],
        scratch_shapes=[pltpu.VMEM((RW_TM, D_MODEL), f32)],
        compiler_params=_cparams("parallel"),
        name="rwkv_pre",
    )(x2, x2, g, mu8, wr, wk, wv, w1, w2, a1, a2, g1, g2, w0, a0, k_k, k_a, r_k, tri, hred, hexp)


RA_GROUP = 2 * CHUNK
RA_ROWS = 1024


def _rwkv_chunk_kernel(at_ref, bt_ref, kt_ref, rt_ref, v_ref, gc_ref,
                       q_ref, yl_ref, m_ref, g_ref):
    n = RA_GROUP
    ri = lax.broadcasted_iota(jnp.int32, (n, n), 0)
    ci = lax.broadcasted_iota(jnp.int32, (n, n), 1)
    same = (ri // CHUNK) == (ci // CHUNK)
    strict = same & (ci < ri)
    incl = same & (ci <= ri)
    eye = (ri == ci).astype(f32)
    lane = lax.broadcasted_iota(jnp.int32, (1, LANES), 1)
    low = lane < RWKV_HEAD
    rk = lax.broadcasted_iota(jnp.int32, (RWKV_HEAD, LANES), 0)
    diag = rk == (lax.broadcasted_iota(jnp.int32, (RWKV_HEAD, LANES), 1) % RWKV_HEAD)

    groups = []
    for gi in range(RA_ROWS // n):
        rows = slice(gi * n, (gi + 1) * n)
        at, bt, kt, rt, v = (r[rows, :] for r in (at_ref, bt_ref, kt_ref, rt_ref, v_ref))
        groups.append(dict(at=at, bt=bt, kt=kt, rt=rt, v=v, vb=v.astype(bf16),
                           ar=jnp.concatenate([at, rt], axis=0),
                           bk=jnp.concatenate([bt, kt], axis=0).astype(bf16)))
    probs = [(gr, hl) for gr in groups for hl in (low, jnp.logical_not(low))]

    gs = [lax.dot_general(jnp.where(hl, gr["ar"], 0.0).astype(bf16), gr["bk"], NT,
                          preferred_element_type=f32) for gr, hl in probs]
    aab = [jnp.where(strict, g[:n, :n], 0.0) for g in gs]
    aak = [jnp.where(strict, g[:n, n:], 0.0) for g in gs]
    arb = [jnp.where(incl, g[n:, :n], 0.0) for g in gs]
    ark = [jnp.where(incl, g[n:, n:], 0.0) for g in gs]

    ts = [eye + a for a in aab]
    ps = aab
    for _ in range((CHUNK - 1).bit_length() - 1):
        ps = [_bdot(p, p) for p in ps]
        ts = [t + _bdot(t, p) for t, p in zip(ts, ps)]

    akv = [_bdot(a, gr["vb"]) for a, (gr, _) in zip(aak, probs)]
    tw = [_bdot(t, jnp.concatenate([gr["at"], x], axis=1))
          for t, x, (gr, _) in zip(ts, akv, probs)]
    qy = [_bdot(a, w) for a, w in zip(arb, tw)]
    rkv = [_bdot(a, gr["vb"]) for a, (gr, _) in zip(ark, probs)]

    for gi, gr in enumerate(groups):
        lo, hi = 2 * gi, 2 * gi + 1
        w_hat = jnp.where(low, tw[lo][:, :LANES], tw[hi][:, :LANES])
        u_loc = jnp.where(low, tw[lo][:, LANES:], tw[hi][:, LANES:])
        base = gi * n
        q_ref[base:base + n, :] = gr["rt"] + jnp.where(low, qy[lo][:, :LANES], qy[hi][:, :LANES])
        yl_ref[base:base + n, :] = (jnp.where(low, qy[lo][:, LANES:], qy[hi][:, LANES:])
                                    + jnp.where(low, rkv[lo], rkv[hi]))
        for c in range(n // CHUNK):
            rows = slice(c * CHUNK, (c + 1) * CHUNK)
            ch = base // CHUNK + c
            gc = gc_ref[ch * 8:ch * 8 + 1, :]
            bh = gr["bt"][rows] * gc
            kh = gr["kt"][rows] * gc
            pm = _bdot(w_hat[rows], bh, TN)
            pg = _bdot(jnp.concatenate([u_loc[rows], gr["v"][rows]], axis=0),
                       jnp.concatenate([bh, kh], axis=0), TN)
            out = slice(base + c * CHUNK, base + (c + 1) * CHUNK)
            m_ref[out, :] = (jnp.where(low, pm[:RWKV_HEAD], pm[RWKV_HEAD:])
                             + jnp.where(diag, gc, 0.0))
            g_ref[out, :] = jnp.where(low, pg[:RWKV_HEAD], pg[RWKV_HEAD:])


def _rwkv_chunk(at, bt, kt, rt, v, gc):
    m = at.shape[0]
    blk = pl.BlockSpec((RA_ROWS, LANES), lambda i, p: (i, p))
    gblk = pl.BlockSpec((RA_ROWS // CHUNK * 8, LANES), lambda i, p: (i, p))
    act = jax.ShapeDtypeStruct((m, D_MODEL), f32)
    return pl.pallas_call(
        _rwkv_chunk_kernel,
        grid=(m // RA_ROWS, D_MODEL // LANES),
        in_specs=[blk, blk, blk, blk, blk, gblk],
        out_specs=[blk, blk, blk, blk],
        out_shape=[act, act, act, act],
        compiler_params=_cparams("parallel", "parallel"),
        name="rwkv_chunk",
    )(at, bt, kt, rt, v, gc)


SCAN_LANES = D_MODEL
SCAN_SPLIT = 2


def _rwkv_scan_kernel(q_ref, yl_ref, m_ref, g_ref, y_ref, state_ref, *, seq):
    pairs = SCAN_LANES // LANES
    low = lax.broadcasted_iota(jnp.int32, (1, LANES), 1) < RWKV_HEAD

    def blockdiag(x):
        return jnp.concatenate([jnp.where(low, x, 0.0), jnp.where(low, 0.0, x)], axis=0)

    def body(c, states):
        rows = pl.ds(pl.multiple_of(c * CHUNK, CHUNK), CHUNK)
        new_states = []
        for p in range(pairs):
            cols = slice(p * LANES, (p + 1) * LANES)
            s = states[p]
            mc = blockdiag(m_ref[rows, cols]).astype(bf16)
            s_hi = s.astype(bf16)
            s_lo = (s - s_hi.astype(f32)).astype(bf16)
            sm = (jnp.dot(s_hi, mc, preferred_element_type=f32)
                  + jnp.dot(s_lo, mc, preferred_element_type=f32))
            new_states.append(sm + g_ref[rows, cols])
        for p in range(pairs):
            cols = slice(p * LANES, (p + 1) * LANES)
            y_ref[rows, cols] = (yl_ref[rows, cols]
                                 + _bdot(q_ref[rows, cols], blockdiag(states[p]), NT))
        return tuple(new_states)

    @pl.when(pl.program_id(1) == 0)
    def _():
        state_ref[...] = jnp.zeros(state_ref.shape, f32)

    init = tuple(state_ref[p] for p in range(pairs))
    final = lax.fori_loop(0, seq // SCAN_SPLIT // CHUNK, body, init)
    for p in range(pairs):
        state_ref[p] = final[p]


def _rwkv_scan(q, yl, mm, gg, batch, seq):
    rows = seq // SCAN_SPLIT
    blk = pl.BlockSpec((rows, SCAN_LANES), lambda b, s: (b * SCAN_SPLIT + s, 0))
    return pl.pallas_call(
        functools.partial(_rwkv_scan_kernel, seq=seq),
        grid=(batch, SCAN_SPLIT),
        in_specs=[blk, blk, blk, blk],
        out_specs=blk,
        out_shape=jax.ShapeDtypeStruct((batch * seq, D_MODEL), f32),
        scratch_shapes=[pltpu.VMEM((SCAN_LANES // LANES, RWKV_HEAD, LANES), f32)],
        compiler_params=_cparams("parallel", "arbitrary"),
        name="rwkv_scan",
    )(q, yl, mm, gg)


def _dup_heads(w, n_heads):
    w = w.reshape(w.shape[0], n_heads, 1, HEAD_DIM)
    return jnp.broadcast_to(w, (w.shape[0], n_heads, 2, HEAD_DIM)).reshape(w.shape[0], n_heads * LANES)


def _pad_cols(w, n):
    return jnp.pad(w, ((0, 0), (0, n - w.shape[1])))


def _pad_rows(w, n):
    return jnp.pad(w, ((0, n - w.shape[0]), (0, 0)))


def _rope_tables(seq):
    inv = 1.0 / (ROPE_THETA ** (jnp.arange(0, HEAD_DIM, 2, dtype=f32) / HEAD_DIM))
    ang = jnp.arange(seq, dtype=f32)[:, None] * inv[None, :]
    cos, sin = jnp.cos(ang), jnp.sin(ang)
    return jnp.tile(jnp.concatenate([cos, cos], 1), (1, 2)), jnp.tile(jnp.concatenate([-sin, sin], 1), (1, 2))


def _dsa_layer(x2, g, w_in, w_o, batch, seq):
    o0 = ATT_HEADS * HEAD_DIM
    o1 = o0 + ATT_KV_HEADS * HEAD_DIM
    o2 = o1 + ATT_KV_HEADS * HEAD_DIM
    o3 = o2 + IDX_HEADS * HEAD_DIM
    o4 = o3 + HEAD_DIM
    w_in = w_in.astype(bf16)
    wq = w_in[:, :o0]
    wk2 = _dup_heads(w_in[:, o0:o1], ATT_KV_HEADS)
    wv = w_in[:, o1:o2].T
    wqi = w_in[:, o2:o3]
    wki2 = _dup_heads(w_in[:, o3:o4], 1)
    wwi = _pad_cols(w_in[:, o4:], LANES).T
    cos, sin = _rope_tables(seq)
    q, k, vt, qi, ki, wit = _dsa_proj(x2, g, wq, wk2, wv, wqi, wki2, wwi, cos, sin, batch, seq)
    o = _dsa_attn(q, qi, wit, ki, k, vt, batch, seq)
    return _attn_mlp_kernel, "attn_mlp", [o, x2], [w_o.astype(bf16)]


def _rwkv_layer(x2, g, mu, w_rkv, w0, w1, w2, a0, a1, a2, g1, g2, k_k, k_a, r_k, lnx_w, lnx_b, w_o,
                batch, seq):
    vec = lambda p: p.reshape(1, D_MODEL)
    w_rkv = w_rkv.astype(bf16)
    ri = jnp.arange(RW_TM)
    tri = ((ri[:, None] >= ri[None, :]) & (ri[:, None] // CHUNK == ri[None, :] // CHUNK)).astype(bf16)
    di = jnp.arange(D_MODEL)
    hred = (di[:, None] // RWKV_HEAD == jnp.arange(LANES)[None, :]).astype(bf16)
    hexp = jnp.concatenate([hred.T, hred.T], axis=0)
    at, bt, kt, rt, v, gc, bonus, gate = _rwkv_pre(
        x2, g, _pad_rows(mu, 8), w_rkv[0], w_rkv[1], w_rkv[2],
        _pad_cols(w1, LORA_PAD).astype(bf16), _pad_rows(w2, LORA_PAD).astype(bf16),
        _pad_cols(a1, LORA_PAD).astype(bf16), _pad_rows(a2, LORA_PAD).astype(bf16),
        _pad_cols(g1, GATE_PAD).astype(bf16), _pad_rows(g2, GATE_PAD).astype(bf16),
        vec(w0), vec(a0), vec(k_k), vec(k_a), vec(r_k), tri, hred, hexp, seq)
    q, yl, mm, gg = _rwkv_chunk(at, bt, kt, rt, v, gc)
    y = _rwkv_scan(q, yl, mm, gg, batch, seq)
    return (_rwkv_mlp_kernel, "rwkv_mlp", [y, bonus, gate, x2],
            [vec(lnx_w), vec(lnx_b), hred, hexp, w_o.astype(bf16)])


def kernel(x, mixer_norm, mlp_norm, mlp_w_up, mlp_w_down, final_norm, dsa_w_in, dsa_w_o, rwkv_mu, rwkv_w_rkv, rwkv_w0, rwkv_w1, rwkv_w2, rwkv_a0, rwkv_a1, rwkv_a2, rwkv_g1, rwkv_g2, rwkv_k_k, rwkv_k_a, rwkv_r_k, rwkv_lnx_w, rwkv_lnx_b, rwkv_w_o):
    batch, seq, _ = x.shape
    depth = mixer_norm.shape[0]
    x2 = x.reshape(batch * seq, D_MODEL)
    fg = final_norm.reshape(1, D_MODEL)
    w_up = mlp_w_up.astype(bf16)
    w_down = mlp_w_down.astype(bf16)
    for i in range(depth):
        g = mixer_norm[i].reshape(1, D_MODEL)
        j = i // 2
        if i % 2 == 0:
            tail = _dsa_layer(x2, g, dsa_w_in[j], dsa_w_o[j], batch, seq)
        else:
            tail = _rwkv_layer(x2, g, rwkv_mu[j], rwkv_w_rkv[j], rwkv_w0[j], rwkv_w1[j], rwkv_w2[j],
                               rwkv_a0[j], rwkv_a1[j], rwkv_a2[j], rwkv_g1[j], rwkv_g2[j],
                               rwkv_k_k[j], rwkv_k_a[j], rwkv_r_k[j], rwkv_lnx_w[j], rwkv_lnx_b[j],
                               rwkv_w_o[j], batch, seq)
        x2 = _mixer_mlp(*tail, mlp_norm[i].reshape(1, D_MODEL), w_up, w_down, i, fg,
                        final=(i == depth - 1))
    return x2.reshape(batch, seq, D_MODEL)
```

```python
import functools

import jax
import jax.numpy as jnp
from jax import lax
from jax.experimental import pallas as pl
from jax.experimental.pallas import tpu as pltpu

f32 = jnp.float32
bf16 = jnp.bfloat16

D_MODEL = 1024
D_FF = 4 * D_MODEL
NORM_EPS = 1e-6

ATT_HEADS = 16
ATT_KV_HEADS = 4
HEAD_DIM = 64
IDX_HEADS = 8
TOPK_MAX = 256
ROPE_THETA = 10000.0

RWKV_HEAD = 64
RWKV_HEADS = 16
LNX_EPS = 64e-5
CHUNK = 64

LANES = 128
VMEM_LIMIT = 56 * 1024 * 1024

NT = (((1,), (1,)), ((), ()))
TN = (((0,), (0,)), ((), ()))


def _cparams(*sem):
    return pltpu.CompilerParams(dimension_semantics=sem, vmem_limit_bytes=VMEM_LIMIT)


def _rms(x, g):
    return x * lax.rsqrt(jnp.mean(x * x, axis=-1, keepdims=True) + NORM_EPS) * g


def _bdot(a, b, dims=None):
    a = a.astype(bf16)
    b = b.astype(bf16)
    if dims is None:
        return jnp.dot(a, b, preferred_element_type=f32)
    return lax.dot_general(a, b, dims, preferred_element_type=f32)


def _split_dot(a, b, parts):
    out = None
    rem = a
    for p in range(parts):
        piece = rem.astype(bf16)
        term = jnp.dot(piece, b, preferred_element_type=f32)
        out = term if out is None else out + term
        if p + 1 < parts:
            rem = rem - piece.astype(f32)
    return out


def _head_sum(x, hred, hexp):
    red = jnp.dot(x.astype(bf16), hred, preferred_element_type=f32)
    hi = red.astype(bf16)
    lo = (red - hi.astype(f32)).astype(bf16)
    return jnp.dot(jnp.concatenate([hi, lo], axis=1), hexp, preferred_element_type=f32)


def _const_spec(shape):
    return pl.BlockSpec(shape, lambda *_: (0,) * len(shape))


MLP_TM = 512
MLP_FCH = 512


def _mlp_tail(x, g_ref, wu_ref, wd_ref, fg_ref, o_ref, final):
    xn = _rms(x, g_ref[...]).astype(bf16)
    o_ref[...] = x
    for f in range(0, D_FF, MLP_FCH):
        u = jnp.dot(xn, wu_ref[:, f:f + MLP_FCH], preferred_element_type=f32)
        u = jnp.maximum(u, 0.0)
        o_ref[...] += jnp.dot((u * u).astype(bf16), wd_ref[f:f + MLP_FCH, :],
                              preferred_element_type=f32)
    if final:
        o_ref[...] = _rms(o_ref[...], fg_ref[...])


def _attn_mlp_kernel(a_ref, x_ref, wo_ref, g_ref, wu_ref, wd_ref, fg_ref, o_ref, *, final):
    x = x_ref[...] + jnp.dot(a_ref[...], wo_ref[...], preferred_element_type=f32)
    _mlp_tail(x, g_ref, wu_ref, wd_ref, fg_ref, o_ref, final)


def _rwkv_mlp_kernel(y_ref, bonus_ref, gate_ref, x_ref, lw_ref, lb_ref, hred_ref, hexp_ref, wo_ref,
                     g_ref, wu_ref, wd_ref, fg_ref, o_ref, *, final):
    y = y_ref[...]
    hred = hred_ref[...]
    hexp = hexp_ref[...]
    mean = _head_sum(y, hred, hexp) * (1.0 / RWKV_HEAD)
    yc = y - mean
    var = _head_sum(yc * yc, hred, hexp) * (1.0 / RWKV_HEAD)
    yn = yc * lax.rsqrt(var + LNX_EPS) * lw_ref[...] + lb_ref[...]
    out = (yn + bonus_ref[...]) * gate_ref[...]
    x = x_ref[...] + jnp.dot(out.astype(bf16), wo_ref[...], preferred_element_type=f32)
    _mlp_tail(x, g_ref, wu_ref, wd_ref, fg_ref, o_ref, final)


def _resident(shape):
    return pl.BlockSpec(shape, lambda *_: (0,) * len(shape), pipeline_mode=pl.Buffered(1))


def _resident_layer(shape, layer):
    return pl.BlockSpec((None,) + tuple(shape[1:]), lambda *_: (layer,) + (0,) * (len(shape) - 1),
                        pipeline_mode=pl.Buffered(1))


def _mixer_mlp(kernel, name, rows, consts, g, wu_all, wd_all, layer, fg, final):
    m = rows[0].shape[0]
    row = pl.BlockSpec((MLP_TM, D_MODEL), lambda i: (i, 0))
    operands = list(rows) + list(consts) + [g, wu_all, wd_all, fg]
    return pl.pallas_call(
        functools.partial(kernel, final=final),
        grid=(m // MLP_TM,),
        in_specs=([row] * len(rows) + [_resident(c.shape) for c in list(consts) + [g]]
                  + [_resident_layer(wu_all.shape, layer), _resident_layer(wd_all.shape, layer),
                     _resident(fg.shape)]),
        out_specs=row,
        out_shape=jax.ShapeDtypeStruct((m, D_MODEL), f32),
        compiler_params=_cparams("parallel"),
        name=name,
    )(*operands)


DSA_TM = 512


def _rope(x, cos, sin_signed, first_half):
    fwd = pltpu.roll(x, 32, 1)
    bwd = pltpu.roll(x, 96, 1)
    return x * cos + jnp.where(first_half, bwd, fwd) * sin_signed


LOG2E = 1.4426950408889634


VT_ROWS = 80


def _dsa_proj_kernel(x_ref, g_ref, wq_ref, wk_ref, wv_ref, wqi_ref, wki_ref, wwi_ref,
                     cos_ref, sin_ref,
                     q_ref, k_ref, vt_ref, qi_ref, ki_ref, wit_ref):
    h = _rms(x_ref[...], g_ref[...]).astype(bf16)
    tm = h.shape[0]
    cos = cos_ref[...]
    sin = sin_ref[...]
    lane = lax.broadcasted_iota(jnp.int32, (1, LANES), 1)
    first_half = (lane % HEAD_DIM) < (HEAD_DIM // 2)
    low = lane < HEAD_DIM

    def per_head(x, out_ref, pair, scale):
        xc = _rope(x[:, pair * LANES:(pair + 1) * LANES], cos, sin, first_half) * scale
        swapped = pltpu.roll(xc, HEAD_DIM, 1)
        out_ref[:, (2 * pair) * LANES:(2 * pair + 1) * LANES] = jnp.where(low, xc, 0.0).astype(bf16)
        out_ref[:, (2 * pair + 1) * LANES:(2 * pair + 2) * LANES] = jnp.where(low, swapped, 0.0).astype(bf16)

    q = jnp.dot(h, wq_ref[...], preferred_element_type=f32)
    for pair in range(ATT_HEADS // 2):
        per_head(q, q_ref, pair, HEAD_DIM ** -0.5 * LOG2E)
    qi = jnp.dot(h, wqi_ref[...], preferred_element_type=f32)
    for pair in range(IDX_HEADS // 2):
        per_head(qi, qi_ref, pair, 1.0)

    k2 = jnp.dot(h, wk_ref[...], preferred_element_type=f32)
    for c in range(0, ATT_KV_HEADS * LANES, LANES):
        kc = _rope(k2[:, c:c + LANES], cos, sin, first_half)
        k_ref[0, :, c:c + LANES] = jnp.where(low, kc, 0.0).astype(bf16)
    ki2 = _rope(jnp.dot(h, wki_ref[...], preferred_element_type=f32), cos, sin, first_half)
    ki_ref[0] = jnp.where(low, ki2, 0.0).astype(bf16)

    vt = lax.dot_general(wv_ref[...], h, NT, preferred_element_type=f32)
    extra = (lax.broadcasted_iota(jnp.int32, (VT_ROWS - HEAD_DIM, tm), 0) == 0).astype(f32)
    for g in range(ATT_KV_HEADS):
        vt_ref[0, g * VT_ROWS:(g + 1) * VT_ROWS, :] = jnp.concatenate(
            [vt[g * HEAD_DIM:(g + 1) * HEAD_DIM], extra], axis=0).astype(bf16)
    wit_ref[...] = (lax.dot_general(wwi_ref[...], h, NT, preferred_element_type=f32)
                    * (IDX_HEADS ** -0.5 * HEAD_DIM ** -0.5))


def _dsa_proj(x2, g, wq, wk2, wv, wqi, wki2, wwi, cos, sin, batch, seq):
    m = x2.shape[0]
    nblk = seq // DSA_TM
    row = lambda n: pl.BlockSpec((DSA_TM, n), lambda b, i: (b * nblk + i, 0))
    seqrow = lambda n: pl.BlockSpec((1, DSA_TM, n), lambda b, i: (b, i, 0))
    tab = pl.BlockSpec((DSA_TM, LANES), lambda b, i: (i, 0))
    kvw = ATT_KV_HEADS * LANES
    vtr = ATT_KV_HEADS * VT_ROWS
    return pl.pallas_call(
        _dsa_proj_kernel,
        grid=(batch, nblk),
        in_specs=[row(D_MODEL), _const_spec((1, D_MODEL)),
                  _const_spec(wq.shape), _const_spec(wk2.shape), _const_spec(wv.shape),
                  _const_spec(wqi.shape), _const_spec(wki2.shape), _const_spec(wwi.shape),
                  tab, tab],
        out_specs=[row(ATT_HEADS * LANES), seqrow(kvw),
                   pl.BlockSpec((1, vtr, DSA_TM), lambda b, i: (b, 0, i)),
                   row(IDX_HEADS * LANES), seqrow(LANES),
                   pl.BlockSpec((LANES, DSA_TM), lambda b, i: (0, b * nblk + i))],
        out_shape=[jax.ShapeDtypeStruct((m, ATT_HEADS * LANES), bf16),
                   jax.ShapeDtypeStruct((batch, seq, kvw), bf16),
                   jax.ShapeDtypeStruct((batch, vtr, seq), bf16),
                   jax.ShapeDtypeStruct((m, IDX_HEADS * LANES), bf16),
                   jax.ShapeDtypeStruct((batch, seq, LANES), bf16),
                   jax.ShapeDtypeStruct((LANES, m), f32)],
        compiler_params=_cparams("parallel", "parallel"),
        name="dsa_proj",
    )(x2, g, wq, wk2, wv, wqi, wki2, wwi, cos, sin)


DSA_TQ = 128
DSA_CLS = 512
IDX_GROUP = 4
DSA_ROWS = 256
INT_MIN = -2 ** 31
NEG_INF_KEY = -2139095041


FOLD_ROWS = 64


def _fold_rows(x, op):
    rows = x.shape[0]
    if rows > FOLD_ROWS and rows % FOLD_ROWS == 0:
        acc = x[:FOLD_ROWS]
        for r in range(FOLD_ROWS, rows, FOLD_ROWS):
            acc = op(acc, x[r:r + FOLD_ROWS])
        x = acc
    while x.shape[0] % 16 == 0:
        h = x.shape[0] // 2
        x = op(x[:h], x[h:])
    return x


def _col_count(mask):
    return jnp.sum(_fold_rows(jnp.where(mask, 1.0, 0.0), jnp.add), axis=0, keepdims=True).astype(jnp.int32)


def _stack_heads(ref, first, count):
    return jnp.concatenate([ref[:, (first + j) * LANES:(first + j + 1) * LANES] for j in range(count)],
                           axis=0)


def _write_topk_bias(qi_ref, wit_ref, ki_ref, okey_ref, bias_ref, causal, row_s, *, width, top_k):
    wit = wit_ref[...]
    ki = ki_ref[0, 0:width, :]
    score = jnp.zeros((width, DSA_TQ), f32)
    for first in range(0, IDX_HEADS, IDX_GROUP):
        rel = lax.dot_general(ki, _stack_heads(qi_ref, first, IDX_GROUP), NT,
                              preferred_element_type=f32)
        for j in range(IDX_GROUP):
            hd = first + j
            score = score + wit[hd:hd + 1, :] * jnp.maximum(rel[:, j * DSA_TQ:(j + 1) * DSA_TQ], 0.0)
    score = jnp.where(causal, score, -jnp.inf)

    def key_to_float(k):
        k = jnp.maximum(k, NEG_INF_KEY)
        return lax.bitcast_convert_type(k ^ ((k >> 31) & jnp.int32(0x7FFFFFFF)), f32)

    n_nonneg = _col_count(score >= 0.0)
    start_high = n_nonneg >= top_k
    thr = jnp.where(start_high, jnp.int32(0), jnp.int32(INT_MIN))
    n_ge = jnp.where(start_high, n_nonneg, jnp.int32(width))

    def value_step(i, state):
        thr, n_ge = state
        cand = thr | (jnp.int32(1) << (30 - i))
        cnt = _col_count(score >= key_to_float(cand))
        take = cnt >= top_k
        return jnp.where(take, cand, thr), jnp.where(take, cnt, n_ge)

    thr, n_ge = lax.fori_loop(0, 31, value_step, (thr, n_ge))

    thr_f = key_to_float(thr)
    bias_ref[0:width, :] = jnp.where((score >= thr_f) & causal, 0.0, -jnp.inf)
    overfull = (n_ge != top_k) & (thr > NEG_INF_KEY)

    @pl.when(jnp.max(overfull.astype(f32)) > 0.0)
    def _():
        real_thr = thr > NEG_INF_KEY
        next_f = key_to_float(thr + 1)
        beyond = score >= next_f
        in_bin = (score >= thr_f) & jnp.logical_not(beyond) & real_thr
        offset = jnp.where(in_bin, score - jnp.where(real_thr, thr_f, 0.0), -1.0)
        need = top_k - _col_count(beyond)

        okey_ref[...] = jnp.zeros(okey_ref.shape, jnp.int32)

        @pl.when(jnp.max(offset) > 0.0)
        def _():
            def offset_step(i, okey):
                cand = okey | (jnp.int32(1) << (30 - i))
                cnt = _col_count(offset >= lax.bitcast_convert_type(cand, f32))
                return jnp.where(cnt >= need, cand, okey)

            okey = lax.fori_loop(0, 31, offset_step, jnp.zeros((1, DSA_TQ), jnp.int32))
            okey_ref[...] = jnp.broadcast_to(okey, okey_ref.shape)

        othr = lax.bitcast_convert_type(okey_ref[0:1, :], f32)
        above = offset > othr
        tied = offset == othr
        need = need - _col_count(above)
        nbits = (width - 1).bit_length()

        def index_step(i, pos):
            cand = pos | (jnp.int32(1) << (nbits - 1 - i))
            return jnp.where(_col_count(tied & (row_s < cand)) < need, cand, pos)

        pos = lax.fori_loop(0, nbits, index_step, jnp.zeros((1, DSA_TQ), jnp.int32))
        keep = (beyond | above | (tied & (row_s <= pos))) & causal
        bias_ref[0:width, :] = jnp.where(keep, 0.0, -jnp.inf)


def _dsa_attn_body(q_ref, qi_ref, wit_ref, ki_ref, k_ref, vt_ref, o_ref,
                   okey_ref, bias_ref, logit_ref, *, width, top_k, t0, select):
    col_t = t0 + lax.broadcasted_iota(jnp.int32, (1, DSA_TQ), 1)
    row_s = lax.broadcasted_iota(jnp.int32, (width, 1), 0)
    causal = row_s <= col_t
    if select:
        _write_topk_bias(qi_ref, wit_ref, ki_ref, okey_ref, bias_ref, causal, row_s,
                         width=width, top_k=top_k)
    else:
        bias_ref[0:width, :] = jnp.where(causal, 0.0, -jnp.inf)

    group = ATT_HEADS // ATT_KV_HEADS
    rows = min(DSA_ROWS, width)
    for g in range(ATT_KV_HEADS):
        rq = _stack_heads(q_ref, g * group, group)
        maxes = [None] * group
        for rc in range(0, width, rows):
            logits4 = lax.dot_general(k_ref[0, rc:rc + rows, g * LANES:(g + 1) * LANES], rq, NT,
                                      preferred_element_type=f32)
            b = bias_ref[rc:rc + rows, :]
            for j in range(group):
                logits = logits4[:, j * DSA_TQ:(j + 1) * DSA_TQ] + b
                logit_ref[j, rc:rc + rows, :] = logits
                m = _fold_rows(logits, jnp.maximum)
                maxes[j] = m if maxes[j] is None else jnp.maximum(maxes[j], m)
        outs = []
        for j in range(group):
            mx = jnp.max(maxes[j], axis=0, keepdims=True)
            pv = None
            for rc in range(0, width, rows):
                p = jnp.exp2(logit_ref[j, rc:rc + rows, :] - mx).astype(bf16)
                part = jnp.dot(vt_ref[0, g * VT_ROWS:(g + 1) * VT_ROWS, rc:rc + rows], p,
                               preferred_element_type=f32)
                pv = part if pv is None else pv + part
            outs.append(pv[:HEAD_DIM] * (1.0 / pv[HEAD_DIM:HEAD_DIM + 1]))
        for pair in range(group // 2):
            both = jnp.concatenate(outs[2 * pair:2 * pair + 2], axis=0).T
            col = (g * group // 2 + pair) * LANES
            o_ref[:, col:col + LANES] = both.astype(bf16)


def _dsa_attn_kernel(q_ref, qi_ref, wit_ref, ki_ref, k_ref, vt_ref, o_ref,
                     okey_ref, bias_ref, logit_ref, *, seq, top_k):
    i = pl.program_id(1)
    refs = (q_ref, qi_ref, wit_ref, ki_ref, k_ref, vt_ref, o_ref, okey_ref, bias_ref, logit_ref)
    dense_blocks = top_k // DSA_TQ

    @pl.when(i < dense_blocks)
    def _():
        _dsa_attn_body(*refs, width=dense_blocks * DSA_TQ, top_k=top_k, t0=i * DSA_TQ, select=False)

    blocks_per_class = DSA_CLS // DSA_TQ
    for cls in range(seq // DSA_CLS):
        @pl.when((i >= dense_blocks) & (i // blocks_per_class == cls))
        def _(cls=cls):
            _dsa_attn_body(*refs, width=DSA_CLS * (cls + 1), top_k=top_k, t0=i * DSA_TQ, select=True)


def _dsa_attn(q, qi, wit, ki, k, vt, batch, seq):
    nq = seq // DSA_TQ
    top_k = min(TOPK_MAX, seq // 4)
    qrow = lambda n: pl.BlockSpec((DSA_TQ, n), lambda b, i: (b * nq + i, 0))
    full = lambda n: pl.BlockSpec((1, seq, n), lambda b, i: (b, 0, 0))
    return pl.pallas_call(
        functools.partial(_dsa_attn_kernel, seq=seq, top_k=top_k),
        grid=(batch, nq),
        in_specs=[qrow(ATT_HEADS * LANES), qrow(IDX_HEADS * LANES),
                  pl.BlockSpec((8, DSA_TQ), lambda b, i: (0, b * nq + i)),
                  full(LANES), full(ATT_KV_HEADS * LANES),
                  pl.BlockSpec((1, ATT_KV_HEADS * VT_ROWS, seq), lambda b, i: (b, 0, 0))],
        out_specs=qrow(ATT_HEADS * HEAD_DIM),
        out_shape=jax.ShapeDtypeStruct((batch * seq, ATT_HEADS * HEAD_DIM), bf16),
        scratch_shapes=[pltpu.VMEM((8, DSA_TQ), jnp.int32),
                        pltpu.VMEM((seq, DSA_TQ), f32),
                        pltpu.VMEM((ATT_HEADS // ATT_KV_HEADS, seq, DSA_TQ), f32)],
        compiler_params=_cparams("parallel", "arbitrary"),
        name="dsa_attn",
    )(q, qi, wit, ki, k, vt)


RW_TM = 512
LORA_PAD = 128
GATE_PAD = 256


def _rwkv_pre_kernel(x_ref, xp_ref, g_ref, mu_ref, wr_ref, wk_ref, wv_ref,
                     w1_ref, w2_ref, a1_ref, a2_ref, g1_ref, g2_ref,
                     w0_ref, a0_ref, kk_ref, ka_ref, rk_ref, tri_ref, hred_ref, hexp_ref,
                     at_ref, bt_ref, kt_ref, rt_ref, v_ref, gc_ref, bonus_ref, gate_ref,
                     lc_scr, *, seq):
    i = pl.program_id(0)
    g = g_ref[...]
    h = _rms(x_ref[...], g)
    hp = _rms(xp_ref[...], g)[7:8, :]
    hp = jnp.where((i * RW_TM) % seq == 0, 0.0, hp)
    rowi = lax.broadcasted_iota(jnp.int32, (RW_TM, 1), 0)
    hs = jnp.where(rowi == 0, hp, pltpu.roll(h, 1, 0))
    xx = hs - h
    mu = mu_ref[...]

    def mix(c):
        return (h + xx * mu[c:c + 1, :]).astype(bf16)

    r = jnp.dot(mix(0), wr_ref[...], preferred_element_type=f32)
    k = jnp.dot(mix(1), wk_ref[...], preferred_element_type=f32)
    v = jnp.dot(mix(2), wv_ref[...], preferred_element_type=f32)

    wl = w0_ref[...] + _bdot(jnp.tanh(jnp.dot(mix(3), w1_ref[...], preferred_element_type=f32)),
                             w2_ref[...])
    nwl = -wl
    softplus = jnp.maximum(nwl, 0.0) + jnp.log1p(jnp.exp(-jnp.abs(nwl)))
    ld = -jnp.exp(-softplus - 0.5)
    a = jax.nn.sigmoid(a0_ref[...] + _bdot(jnp.dot(mix(4), a1_ref[...],
                                                   preferred_element_type=f32), a2_ref[...]))
    gate_ref[...] = _bdot(jax.nn.sigmoid(jnp.dot(mix(5), g1_ref[...],
                                                 preferred_element_type=f32)), g2_ref[...]).astype(bf16)

    hred = hred_ref[...]
    hexp = hexp_ref[...]
    z = k * kk_ref[...]
    kk = z * lax.rsqrt(jnp.maximum(_head_sum(z * z, hred, hexp), 1e-24))
    k2 = k * (1.0 + (a - 1.0) * ka_ref[...])
    bonus_ref[...] = (_head_sum(r * k2 * rk_ref[...], hred, hexp) * v).astype(bf16)

    lc = _split_dot_left(tri_ref[...], ld, 2)
    lc_scr[...] = lc
    for c in range(RW_TM // CHUNK):
        last = lc_scr[c * CHUNK + CHUNK - 1:c * CHUNK + CHUNK, :]
        gc_ref[c * 8:(c + 1) * 8, :] = jnp.broadcast_to(jnp.exp(last), (8, D_MODEL))
    einv = jnp.exp(-lc)
    at_ref[...] = -kk * jnp.exp(lc - ld)
    bt_ref[...] = kk * a * einv
    kt_ref[...] = k2 * einv
    rt_ref[...] = r * jnp.exp(lc)
    v_ref[...] = v


def _split_dot_left(a, b, parts):
    out = None
    rem = b
    for p in range(parts):
        piece = rem.astype(bf16)
        term = jnp.dot(a, piece, preferred_element_type=f32)
        out = term if out is None else out + term
        if p + 1 < parts:
            rem = rem - piece.astype(f32)
    return out


def _rwkv_pre(x2, g, mu8, wr, wk, wv, w1, w2, a1, a2, g1, g2, w0, a0, k_k, k_a, r_k, tri, hred, hexp,
              seq):
    m = x2.shape[0]
    row = pl.BlockSpec((RW_TM, D_MODEL), lambda i: (i, 0))
    prev = pl.BlockSpec((8, D_MODEL), lambda i: (jnp.maximum(i * (RW_TM // 8) - 1, 0), 0))
    vec = _const_spec((1, D_MODEL))
    gcrow = pl.BlockSpec((RW_TM // CHUNK * 8, D_MODEL), lambda i: (i, 0))
    act = jax.ShapeDtypeStruct((m, D_MODEL), f32)
    return pl.pallas_call(
        functools.partial(_rwkv_pre_kernel, seq=seq),
        grid=(m // RW_TM,),
        in_specs=[row, prev, vec, _const_spec((8, D_MODEL)),
                  _const_spec(wr.shape), _const_spec(wk.shape), _const_spec(wv.shape),
                  _const_spec(w1.shape), _const_spec(w2.shape),
                  _const_spec(a1.shape), _const_spec(a2.shape),
                  _const_spec(g1.shape), _const_spec(g2.shape),
                  vec, vec, vec, vec, vec,
                  _const_spec(tri.shape), _const_spec(hred.shape), _const_spec(hexp.shape)],
        out_specs=[row, row, row, row, row, gcrow, row, row],
        out_shape=[act, act, act, act, act,
                   jax.ShapeDtypeStruct((m // CHUNK * 8, D_MODEL), f32),
                   jax.ShapeDtypeStruct((m, D_MODEL), bf16), jax.ShapeDtypeStruct((m, D_MODEL), bf16)],
        scratch_shapes=[pltpu.VMEM((RW_TM, D_MODEL), f32)],
        compiler_params=_cparams("parallel"),
        name="rwkv_pre",
    )(x2, x2, g, mu8, wr, wk, wv, w1, w2, a1, a2, g1, g2, w0, a0, k_k, k_a, r_k, tri, hred, hexp)


RA_GROUP = 2 * CHUNK
RA_ROWS = 1024


def _rwkv_chunk_kernel(at_ref, bt_ref, kt_ref, rt_ref, v_ref, gc_ref,
                       q_ref, yl_ref, m_ref, g_ref):
    n = RA_GROUP
    ri = lax.broadcasted_iota(jnp.int32, (n, n), 0)
    ci = lax.broadcasted_iota(jnp.int32, (n, n), 1)
    same = (ri // CHUNK) == (ci // CHUNK)
    strict = same & (ci < ri)
    incl = same & (ci <= ri)
    eye = (ri == ci).astype(f32)
    lane = lax.broadcasted_iota(jnp.int32, (1, LANES), 1)
    low = lane < RWKV_HEAD
    rk = lax.broadcasted_iota(jnp.int32, (RWKV_HEAD, LANES), 0)
    diag = rk == (lax.broadcasted_iota(jnp.int32, (RWKV_HEAD, LANES), 1) % RWKV_HEAD)

    groups = []
    for gi in range(RA_ROWS // n):
        rows = slice(gi * n, (gi + 1) * n)
        at, bt, kt, rt, v = (r[rows, :] for r in (at_ref, bt_ref, kt_ref, rt_ref, v_ref))
        groups.append(dict(at=at, bt=bt, kt=kt, rt=rt, v=v, vb=v.astype(bf16),
                           ar=jnp.concatenate([at, rt], axis=0),
                           bk=jnp.concatenate([bt, kt], axis=0).astype(bf16)))
    probs = [(gr, hl) for gr in groups for hl in (low, jnp.logical_not(low))]

    gs = [lax.dot_general(jnp.where(hl, gr["ar"], 0.0).astype(bf16), gr["bk"], NT,
                          preferred_element_type=f32) for gr, hl in probs]
    aab = [jnp.where(strict, g[:n, :n], 0.0) for g in gs]
    aak = [jnp.where(strict, g[:n, n:], 0.0) for g in gs]
    arb = [jnp.where(incl, g[n:, :n], 0.0) for g in gs]
    ark = [jnp.where(incl, g[n:, n:], 0.0) for g in gs]

    ts = [eye + a for a in aab]
    ps = aab
    for _ in range((CHUNK - 1).bit_length() - 1):
        ps = [_bdot(p, p) for p in ps]
        ts = [t + _bdot(t, p) for t, p in zip(ts, ps)]

    akv = [_bdot(a, gr["vb"]) for a, (gr, _) in zip(aak, probs)]
    tw = [_bdot(t, jnp.concatenate([gr["at"], x], axis=1))
          for t, x, (gr, _) in zip(ts, akv, probs)]
    qy = [_bdot(a, w) for a, w in zip(arb, tw)]
    rkv = [_bdot(a, gr["vb"]) for a, (gr, _) in zip(ark, probs)]

    for gi, gr in enumerate(groups):
        lo, hi = 2 * gi, 2 * gi + 1
        w_hat = jnp.where(low, tw[lo][:, :LANES], tw[hi][:, :LANES])
        u_loc = jnp.where(low, tw[lo][:, LANES:], tw[hi][:, LANES:])
        base = gi * n
        q_ref[base:base + n, :] = gr["rt"] + jnp.where(low, qy[lo][:, :LANES], qy[hi][:, :LANES])
        yl_ref[base:base + n, :] = (jnp.where(low, qy[lo][:, LANES:], qy[hi][:, LANES:])
                                    + jnp.where(low, rkv[lo], rkv[hi]))
        for c in range(n // CHUNK):
            rows = slice(c * CHUNK, (c + 1) * CHUNK)
            ch = base // CHUNK + c
            gc = gc_ref[ch * 8:ch * 8 + 1, :]
            bh = gr["bt"][rows] * gc
            kh = gr["kt"][rows] * gc
            pm = _bdot(w_hat[rows], bh, TN)
            pg = _bdot(jnp.concatenate([u_loc[rows], gr["v"][rows]], axis=0),
                       jnp.concatenate([bh, kh], axis=0), TN)
            out = slice(base + c * CHUNK, base + (c + 1) * CHUNK)
            m_ref[out, :] = (jnp.where(low, pm[:RWKV_HEAD], pm[RWKV_HEAD:])
                             + jnp.where(diag, gc, 0.0))
            g_ref[out, :] = jnp.where(low, pg[:RWKV_HEAD], pg[RWKV_HEAD:])


def _rwkv_chunk(at, bt, kt, rt, v, gc):
    m = at.shape[0]
    blk = pl.BlockSpec((RA_ROWS, LANES), lambda i, p: (i, p))
    gblk = pl.BlockSpec((RA_ROWS // CHUNK * 8, LANES), lambda i, p: (i, p))
    act = jax.ShapeDtypeStruct((m, D_MODEL), f32)
    return pl.pallas_call(
        _rwkv_chunk_kernel,
        grid=(m // RA_ROWS, D_MODEL // LANES),
        in_specs=[blk, blk, blk, blk, blk, gblk],
        out_specs=[blk, blk, blk, blk],
        out_shape=[act, act, act, act],
        compiler_params=_cparams("parallel", "parallel"),
        name="rwkv_chunk",
    )(at, bt, kt, rt, v, gc)


SCAN_LANES = D_MODEL
SCAN_SPLIT = 2


def _rwkv_scan_kernel(q_ref, yl_ref, m_ref, g_ref, y_ref, state_ref, *, seq):
    pairs = SCAN_LANES // LANES
    low = lax.broadcasted_iota(jnp.int32, (1, LANES), 1) < RWKV_HEAD

    def blockdiag(x):
        return jnp.concatenate([jnp.where(low, x, 0.0), jnp.where(low, 0.0, x)], axis=0)

    def body(c, states):
        rows = pl.ds(pl.multiple_of(c * CHUNK, CHUNK), CHUNK)
        new_states = []
        for p in range(pairs):
            cols = slice(p * LANES, (p + 1) * LANES)
            s = states[p]
            mc = blockdiag(m_ref[rows, cols]).astype(bf16)
            s_hi = s.astype(bf16)
            s_lo = (s - s_hi.astype(f32)).astype(bf16)
            sm = (jnp.dot(s_hi, mc, preferred_element_type=f32)
                  + jnp.dot(s_lo, mc, preferred_element_type=f32))
            new_states.append(sm + g_ref[rows, cols])
        for p in range(pairs):
            cols = slice(p * LANES, (p + 1) * LANES)
            y_ref[rows, cols] = (yl_ref[rows, cols]
                                 + _bdot(q_ref[rows, cols], blockdiag(states[p]), NT))
        return tuple(new_states)

    @pl.when(pl.program_id(1) == 0)
    def _():
        state_ref[...] = jnp.zeros(state_ref.shape, f32)

    init = tuple(state_ref[p] for p in range(pairs))
    final = lax.fori_loop(0, seq // SCAN_SPLIT // CHUNK, body, init)
    for p in range(pairs):
        state_ref[p] = final[p]


def _rwkv_scan(q, yl, mm, gg, batch, seq):
    rows = seq // SCAN_SPLIT
    blk = pl.BlockSpec((rows, SCAN_LANES), lambda b, s: (b * SCAN_SPLIT + s, 0))
    return pl.pallas_call(
        functools.partial(_rwkv_scan_kernel, seq=seq),
        grid=(batch, SCAN_SPLIT),
        in_specs=[blk, blk, blk, blk],
        out_specs=blk,
        out_shape=jax.ShapeDtypeStruct((batch * seq, D_MODEL), f32),
        scratch_shapes=[pltpu.VMEM((SCAN_LANES // LANES, RWKV_HEAD, LANES), f32)],
        compiler_params=_cparams("parallel", "arbitrary"),
        name="rwkv_scan",
    )(q, yl, mm, gg)


def _dup_heads(w, n_heads):
    w = w.reshape(w.shape[0], n_heads, 1, HEAD_DIM)
    return jnp.broadcast_to(w, (w.shape[0], n_heads, 2, HEAD_DIM)).reshape(w.shape[0], n_heads * LANES)


def _pad_cols(w, n):
    return jnp.pad(w, ((0, 0), (0, n - w.shape[1])))


def _pad_rows(w, n):
    return jnp.pad(w, ((0, n - w.shape[0]), (0, 0)))


def _rope_tables(seq):
    inv = 1.0 / (ROPE_THETA ** (jnp.arange(0, HEAD_DIM, 2, dtype=f32) / HEAD_DIM))
    ang = jnp.arange(seq, dtype=f32)[:, None] * inv[None, :]
    cos, sin = jnp.cos(ang), jnp.sin(ang)
    return jnp.tile(jnp.concatenate([cos, cos], 1), (1, 2)), jnp.tile(jnp.concatenate([-sin, sin], 1), (1, 2))


def _dsa_layer(x2, g, w_in, w_o, batch, seq):
    o0 = ATT_HEADS * HEAD_DIM
    o1 = o0 + ATT_KV_HEADS * HEAD_DIM
    o2 = o1 + ATT_KV_HEADS * HEAD_DIM
    o3 = o2 + IDX_HEADS * HEAD_DIM
    o4 = o3 + HEAD_DIM
    w_in = w_in.astype(bf16)
    wq = w_in[:, :o0]
    wk2 = _dup_heads(w_in[:, o0:o1], ATT_KV_HEADS)
    wv = w_in[:, o1:o2].T
    wqi = w_in[:, o2:o3]
    wki2 = _dup_heads(w_in[:, o3:o4], 1)
    wwi = _pad_cols(w_in[:, o4:], LANES).T
    cos, sin = _rope_tables(seq)
    q, k, vt, qi, ki, wit = _dsa_proj(x2, g, wq, wk2, wv, wqi, wki2, wwi, cos, sin, batch, seq)
    o = _dsa_attn(q, qi, wit, ki, k, vt, batch, seq)
    return _attn_mlp_kernel, "attn_mlp", [o, x2], [w_o.astype(bf16)]


def _rwkv_layer(x2, g, mu, w_rkv, w0, w1, w2, a0, a1, a2, g1, g2, k_k, k_a, r_k, lnx_w, lnx_b, w_o,
                batch, seq):
    vec = lambda p: p.reshape(1, D_MODEL)
    w_rkv = w_rkv.astype(bf16)
    ri = jnp.arange(RW_TM)
    tri = ((ri[:, None] >= ri[None, :]) & (ri[:, None] // CHUNK == ri[None, :] // CHUNK)).astype(bf16)
    di = jnp.arange(D_MODEL)
    hred = (di[:, None] // RWKV_HEAD == jnp.arange(LANES)[None, :]).astype(bf16)
    hexp = jnp.concatenate([hred.T, hred.T], axis=0)
    at, bt, kt, rt, v, gc, bonus, gate = _rwkv_pre(
        x2, g, _pad_rows(mu, 8), w_rkv[0], w_rkv[1], w_rkv[2],
        _pad_cols(w1, LORA_PAD).astype(bf16), _pad_rows(w2, LORA_PAD).astype(bf16),
        _pad_cols(a1, LORA_PAD).astype(bf16), _pad_rows(a2, LORA_PAD).astype(bf16),
        _pad_cols(g1, GATE_PAD).astype(bf16), _pad_rows(g2, GATE_PAD).astype(bf16),
        vec(w0), vec(a0), vec(k_k), vec(k_a), vec(r_k), tri, hred, hexp, seq)
    q, yl, mm, gg = _rwkv_chunk(at, bt, kt, rt, v, gc)
    y = _rwkv_scan(q, yl, mm, gg, batch, seq)
    return (_rwkv_mlp_kernel, "rwkv_mlp", [y, bonus, gate, x2],
            [vec(lnx_w), vec(lnx_b), hred, hexp, w_o.astype(bf16)])


def kernel(x, mixer_norm, mlp_norm, mlp_w_up, mlp_w_down, final_norm, dsa_w_in, dsa_w_o, rwkv_mu, rwkv_w_rkv, rwkv_w0, rwkv_w1, rwkv_w2, rwkv_a0, rwkv_a1, rwkv_a2, rwkv_g1, rwkv_g2, rwkv_k_k, rwkv_k_a, rwkv_r_k, rwkv_lnx_w, rwkv_lnx_b, rwkv_w_o):
    batch, seq, _ = x.shape
    depth = mixer_norm.shape[0]
    x2 = x.reshape(batch * seq, D_MODEL)
    fg = final_norm.reshape(1, D_MODEL)
    w_up = mlp_w_up.astype(bf16)
    w_down = mlp_w_down.astype(bf16)
    for i in range(depth):
        g = mixer_norm[i].reshape(1, D_MODEL)
        j = i // 2
        if i % 2 == 0:
            tail = _dsa_layer(x2, g, dsa_w_in[j], dsa_w_o[j], batch, seq)
        else:
            tail = _rwkv_layer(x2, g, rwkv_mu[j], rwkv_w_rkv[j], rwkv_w0[j], rwkv_w1[j], rwkv_w2[j],
                               rwkv_a0[j], rwkv_a1[j], rwkv_a2[j], rwkv_g1[j], rwkv_g2[j],
                               rwkv_k_k[j], rwkv_k_a[j], rwkv_r_k[j], rwkv_lnx_w[j], rwkv_lnx_b[j],
                               rwkv_w_o[j], batch, seq)
        x2 = _mixer_mlp(*tail, mlp_norm[i].reshape(1, D_MODEL), w_up, w_down, i, fg,
                        final=(i == depth - 1))
    return x2.reshape(batch, seq, D_MODEL)
```

```python
import functools

import jax
import jax.numpy as jnp
from jax import lax
from jax.experimental import pallas as pl
from jax.experimental.pallas import tpu as pltpu

f32 = jnp.float32
bf16 = jnp.bfloat16

D_MODEL = 1024
D_FF = 4 * D_MODEL
NORM_EPS = 1e-6

ATT_HEADS = 16
ATT_KV_HEADS = 4
HEAD_DIM = 64
IDX_HEADS = 8
TOPK_MAX = 256
ROPE_THETA = 10000.0

RWKV_HEAD = 64
RWKV_HEADS = 16
LNX_EPS = 64e-5
CHUNK = 64

LANES = 128
VMEM_LIMIT = 56 * 1024 * 1024

NT = (((1,), (1,)), ((), ()))
TN = (((0,), (0,)), ((), ()))


def _cparams(*sem):
    return pltpu.CompilerParams(dimension_semantics=sem, vmem_limit_bytes=VMEM_LIMIT)


def _rms(x, g):
    return x * lax.rsqrt(jnp.mean(x * x, axis=-1, keepdims=True) + NORM_EPS) * g


def _bdot(a, b, dims=None):
    a = a.astype(bf16)
    b = b.astype(bf16)
    if dims is None:
        return jnp.dot(a, b, preferred_element_type=f32)
    return lax.dot_general(a, b, dims, preferred_element_type=f32)


def _split_dot(a, b, parts):
    out = None
    rem = a
    for p in range(parts):
        piece = rem.astype(bf16)
        term = jnp.dot(piece, b, preferred_element_type=f32)
        out = term if out is None else out + term
        if p + 1 < parts:
            rem = rem - piece.astype(f32)
    return out


def _head_sum(x, hred, hexp):
    red = jnp.dot(x.astype(bf16), hred, preferred_element_type=f32)
    hi = red.astype(bf16)
    lo = (red - hi.astype(f32)).astype(bf16)
    return jnp.dot(jnp.concatenate([hi, lo], axis=1), hexp, preferred_element_type=f32)


def _const_spec(shape):
    return pl.BlockSpec(shape, lambda *_: (0,) * len(shape))


MLP_TM = 512
MLP_FCH = 512


def _mlp_tail(x, g_ref, wu_ref, wd_ref, fg_ref, o_ref, final):
    xn = _rms(x, g_ref[...]).astype(bf16)
    o_ref[...] = x
    for f in range(0, D_FF, MLP_FCH):
        u = jnp.dot(xn, wu_ref[:, f:f + MLP_FCH], preferred_element_type=f32)
        u = jnp.maximum(u, 0.0)
        o_ref[...] += jnp.dot((u * u).astype(bf16), wd_ref[f:f + MLP_FCH, :],
                              preferred_element_type=f32)
    if final:
        o_ref[...] = _rms(o_ref[...], fg_ref[...])


def _attn_mlp_kernel(a_ref, x_ref, wo_ref, g_ref, wu_ref, wd_ref, fg_ref, o_ref, *, final):
    x = x_ref[...] + jnp.dot(a_ref[...], wo_ref[...], preferred_element_type=f32)
    _mlp_tail(x, g_ref, wu_ref, wd_ref, fg_ref, o_ref, final)


def _rwkv_mlp_kernel(y_ref, bonus_ref, gate_ref, x_ref, lw_ref, lb_ref, hred_ref, hexp_ref, wo_ref,
                     g_ref, wu_ref, wd_ref, fg_ref, o_ref, *, final):
    y = y_ref[...]
    hred = hred_ref[...]
    hexp = hexp_ref[...]
    mean = _head_sum(y, hred, hexp) * (1.0 / RWKV_HEAD)
    yc = y - mean
    var = _head_sum(yc * yc, hred, hexp) * (1.0 / RWKV_HEAD)
    yn = yc * lax.rsqrt(var + LNX_EPS) * lw_ref[...] + lb_ref[...]
    out = (yn + bonus_ref[...]) * gate_ref[...]
    x = x_ref[...] + jnp.dot(out.astype(bf16), wo_ref[...], preferred_element_type=f32)
    _mlp_tail(x, g_ref, wu_ref, wd_ref, fg_ref, o_ref, final)


def _resident(shape):
    return pl.BlockSpec(shape, lambda *_: (0,) * len(shape), pipeline_mode=pl.Buffered(1))


def _resident_layer(shape, layer):
    return pl.BlockSpec((None,) + tuple(shape[1:]), lambda *_: (layer,) + (0,) * (len(shape) - 1),
                        pipeline_mode=pl.Buffered(1))


def _mixer_mlp(kernel, name, rows, consts, g, wu_all, wd_all, layer, fg, final):
    m = rows[0].shape[0]
    row = pl.BlockSpec((MLP_TM, D_MODEL), lambda i: (i, 0))
    operands = list(rows) + list(consts) + [g, wu_all, wd_all, fg]
    return pl.pallas_call(
        functools.partial(kernel, final=final),
        grid=(m // MLP_TM,),
        in_specs=([row] * len(rows) + [_resident(c.shape) for c in list(consts) + [g]]
                  + [_resident_layer(wu_all.shape, layer), _resident_layer(wd_all.shape, layer),
                     _resident(fg.shape)]),
        out_specs=row,
        out_shape=jax.ShapeDtypeStruct((m, D_MODEL), f32),
        compiler_params=_cparams("parallel"),
        name=name,
    )(*operands)


DSA_TM = 512


def _rope(x, cos, sin_signed, first_half):
    fwd = pltpu.roll(x, 32, 1)
    bwd = pltpu.roll(x, 96, 1)
    return x * cos + jnp.where(first_half, bwd, fwd) * sin_signed


LOG2E = 1.4426950408889634


VT_ROWS = 80


def _dsa_proj_kernel(x_ref, g_ref, wq_ref, wk_ref, wv_ref, wqi_ref, wki_ref, wwi_ref,
                     cos_ref, sin_ref,
                     q_ref, k_ref, vt_ref, qi_ref, ki_ref, wit_ref):
    h = _rms(x_ref[...], g_ref[...]).astype(bf16)
    tm = h.shape[0]
    cos = cos_ref[...]
    sin = sin_ref[...]
    lane = lax.broadcasted_iota(jnp.int32, (1, LANES), 1)
    first_half = (lane % HEAD_DIM) < (HEAD_DIM // 2)
    low = lane < HEAD_DIM

    def per_head(x, out_ref, pair, scale):
        xc = _rope(x[:, pair * LANES:(pair + 1) * LANES], cos, sin, first_half) * scale
        swapped = pltpu.roll(xc, HEAD_DIM, 1)
        out_ref[:, (2 * pair) * LANES:(2 * pair + 1) * LANES] = jnp.where(low, xc, 0.0).astype(bf16)
        out_ref[:, (2 * pair + 1) * LANES:(2 * pair + 2) * LANES] = jnp.where(low, swapped, 0.0).astype(bf16)

    q = jnp.dot(h, wq_ref[...], preferred_element_type=f32)
    for pair in range(ATT_HEADS // 2):
        per_head(q, q_ref, pair, HEAD_DIM ** -0.5 * LOG2E)
    qi = jnp.dot(h, wqi_ref[...], preferred_element_type=f32)
    for pair in range(IDX_HEADS // 2):
        per_head(qi, qi_ref, pair, 1.0)

    k2 = jnp.dot(h, wk_ref[...], preferred_element_type=f32)
    for c in range(0, ATT_KV_HEADS * LANES, LANES):
        kc = _rope(k2[:, c:c + LANES], cos, sin, first_half)
        k_ref[0, :, c:c + LANES] = jnp.where(low, kc, 0.0).astype(bf16)
    ki2 = _rope(jnp.dot(h, wki_ref[...], preferred_element_type=f32), cos, sin, first_half)
    ki_ref[0] = jnp.where(low, ki2, 0.0).astype(bf16)

    vt = lax.dot_general(wv_ref[...], h, NT, preferred_element_type=f32)
    extra = (lax.broadcasted_iota(jnp.int32, (VT_ROWS - HEAD_DIM, tm), 0) == 0).astype(f32)
    for g in range(ATT_KV_HEADS):
        vt_ref[0, g * VT_ROWS:(g + 1) * VT_ROWS, :] = jnp.concatenate(
            [vt[g * HEAD_DIM:(g + 1) * HEAD_DIM], extra], axis=0).astype(bf16)
    wit_ref[...] = (lax.dot_general(wwi_ref[...], h, NT, preferred_element_type=f32)
                    * (IDX_HEADS ** -0.5 * HEAD_DIM ** -0.5))


def _dsa_proj(x2, g, wq, wk2, wv, wqi, wki2, wwi, cos, sin, batch, seq):
    m = x2.shape[0]
    nblk = seq // DSA_TM
    row = lambda n: pl.BlockSpec((DSA_TM, n), lambda b, i: (b * nblk + i, 0))
    seqrow = lambda n: pl.BlockSpec((1, DSA_TM, n), lambda b, i: (b, i, 0))
    tab = pl.BlockSpec((DSA_TM, LANES), lambda b, i: (i, 0))
    kvw = ATT_KV_HEADS * LANES
    vtr = ATT_KV_HEADS * VT_ROWS
    return pl.pallas_call(
        _dsa_proj_kernel,
        grid=(batch, nblk),
        in_specs=[row(D_MODEL), _const_spec((1, D_MODEL)),
                  _const_spec(wq.shape), _const_spec(wk2.shape), _const_spec(wv.shape),
                  _const_spec(wqi.shape), _const_spec(wki2.shape), _const_spec(wwi.shape),
                  tab, tab],
        out_specs=[row(ATT_HEADS * LANES), seqrow(kvw),
                   pl.BlockSpec((1, vtr, DSA_TM), lambda b, i: (b, 0, i)),
                   row(IDX_HEADS * LANES), seqrow(LANES),
                   pl.BlockSpec((LANES, DSA_TM), lambda b, i: (0, b * nblk + i))],
        out_shape=[jax.ShapeDtypeStruct((m, ATT_HEADS * LANES), bf16),
                   jax.ShapeDtypeStruct((batch, seq, kvw), bf16),
                   jax.ShapeDtypeStruct((batch, vtr, seq), bf16),
                   jax.ShapeDtypeStruct((m, IDX_HEADS * LANES), bf16),
                   jax.ShapeDtypeStruct((batch, seq, LANES), bf16),
                   jax.ShapeDtypeStruct((LANES, m), f32)],
        compiler_params=_cparams("parallel", "parallel"),
        name="dsa_proj",
    )(x2, g, wq, wk2, wv, wqi, wki2, wwi, cos, sin)


DSA_TQ = 128
DSA_CLS = 512
IDX_GROUP = 4
DSA_ROWS = 256
INT_MIN = -2 ** 31
NEG_INF_KEY = -2139095041


FOLD_ROWS = 64


def _fold_rows(x, op):
    rows = x.shape[0]
    if rows > FOLD_ROWS and rows % FOLD_ROWS == 0:
        acc = x[:FOLD_ROWS]
        for r in range(FOLD_ROWS, rows, FOLD_ROWS):
            acc = op(acc, x[r:r + FOLD_ROWS])
        x = acc
    while x.shape[0] % 16 == 0:
        h = x.shape[0] // 2
        x = op(x[:h], x[h:])
    return x


def _col_count(mask):
    return jnp.sum(_fold_rows(jnp.where(mask, 1.0, 0.0), jnp.add), axis=0, keepdims=True).astype(jnp.int32)


def _stack_heads(ref, first, count):
    return jnp.concatenate([ref[:, (first + j) * LANES:(first + j + 1) * LANES] for j in range(count)],
                           axis=0)


def _write_topk_bias(qi_ref, wit_ref, ki_ref, okey_ref, bias_ref, causal, row_s, *, width, top_k):
    wit = wit_ref[...]
    ki = ki_ref[0, 0:width, :]
    score = jnp.zeros((width, DSA_TQ), f32)
    for first in range(0, IDX_HEADS, IDX_GROUP):
        rel = lax.dot_general(ki, _stack_heads(qi_ref, first, IDX_GROUP), NT,
                              preferred_element_type=f32)
        for j in range(IDX_GROUP):
            hd = first + j
            score = score + wit[hd:hd + 1, :] * jnp.maximum(rel[:, j * DSA_TQ:(j + 1) * DSA_TQ], 0.0)
    score = jnp.where(causal, score, -jnp.inf)

    def key_to_float(k):
        k = jnp.maximum(k, NEG_INF_KEY)
        return lax.bitcast_convert_type(k ^ ((k >> 31) & jnp.int32(0x7FFFFFFF)), f32)

    n_nonneg = _col_count(score >= 0.0)
    start_high = n_nonneg >= top_k
    thr = jnp.where(start_high, jnp.int32(0), jnp.int32(INT_MIN))
    n_ge = jnp.where(start_high, n_nonneg, jnp.int32(width))

    def value_step(i, state):
        thr, n_ge = state
        cand = thr | (jnp.int32(1) << (30 - i))
        cnt = _col_count(score >= key_to_float(cand))
        take = cnt >= top_k
        return jnp.where(take, cand, thr), jnp.where(take, cnt, n_ge)

    thr, n_ge = lax.fori_loop(0, 31, value_step, (thr, n_ge))

    thr_f = key_to_float(thr)
    bias_ref[0:width, :] = jnp.where((score >= thr_f) & causal, 0.0, -jnp.inf)
    overfull = (n_ge != top_k) & (thr > NEG_INF_KEY)

    @pl.when(jnp.max(overfull.astype(f32)) > 0.0)
    def _():
        real_thr = thr > NEG_INF_KEY
        next_f = key_to_float(thr + 1)
        beyond = score >= next_f
        in_bin = (score >= thr_f) & jnp.logical_not(beyond) & real_thr
        offset = jnp.where(in_bin, score - jnp.where(real_thr, thr_f, 0.0), -1.0)
        need = top_k - _col_count(beyond)

        okey_ref[...] = jnp.zeros(okey_ref.shape, jnp.int32)

        @pl.when(jnp.max(offset) > 0.0)
        def _():
            def offset_step(i, okey):
                cand = okey | (jnp.int32(1) << (30 - i))
                cnt = _col_count(offset >= lax.bitcast_convert_type(cand, f32))
                return jnp.where(cnt >= need, cand, okey)

            okey = lax.fori_loop(0, 31, offset_step, jnp.zeros((1, DSA_TQ), jnp.int32))
            okey_ref[...] = jnp.broadcast_to(okey, okey_ref.shape)

        othr = lax.bitcast_convert_type(okey_ref[0:1, :], f32)
        above = offset > othr
        tied = offset == othr
        need = need - _col_count(above)
        nbits = (width - 1).bit_length()

        def index_step(i, pos):
            cand = pos | (jnp.int32(1) << (nbits - 1 - i))
            return jnp.where(_col_count(tied & (row_s < cand)) < need, cand, pos)

        pos = lax.fori_loop(0, nbits, index_step, jnp.zeros((1, DSA_TQ), jnp.int32))
        keep = (beyond | above | (tied & (row_s <= pos))) & causal
        bias_ref[0:width, :] = jnp.where(keep, 0.0, -jnp.inf)


def _dsa_attn_body(q_ref, qi_ref, wit_ref, ki_ref, k_ref, vt_ref, o_ref,
                   okey_ref, bias_ref, logit_ref, *, width, top_k, t0, select):
    col_t = t0 + lax.broadcasted_iota(jnp.int32, (1, DSA_TQ), 1)
    row_s = lax.broadcasted_iota(jnp.int32, (width, 1), 0)
    causal = row_s <= col_t
    if select:
        _write_topk_bias(qi_ref, wit_ref, ki_ref, okey_ref, bias_ref, causal, row_s,
                         width=width, top_k=top_k)
    else:
        bias_ref[0:width, :] = jnp.where(causal, 0.0, -jnp.inf)

    group = ATT_HEADS // ATT_KV_HEADS
    rows = min(DSA_ROWS, width)
    for g in range(ATT_KV_HEADS):
        rq = _stack_heads(q_ref, g * group, group)
        maxes = [None] * group
        for rc in range(0, width, rows):
            logits4 = lax.dot_general(k_ref[0, rc:rc + rows, g * LANES:(g + 1) * LANES], rq, NT,
                                      preferred_element_type=f32)
            b = bias_ref[rc:rc + rows, :]
            for j in range(group):
                logits = logits4[:, j * DSA_TQ:(j + 1) * DSA_TQ] + b
                logit_ref[j, rc:rc + rows, :] = logits
                m = _fold_rows(logits, jnp.maximum)
                maxes[j] = m if maxes[j] is None else jnp.maximum(maxes[j], m)
        outs = []
        for j in range(group):
            mx = jnp.max(maxes[j], axis=0, keepdims=True)
            pv = None
            for rc in range(0, width, rows):
                p = jnp.exp2(logit_ref[j, rc:rc + rows, :] - mx).astype(bf16)
                part = jnp.dot(vt_ref[0, g * VT_ROWS:(g + 1) * VT_ROWS, rc:rc + rows], p,
                               preferred_element_type=f32)
                pv = part if pv is None else pv + part
            outs.append(pv[:HEAD_DIM] * (1.0 / pv[HEAD_DIM:HEAD_DIM + 1]))
        for pair in range(group // 2):
            both = jnp.concatenate(outs[2 * pair:2 * pair + 2], axis=0).T
            col = (g * group // 2 + pair) * LANES
            o_ref[:, col:col + LANES] = both.astype(bf16)


def _dsa_attn_kernel(q_ref, qi_ref, wit_ref, ki_ref, k_ref, vt_ref, o_ref,
                     okey_ref, bias_ref, logit_ref, *, seq, top_k):
    i = pl.program_id(1)
    refs = (q_ref, qi_ref, wit_ref, ki_ref, k_ref, vt_ref, o_ref, okey_ref, bias_ref, logit_ref)
    dense_blocks = top_k // DSA_TQ

    @pl.when(i < dense_blocks)
    def _():
        _dsa_attn_body(*refs, width=dense_blocks * DSA_TQ, top_k=top_k, t0=i * DSA_TQ, select=False)

    blocks_per_class = DSA_CLS // DSA_TQ
    for cls in range(seq // DSA_CLS):
        @pl.when((i >= dense_blocks) & (i // blocks_per_class == cls))
        def _(cls=cls):
            _dsa_attn_body(*refs, width=DSA_CLS * (cls + 1), top_k=top_k, t0=i * DSA_TQ, select=True)


def _dsa_attn(q, qi, wit, ki, k, vt, batch, seq):
    nq = seq // DSA_TQ
    top_k = min(TOPK_MAX, seq // 4)
    qrow = lambda n: pl.BlockSpec((DSA_TQ, n), lambda b, i: (b * nq + i, 0))
    full = lambda n: pl.BlockSpec((1, seq, n), lambda b, i: (b, 0, 0))
    return pl.pallas_call(
        functools.partial(_dsa_attn_kernel, seq=seq, top_k=top_k),
        grid=(batch, nq),
        in_specs=[qrow(ATT_HEADS * LANES), qrow(IDX_HEADS * LANES),
                  pl.BlockSpec((8, DSA_TQ), lambda b, i: (0, b * nq + i)),
                  full(LANES), full(ATT_KV_HEADS * LANES),
                  pl.BlockSpec((1, ATT_KV_HEADS * VT_ROWS, seq), lambda b, i: (b, 0, 0))],
        out_specs=qrow(ATT_HEADS * HEAD_DIM),
        out_shape=jax.ShapeDtypeStruct((batch * seq, ATT_HEADS * HEAD_DIM), bf16),
        scratch_shapes=[pltpu.VMEM((8, DSA_TQ), jnp.int32),
                        pltpu.VMEM((seq, DSA_TQ), f32),
                        pltpu.VMEM((ATT_HEADS // ATT_KV_HEADS, seq, DSA_TQ), f32)],
        compiler_params=_cparams("parallel", "arbitrary"),
        name="dsa_attn",
    )(q, qi, wit, ki, k, vt)


RW_TM = 512
LORA_PAD = 128
GATE_PAD = 256


def _rwkv_pre_kernel(x_ref, xp_ref, g_ref, mu_ref, wr_ref, wk_ref, wv_ref,
                     w1_ref, w2_ref, a1_ref, a2_ref, g1_ref, g2_ref,
                     w0_ref, a0_ref, kk_ref, ka_ref, rk_ref, tri_ref, hred_ref, hexp_ref,
                     at_ref, bt_ref, kt_ref, rt_ref, v_ref, gc_ref, bonus_ref, gate_ref,
                     lc_scr, *, seq):
    i = pl.program_id(0)
    g = g_ref[...]
    h = _rms(x_ref[...], g)
    hp = _rms(xp_ref[...], g)[7:8, :]
    hp = jnp.where((i * RW_TM) % seq == 0, 0.0, hp)
    rowi = lax.broadcasted_iota(jnp.int32, (RW_TM, 1), 0)
    hs = jnp.where(rowi == 0, hp, pltpu.roll(h, 1, 0))
    xx = hs - h
    mu = mu_ref[...]

    def mix(c):
        return (h + xx * mu[c:c + 1, :]).astype(bf16)

    r = jnp.dot(mix(0), wr_ref[...], preferred_element_type=f32)
    k = jnp.dot(mix(1), wk_ref[...], preferred_element_type=f32)
    v = jnp.dot(mix(2), wv_ref[...], preferred_element_type=f32)

    wl = w0_ref[...] + _bdot(jnp.tanh(jnp.dot(mix(3), w1_ref[...], preferred_element_type=f32)),
                             w2_ref[...])
    nwl = -wl
    softplus = jnp.maximum(nwl, 0.0) + jnp.log1p(jnp.exp(-jnp.abs(nwl)))
    ld = -jnp.exp(-softplus - 0.5)
    a = jax.nn.sigmoid(a0_ref[...] + _bdot(jnp.dot(mix(4), a1_ref[...],
                                                   preferred_element_type=f32), a2_ref[...]))
    gate_ref[...] = _bdot(jax.nn.sigmoid(jnp.dot(mix(5), g1_ref[...],
                                                 preferred_element_type=f32)), g2_ref[...]).astype(bf16)

    hred = hred_ref[...]
    hexp = hexp_ref[...]
    z = k * kk_ref[...]
    kk = z * lax.rsqrt(jnp.maximum(_head_sum(z * z, hred, hexp), 1e-24))
    k2 = k * (1.0 + (a - 1.0) * ka_ref[...])
    bonus_ref[...] = (_head_sum(r * k2 * rk_ref[...], hred, hexp) * v).astype(bf16)

    lc = _split_dot_left(tri_ref[...], ld, 2)
    lc_scr[...] = lc
    for c in range(RW_TM // CHUNK):
        last = lc_scr[c * CHUNK + CHUNK - 1:c * CHUNK + CHUNK, :]
        gc_ref[c * 8:(c + 1) * 8, :] = jnp.broadcast_to(jnp.exp(last), (8, D_MODEL))
    einv = jnp.exp(-lc)
    at_ref[...] = -kk * jnp.exp(lc - ld)
    bt_ref[...] = kk * a * einv
    kt_ref[...] = k2 * einv
    rt_ref[...] = r * jnp.exp(lc)
    v_ref[...] = v


def _split_dot_left(a, b, parts):
    out = None
    rem = b
    for p in range(parts):
        piece = rem.astype(bf16)
        term = jnp.dot(a, piece, preferred_element_type=f32)
        out = term if out is None else out + term
        if p + 1 < parts:
            rem = rem - piece.astype(f32)
    return out


def _rwkv_pre(x2, g, mu8, wr, wk, wv, w1, w2, a1, a2, g1, g2, w0, a0, k_k, k_a, r_k, tri, hred, hexp,
              seq):
    m = x2.shape[0]
    row = pl.BlockSpec((RW_TM, D_MODEL), lambda i: (i, 0))
    prev = pl.BlockSpec((8, D_MODEL), lambda i: (jnp.maximum(i * (RW_TM // 8) - 1, 0), 0))
    vec = _const_spec((1, D_MODEL))
    gcrow = pl.BlockSpec((RW_TM // CHUNK * 8, D_MODEL), lambda i: (i, 0))
    act = jax.ShapeDtypeStruct((m, D_MODEL), f32)
    return pl.pallas_call(
        functools.partial(_rwkv_pre_kernel, seq=seq),
        grid=(m // RW_TM,),
        in_specs=[row, prev, vec, _const_spec((8, D_MODEL)),
                  _const_spec(wr.shape), _const_spec(wk.shape), _const_spec(wv.shape),
                  _const_spec(w1.shape), _const_spec(w2.shape),
                  _const_spec(a1.shape), _const_spec(a2.shape),
                  _const_spec(g1.shape), _const_spec(g2.shape),
                  vec, vec, vec, vec, vec,
                  _const_spec(tri.shape), _const_spec(hred.shape), _const_spec(hexp.shape)],
        out_specs=[row, row, row, row, row, gcrow, row, row],
        out_shape=[act, act, act, act, act,
                   jax.ShapeDtypeStruct((m // CHUNK * 8, D_MODEL), f32),
                   jax.ShapeDtypeStruct((m, D_MODEL), bf16), jax.ShapeDtypeStruct((m, D_MODEL), bf16)],
        scratch_shapes=[pltpu.VMEM((RW_TM, D_MODEL), f32)],
        compiler_params=_cparams("parallel"),
        name="rwkv_pre",
    )(x2, x2, g, mu8, wr, wk, wv, w1, w2, a1, a2, g1, g2, w0, a0, k_k, k_a, r_k, tri, hred, hexp)


RA_GROUP = 2 * CHUNK
RA_ROWS = 1024


def _rwkv_chunk_kernel(at_ref, bt_ref, kt_ref, rt_ref, v_ref, gc_ref,
                       q_ref, yl_ref, m_ref, g_ref):
    n = RA_GROUP
    ri = lax.broadcasted_iota(jnp.int32, (n, n), 0)
    ci = lax.broadcasted_iota(jnp.int32, (n, n), 1)
    same = (ri // CHUNK) == (ci // CHUNK)
    strict = same & (ci < ri)
    incl = same & (ci <= ri)
    eye = (ri == ci).astype(f32)
    lane = lax.broadcasted_iota(jnp.int32, (1, LANES), 1)
    low = lane < RWKV_HEAD
    rk = lax.broadcasted_iota(jnp.int32, (RWKV_HEAD, LANES), 0)
    diag = rk == (lax.broadcasted_iota(jnp.int32, (RWKV_HEAD, LANES), 1) % RWKV_HEAD)

    groups = []
    for gi in range(RA_ROWS // n):
        rows = slice(gi * n, (gi + 1) * n)
        at, bt, kt, rt, v = (r[rows, :] for r in (at_ref, bt_ref, kt_ref, rt_ref, v_ref))
        groups.append(dict(at=at, bt=bt, kt=kt, rt=rt, v=v, vb=v.astype(bf16),
                           ar=jnp.concatenate([at, rt], axis=0),
                           bk=jnp.concatenate([bt, kt], axis=0).astype(bf16)))
    probs = [(gr, hl) for gr in groups for hl in (low, jnp.logical_not(low))]

    gs = [lax.dot_general(jnp.where(hl, gr["ar"], 0.0).astype(bf16), gr["bk"], NT,
                          preferred_element_type=f32) for gr, hl in probs]
    aab = [jnp.where(strict, g[:n, :n], 0.0) for g in gs]
    aak = [jnp.where(strict, g[:n, n:], 0.0) for g in gs]
    arb = [jnp.where(incl, g[n:, :n], 0.0) for g in gs]
    ark = [jnp.where(incl, g[n:, n:], 0.0) for g in gs]

    ts = [eye + a for a in aab]
    ps = aab
    for _ in range((CHUNK - 1).bit_length() - 1):
        ps = [_bdot(p, p) for p in ps]
        ts = [t + _bdot(t, p) for t, p in zip(ts, ps)]

    akv = [_bdot(a, gr["vb"]) for a, (gr, _) in zip(aak, probs)]
    tw = [_bdot(t, jnp.concatenate([gr["at"], x], axis=1))
          for t, x, (gr, _) in zip(ts, akv, probs)]
    qy = [_bdot(a, w) for a, w in zip(arb, tw)]
    rkv = [_bdot(a, gr["vb"]) for a, (gr, _) in zip(ark, probs)]

    for gi, gr in enumerate(groups):
        lo, hi = 2 * gi, 2 * gi + 1
        w_hat = jnp.where(low, tw[lo][:, :LANES], tw[hi][:, :LANES])
        u_loc = jnp.where(low, tw[lo][:, LANES:], tw[hi][:, LANES:])
        base = gi * n
        q_ref[base:base + n, :] = (gr["rt"] + jnp.where(low, qy[lo][:, :LANES], qy[hi][:, :LANES])
                                   ).astype(bf16)
        yl_ref[base:base + n, :] = (jnp.where(low, qy[lo][:, LANES:], qy[hi][:, LANES:])
                                    + jnp.where(low, rkv[lo], rkv[hi]))
        for c in range(n // CHUNK):
            rows = slice(c * CHUNK, (c + 1) * CHUNK)
            ch = base // CHUNK + c
            gc = gc_ref[ch * 8:ch * 8 + 1, :]
            bh = gr["bt"][rows] * gc
            kh = gr["kt"][rows] * gc
            pm = _bdot(w_hat[rows], bh, TN)
            pg = _bdot(jnp.concatenate([u_loc[rows], gr["v"][rows]], axis=0),
                       jnp.concatenate([bh, kh], axis=0), TN)
            out = slice(base + c * CHUNK, base + (c + 1) * CHUNK)
            m_ref[out, :] = (jnp.where(low, pm[:RWKV_HEAD], pm[RWKV_HEAD:])
                             + jnp.where(diag, gc, 0.0)).astype(bf16)
            g_ref[out, :] = jnp.where(low, pg[:RWKV_HEAD], pg[RWKV_HEAD:])


def _rwkv_chunk(at, bt, kt, rt, v, gc):
    m = at.shape[0]
    blk = pl.BlockSpec((RA_ROWS, LANES), lambda i, p: (i, p))
    gblk = pl.BlockSpec((RA_ROWS // CHUNK * 8, LANES), lambda i, p: (i, p))
    act = jax.ShapeDtypeStruct((m, D_MODEL), f32)
    half = jax.ShapeDtypeStruct((m, D_MODEL), bf16)
    return pl.pallas_call(
        _rwkv_chunk_kernel,
        grid=(m // RA_ROWS, D_MODEL // LANES),
        in_specs=[blk, blk, blk, blk, blk, gblk],
        out_specs=[blk, blk, blk, blk],
        out_shape=[half, act, half, act],
        compiler_params=_cparams("parallel", "parallel"),
        name="rwkv_chunk",
    )(at, bt, kt, rt, v, gc)


SCAN_LANES = D_MODEL
SCAN_SPLIT = 2


def _rwkv_scan_kernel(q_ref, yl_ref, m_ref, g_ref, y_ref, state_ref, *, seq):
    pairs = SCAN_LANES // LANES
    low = lax.broadcasted_iota(jnp.int32, (1, LANES), 1) < RWKV_HEAD

    def blockdiag(x):
        return jnp.concatenate([jnp.where(low, x, 0.0), jnp.where(low, 0.0, x)], axis=0)

    def body(c, states):
        rows = pl.ds(pl.multiple_of(c * CHUNK, CHUNK), CHUNK)
        new_states = []
        for p in range(pairs):
            cols = slice(p * LANES, (p + 1) * LANES)
            s = states[p]
            mc = blockdiag(m_ref[rows, cols]).astype(bf16)
            s_hi = s.astype(bf16)
            s_lo = (s - s_hi.astype(f32)).astype(bf16)
            sm = (jnp.dot(s_hi, mc, preferred_element_type=f32)
                  + jnp.dot(s_lo, mc, preferred_element_type=f32))
            new_states.append(sm + g_ref[rows, cols])
        for p in range(pairs):
            cols = slice(p * LANES, (p + 1) * LANES)
            y_ref[rows, cols] = (yl_ref[rows, cols]
                                 + _bdot(q_ref[rows, cols], blockdiag(states[p]), NT))
        return tuple(new_states)

    @pl.when(pl.program_id(1) == 0)
    def _():
        state_ref[...] = jnp.zeros(state_ref.shape, f32)

    init = tuple(state_ref[p] for p in range(pairs))
    final = lax.fori_loop(0, seq // SCAN_SPLIT // CHUNK, body, init)
    for p in range(pairs):
        state_ref[p] = final[p]


def _rwkv_scan(q, yl, mm, gg, batch, seq):
    rows = seq // SCAN_SPLIT
    blk = pl.BlockSpec((rows, SCAN_LANES), lambda b, s: (b * SCAN_SPLIT + s, 0))
    return pl.pallas_call(
        functools.partial(_rwkv_scan_kernel, seq=seq),
        grid=(batch, SCAN_SPLIT),
        in_specs=[blk, blk, blk, blk],
        out_specs=blk,
        out_shape=jax.ShapeDtypeStruct((batch * seq, D_MODEL), f32),
        scratch_shapes=[pltpu.VMEM((SCAN_LANES // LANES, RWKV_HEAD, LANES), f32)],
        compiler_params=_cparams("parallel", "arbitrary"),
        name="rwkv_scan",
    )(q, yl, mm, gg)


def _dup_heads(w, n_heads):
    w = w.reshape(w.shape[0], n_heads, 1, HEAD_DIM)
    return jnp.broadcast_to(w, (w.shape[0], n_heads, 2, HEAD_DIM)).reshape(w.shape[0], n_heads * LANES)


def _pad_cols(w, n):
    return jnp.pad(w, ((0, 0), (0, n - w.shape[1])))


def _pad_rows(w, n):
    return jnp.pad(w, ((0, n - w.shape[0]), (0, 0)))


def _rope_tables(seq):
    inv = 1.0 / (ROPE_THETA ** (jnp.arange(0, HEAD_DIM, 2, dtype=f32) / HEAD_DIM))
    ang = jnp.arange(seq, dtype=f32)[:, None] * inv[None, :]
    cos, sin = jnp.cos(ang), jnp.sin(ang)
    return jnp.tile(jnp.concatenate([cos, cos], 1), (1, 2)), jnp.tile(jnp.concatenate([-sin, sin], 1), (1, 2))


def _dsa_layer(x2, g, w_in, w_o, batch, seq):
    o0 = ATT_HEADS * HEAD_DIM
    o1 = o0 + ATT_KV_HEADS * HEAD_DIM
    o2 = o1 + ATT_KV_HEADS * HEAD_DIM
    o3 = o2 + IDX_HEADS * HEAD_DIM
    o4 = o3 + HEAD_DIM
    w_in = w_in.astype(bf16)
    wq = w_in[:, :o0]
    wk2 = _dup_heads(w_in[:, o0:o1], ATT_KV_HEADS)
    wv = w_in[:, o1:o2].T
    wqi = w_in[:, o2:o3]
    wki2 = _dup_heads(w_in[:, o3:o4], 1)
    wwi = _pad_cols(w_in[:, o4:], LANES).T
    cos, sin = _rope_tables(seq)
    q, k, vt, qi, ki, wit = _dsa_proj(x2, g, wq, wk2, wv, wqi, wki2, wwi, cos, sin, batch, seq)
    o = _dsa_attn(q, qi, wit, ki, k, vt, batch, seq)
    return _attn_mlp_kernel, "attn_mlp", [o, x2], [w_o.astype(bf16)]


def _rwkv_layer(x2, g, mu, w_rkv, w0, w1, w2, a0, a1, a2, g1, g2, k_k, k_a, r_k, lnx_w, lnx_b, w_o,
                batch, seq):
    vec = lambda p: p.reshape(1, D_MODEL)
    w_rkv = w_rkv.astype(bf16)
    ri = jnp.arange(RW_TM)
    tri = ((ri[:, None] >= ri[None, :]) & (ri[:, None] // CHUNK == ri[None, :] // CHUNK)).astype(bf16)
    di = jnp.arange(D_MODEL)
    hred = (di[:, None] // RWKV_HEAD == jnp.arange(LANES)[None, :]).astype(bf16)
    hexp = jnp.concatenate([hred.T, hred.T], axis=0)
    at, bt, kt, rt, v, gc, bonus, gate = _rwkv_pre(
        x2, g, _pad_rows(mu, 8), w_rkv[0], w_rkv[1], w_rkv[2],
        _pad_cols(w1, LORA_PAD).astype(bf16), _pad_rows(w2, LORA_PAD).astype(bf16),
        _pad_cols(a1, LORA_PAD).astype(bf16), _pad_rows(a2, LORA_PAD).astype(bf16),
        _pad_cols(g1, GATE_PAD).astype(bf16), _pad_rows(g2, GATE_PAD).astype(bf16),
        vec(w0), vec(a0), vec(k_k), vec(k_a), vec(r_k), tri, hred, hexp, seq)
    q, yl, mm, gg = _rwkv_chunk(at, bt, kt, rt, v, gc)
    y = _rwkv_scan(q, yl, mm, gg, batch, seq)
    return (_rwkv_mlp_kernel, "rwkv_mlp", [y, bonus, gate, x2],
            [vec(lnx_w), vec(lnx_b), hred, hexp, w_o.astype(bf16)])


def kernel(x, mixer_norm, mlp_norm, mlp_w_up, mlp_w_down, final_norm, dsa_w_in, dsa_w_o, rwkv_mu, rwkv_w_rkv, rwkv_w0, rwkv_w1, rwkv_w2, rwkv_a0, rwkv_a1, rwkv_a2, rwkv_g1, rwkv_g2, rwkv_k_k, rwkv_k_a, rwkv_r_k, rwkv_lnx_w, rwkv_lnx_b, rwkv_w_o):
    batch, seq, _ = x.shape
    depth = mixer_norm.shape[0]
    x2 = x.reshape(batch * seq, D_MODEL)
    fg = final_norm.reshape(1, D_MODEL)
    w_up = mlp_w_up.astype(bf16)
    w_down = mlp_w_down.astype(bf16)
    for i in range(depth):
        g = mixer_norm[i].reshape(1, D_MODEL)
        j = i // 2
        if i % 2 == 0:
            tail = _dsa_layer(x2, g, dsa_w_in[j], dsa_w_o[j], batch, seq)
        else:
            tail = _rwkv_layer(x2, g, rwkv_mu[j], rwkv_w_rkv[j], rwkv_w0[j], rwkv_w1[j], rwkv_w2[j],
                               rwkv_a0[j], rwkv_a1[j], rwkv_a2[j], rwkv_g1[j], rwkv_g2[j],
                               rwkv_k_k[j], rwkv_k_a[j], rwkv_r_k[j], rwkv_lnx_w[j], rwkv_lnx_b[j],
                               rwkv_w_o[j], batch, seq)
        x2 = _mixer_mlp(*tail, mlp_norm[i].reshape(1, D_MODEL), w_up, w_down, i, fg,
                        final=(i == depth - 1))
    return x2.reshape(batch, seq, D_MODEL)
```

```python
import functools

import jax
import jax.numpy as jnp
from jax import lax
from jax.experimental import pallas as pl
from jax.experimental.pallas import tpu as pltpu

f32 = jnp.float32
bf16 = jnp.bfloat16

D_MODEL = 1024
D_FF = 4 * D_MODEL
NORM_EPS = 1e-6

ATT_HEADS = 16
ATT_KV_HEADS = 4
HEAD_DIM = 64
IDX_HEADS = 8
TOPK_MAX = 256
ROPE_THETA = 10000.0

RWKV_HEAD = 64
LNX_EPS = 64e-5
CHUNK = 64

LANES = 128
VMEM_LIMIT = 56 * 1024 * 1024

NT = (((1,), (1,)), ((), ()))
TN = (((0,), (0,)), ((), ()))


def _cparams(*sem):
    return pltpu.CompilerParams(dimension_semantics=sem, vmem_limit_bytes=VMEM_LIMIT)


def _rms(x, g):
    return x * lax.rsqrt(jnp.mean(x * x, axis=-1, keepdims=True) + NORM_EPS) * g


def _bdot(a, b, dims=None):
    a = a.astype(bf16)
    b = b.astype(bf16)
    if dims is None:
        return jnp.dot(a, b, preferred_element_type=f32)
    return lax.dot_general(a, b, dims, preferred_element_type=f32)


def _head_sum(x, hred, hexp):
    red = jnp.dot(x.astype(bf16), hred, preferred_element_type=f32)
    hi = red.astype(bf16)
    lo = (red - hi.astype(f32)).astype(bf16)
    return jnp.dot(jnp.concatenate([hi, lo], axis=1), hexp, preferred_element_type=f32)


def _const_spec(shape):
    return pl.BlockSpec(shape, lambda *_: (0,) * len(shape))


MLP_TM = 512
MLP_FCH = 512


def _mlp_tail(x, g_ref, wu_ref, wd_ref, fg_ref, o_ref, final):
    xn = _rms(x, g_ref[...]).astype(bf16)
    o_ref[...] = x
    for f in range(0, D_FF, MLP_FCH):
        u = jnp.dot(xn, wu_ref[:, f:f + MLP_FCH], preferred_element_type=f32)
        u = jnp.maximum(u, 0.0)
        o_ref[...] += jnp.dot((u * u).astype(bf16), wd_ref[f:f + MLP_FCH, :],
                              preferred_element_type=f32)
    if final:
        o_ref[...] = _rms(o_ref[...], fg_ref[...])


def _attn_mlp_kernel(a_ref, x_ref, wo_ref, g_ref, wu_ref, wd_ref, fg_ref, o_ref, *, final):
    x = x_ref[...] + jnp.dot(a_ref[...], wo_ref[...], preferred_element_type=f32)
    _mlp_tail(x, g_ref, wu_ref, wd_ref, fg_ref, o_ref, final)


def _rwkv_mlp_kernel(y_ref, bonus_ref, gate_ref, x_ref, lw_ref, lb_ref, hred_ref, hexp_ref, wo_ref,
                     g_ref, wu_ref, wd_ref, fg_ref, o_ref, *, final):
    y = y_ref[...]
    hred = hred_ref[...]
    hexp = hexp_ref[...]
    mean = _head_sum(y, hred, hexp) * (1.0 / RWKV_HEAD)
    yc = y - mean
    var = _head_sum(yc * yc, hred, hexp) * (1.0 / RWKV_HEAD)
    yn = yc * lax.rsqrt(var + LNX_EPS) * lw_ref[...] + lb_ref[...]
    out = (yn + bonus_ref[...]) * gate_ref[...]
    x = x_ref[...] + jnp.dot(out.astype(bf16), wo_ref[...], preferred_element_type=f32)
    _mlp_tail(x, g_ref, wu_ref, wd_ref, fg_ref, o_ref, final)


def _resident(shape):
    return pl.BlockSpec(shape, lambda *_: (0,) * len(shape), pipeline_mode=pl.Buffered(1))


def _resident_layer(shape, layer):
    return pl.BlockSpec((None,) + tuple(shape[1:]), lambda *_: (layer,) + (0,) * (len(shape) - 1),
                        pipeline_mode=pl.Buffered(1))


def _mixer_mlp(kernel, name, rows, consts, g, wu_all, wd_all, layer, fg, final):
    m = rows[0].shape[0]
    row = pl.BlockSpec((MLP_TM, D_MODEL), lambda i: (i, 0))
    operands = list(rows) + list(consts) + [g, wu_all, wd_all, fg]
    return pl.pallas_call(
        functools.partial(kernel, final=final),
        grid=(m // MLP_TM,),
        in_specs=([row] * len(rows) + [_resident(c.shape) for c in list(consts) + [g]]
                  + [_resident_layer(wu_all.shape, layer), _resident_layer(wd_all.shape, layer),
                     _resident(fg.shape)]),
        out_specs=row,
        out_shape=jax.ShapeDtypeStruct((m, D_MODEL), f32),
        compiler_params=_cparams("parallel"),
        name=name,
    )(*operands)


DSA_TM = 512


def _rope(x, cos, sin_signed, first_half):
    fwd = pltpu.roll(x, 32, 1)
    bwd = pltpu.roll(x, 96, 1)
    return x * cos + jnp.where(first_half, bwd, fwd) * sin_signed


LOG2E = 1.4426950408889634


VT_ROWS = 80


def _dsa_proj_kernel(x_ref, g_ref, wq_ref, wk_ref, wv_ref, wqi_ref, wki_ref, wwi_ref,
                     cos_ref, sin_ref,
                     q_ref, k_ref, vt_ref, qi_ref, ki_ref, wit_ref):
    h = _rms(x_ref[...], g_ref[...]).astype(bf16)
    tm = h.shape[0]
    cos = cos_ref[...]
    sin = sin_ref[...]
    lane = lax.broadcasted_iota(jnp.int32, (1, LANES), 1)
    first_half = (lane % HEAD_DIM) < (HEAD_DIM // 2)
    low = lane < HEAD_DIM

    def per_head(x, out_ref, pair, scale):
        xc = _rope(x[:, pair * LANES:(pair + 1) * LANES], cos, sin, first_half) * scale
        swapped = pltpu.roll(xc, HEAD_DIM, 1)
        out_ref[:, (2 * pair) * LANES:(2 * pair + 1) * LANES] = jnp.where(low, xc, 0.0).astype(bf16)
        out_ref[:, (2 * pair + 1) * LANES:(2 * pair + 2) * LANES] = jnp.where(low, swapped, 0.0).astype(bf16)

    q = jnp.dot(h, wq_ref[...], preferred_element_type=f32)
    for pair in range(ATT_HEADS // 2):
        per_head(q, q_ref, pair, HEAD_DIM ** -0.5 * LOG2E)
    qi = jnp.dot(h, wqi_ref[...], preferred_element_type=f32)
    for pair in range(IDX_HEADS // 2):
        per_head(qi, qi_ref, pair, 1.0)

    k2 = jnp.dot(h, wk_ref[...], preferred_element_type=f32)
    for c in range(0, ATT_KV_HEADS * LANES, LANES):
        kc = _rope(k2[:, c:c + LANES], cos, sin, first_half)
        k_ref[0, :, c:c + LANES] = jnp.where(low, kc, 0.0).astype(bf16)
    ki2 = _rope(jnp.dot(h, wki_ref[...], preferred_element_type=f32), cos, sin, first_half)
    ki_ref[0] = jnp.where(low, ki2, 0.0).astype(bf16)

    vt = lax.dot_general(wv_ref[...], h, NT, preferred_element_type=f32)
    extra = (lax.broadcasted_iota(jnp.int32, (VT_ROWS - HEAD_DIM, tm), 0) == 0).astype(f32)
    for g in range(ATT_KV_HEADS):
        vt_ref[0, g * VT_ROWS:(g + 1) * VT_ROWS, :] = jnp.concatenate(
            [vt[g * HEAD_DIM:(g + 1) * HEAD_DIM], extra], axis=0).astype(bf16)
    wit_ref[...] = (lax.dot_general(wwi_ref[...], h, NT, preferred_element_type=f32)
                    * (IDX_HEADS ** -0.5 * HEAD_DIM ** -0.5))


def _dsa_proj(x2, g, wq, wk2, wv, wqi, wki2, wwi, cos, sin, batch, seq):
    m = x2.shape[0]
    nblk = seq // DSA_TM
    row = lambda n: pl.BlockSpec((DSA_TM, n), lambda b, i: (b * nblk + i, 0))
    seqrow = lambda n: pl.BlockSpec((1, DSA_TM, n), lambda b, i: (b, i, 0))
    tab = pl.BlockSpec((DSA_TM, LANES), lambda b, i: (i, 0))
    kvw = ATT_KV_HEADS * LANES
    vtr = ATT_KV_HEADS * VT_ROWS
    return pl.pallas_call(
        _dsa_proj_kernel,
        grid=(batch, nblk),
        in_specs=[row(D_MODEL), _const_spec((1, D_MODEL)),
                  _const_spec(wq.shape), _const_spec(wk2.shape), _const_spec(wv.shape),
                  _const_spec(wqi.shape), _const_spec(wki2.shape), _const_spec(wwi.shape),
                  tab, tab],
        out_specs=[row(ATT_HEADS * LANES), seqrow(kvw),
                   pl.BlockSpec((1, vtr, DSA_TM), lambda b, i: (b, 0, i)),
                   row(IDX_HEADS * LANES), seqrow(LANES),
                   pl.BlockSpec((LANES, DSA_TM), lambda b, i: (0, b * nblk + i))],
        out_shape=[jax.ShapeDtypeStruct((m, ATT_HEADS * LANES), bf16),
                   jax.ShapeDtypeStruct((batch, seq, kvw), bf16),
                   jax.ShapeDtypeStruct((batch, vtr, seq), bf16),
                   jax.ShapeDtypeStruct((m, IDX_HEADS * LANES), bf16),
                   jax.ShapeDtypeStruct((batch, seq, LANES), bf16),
                   jax.ShapeDtypeStruct((LANES, m), f32)],
        compiler_params=_cparams("parallel", "parallel"),
        name="dsa_proj",
    )(x2, g, wq, wk2, wv, wqi, wki2, wwi, cos, sin)


DSA_TQ = 128
DSA_CLS = 512
IDX_GROUP = 4
DSA_ROWS = 256
INT_MIN = -2 ** 31
NEG_INF_KEY = -2139095041


FOLD_ROWS = 64


def _fold_rows(x, op):
    rows = x.shape[0]
    if rows > FOLD_ROWS and rows % FOLD_ROWS == 0:
        acc = x[:FOLD_ROWS]
        for r in range(FOLD_ROWS, rows, FOLD_ROWS):
            acc = op(acc, x[r:r + FOLD_ROWS])
        x = acc
    while x.shape[0] % 16 == 0:
        h = x.shape[0] // 2
        x = op(x[:h], x[h:])
    return x


def _col_count(mask):
    return jnp.sum(_fold_rows(jnp.where(mask, 1.0, 0.0), jnp.add), axis=0, keepdims=True).astype(jnp.int32)


def _stack_heads(ref, first, count):
    return jnp.concatenate([ref[:, (first + j) * LANES:(first + j + 1) * LANES] for j in range(count)],
                           axis=0)


def _write_topk_bias(qi_ref, wit_ref, ki_ref, okey_ref, bias_ref, causal, row_s, *, width, top_k):
    wit = wit_ref[...]
    ki = ki_ref[0, 0:width, :]
    score = jnp.zeros((width, DSA_TQ), f32)
    for first in range(0, IDX_HEADS, IDX_GROUP):
        rel = lax.dot_general(ki, _stack_heads(qi_ref, first, IDX_GROUP), NT,
                              preferred_element_type=f32)
        for j in range(IDX_GROUP):
            hd = first + j
            score = score + wit[hd:hd + 1, :] * jnp.maximum(rel[:, j * DSA_TQ:(j + 1) * DSA_TQ], 0.0)
    score = jnp.where(causal, score, -jnp.inf)

    def key_to_float(k):
        k = jnp.maximum(k, NEG_INF_KEY)
        return lax.bitcast_convert_type(k ^ ((k >> 31) & jnp.int32(0x7FFFFFFF)), f32)

    n_nonneg = _col_count(score >= 0.0)
    start_high = n_nonneg >= top_k
    thr = jnp.where(start_high, jnp.int32(0), jnp.int32(INT_MIN))
    n_ge = jnp.where(start_high, n_nonneg, jnp.int32(width))

    def value_step(i, state):
        thr, n_ge = state
        cand = thr | (jnp.int32(1) << (30 - i))
        cnt = _col_count(score >= key_to_float(cand))
        take = cnt >= top_k
        return jnp.where(take, cand, thr), jnp.where(take, cnt, n_ge)

    thr, n_ge = lax.fori_loop(0, 31, value_step, (thr, n_ge))

    thr_f = key_to_float(thr)
    bias_ref[0:width, :] = jnp.where((score >= thr_f) & causal, 0.0, -jnp.inf)
    overfull = (n_ge != top_k) & (thr > NEG_INF_KEY)

    @pl.when(jnp.max(overfull.astype(f32)) > 0.0)
    def _():
        real_thr = thr > NEG_INF_KEY
        next_f = key_to_float(thr + 1)
        beyond = score >= next_f
        in_bin = (score >= thr_f) & jnp.logical_not(beyond) & real_thr
        offset = jnp.where(in_bin, score - jnp.where(real_thr, thr_f, 0.0), -1.0)
        need = top_k - _col_count(beyond)

        okey_ref[...] = jnp.zeros(okey_ref.shape, jnp.int32)

        @pl.when(jnp.max(offset) > 0.0)
        def _():
            def offset_step(i, okey):
                cand = okey | (jnp.int32(1) << (30 - i))
                cnt = _col_count(offset >= lax.bitcast_convert_type(cand, f32))
                return jnp.where(cnt >= need, cand, okey)

            okey = lax.fori_loop(0, 31, offset_step, jnp.zeros((1, DSA_TQ), jnp.int32))
            okey_ref[...] = jnp.broadcast_to(okey, okey_ref.shape)

        othr = lax.bitcast_convert_type(okey_ref[0:1, :], f32)
        above = offset > othr
        tied = offset == othr
        need = need - _col_count(above)
        nbits = (width - 1).bit_length()

        def index_step(i, pos):
            cand = pos | (jnp.int32(1) << (nbits - 1 - i))
            return jnp.where(_col_count(tied & (row_s < cand)) < need, cand, pos)

        pos = lax.fori_loop(0, nbits, index_step, jnp.zeros((1, DSA_TQ), jnp.int32))
        keep = (beyond | above | (tied & (row_s <= pos))) & causal
        bias_ref[0:width, :] = jnp.where(keep, 0.0, -jnp.inf)


def _dsa_attn_body(q_ref, qi_ref, wit_ref, ki_ref, k_ref, vt_ref, o_ref,
                   okey_ref, bias_ref, logit_ref, *, width, top_k, t0, select):
    col_t = t0 + lax.broadcasted_iota(jnp.int32, (1, DSA_TQ), 1)
    row_s = lax.broadcasted_iota(jnp.int32, (width, 1), 0)
    causal = row_s <= col_t
    if select:
        _write_topk_bias(qi_ref, wit_ref, ki_ref, okey_ref, bias_ref, causal, row_s,
                         width=width, top_k=top_k)
    else:
        bias_ref[0:width, :] = jnp.where(causal, 0.0, -jnp.inf)

    group = ATT_HEADS // ATT_KV_HEADS
    rows = min(DSA_ROWS, width)
    for g in range(ATT_KV_HEADS):
        rq = _stack_heads(q_ref, g * group, group)
        maxes = [None] * group
        for rc in range(0, width, rows):
            logits4 = lax.dot_general(k_ref[0, rc:rc + rows, g * LANES:(g + 1) * LANES], rq, NT,
                                      preferred_element_type=f32)
            b = bias_ref[rc:rc + rows, :]
            for j in range(group):
                logits = logits4[:, j * DSA_TQ:(j + 1) * DSA_TQ] + b
                logit_ref[j, rc:rc + rows, :] = logits
                m = _fold_rows(logits, jnp.maximum)
                maxes[j] = m if maxes[j] is None else jnp.maximum(maxes[j], m)
        outs = []
        for j in range(group):
            mx = jnp.max(maxes[j], axis=0, keepdims=True)
            pv = None
            for rc in range(0, width, rows):
                p = jnp.exp2(logit_ref[j, rc:rc + rows, :] - mx).astype(bf16)
                part = jnp.dot(vt_ref[0, g * VT_ROWS:(g + 1) * VT_ROWS, rc:rc + rows], p,
                               preferred_element_type=f32)
                pv = part if pv is None else pv + part
            outs.append(pv[:HEAD_DIM] * (1.0 / pv[HEAD_DIM:HEAD_DIM + 1]))
        for pair in range(group // 2):
            both = jnp.concatenate(outs[2 * pair:2 * pair + 2], axis=0).T
            col = (g * group // 2 + pair) * LANES
            o_ref[:, col:col + LANES] = both.astype(bf16)


def _dsa_attn_kernel(q_ref, qi_ref, wit_ref, ki_ref, k_ref, vt_ref, o_ref,
                     okey_ref, bias_ref, logit_ref, *, seq, top_k):
    i = pl.program_id(1)
    refs = (q_ref, qi_ref, wit_ref, ki_ref, k_ref, vt_ref, o_ref, okey_ref, bias_ref, logit_ref)
    dense_blocks = top_k // DSA_TQ

    @pl.when(i < dense_blocks)
    def _():
        _dsa_attn_body(*refs, width=dense_blocks * DSA_TQ, top_k=top_k, t0=i * DSA_TQ, select=False)

    blocks_per_class = DSA_CLS // DSA_TQ
    for cls in range(seq // DSA_CLS):
        @pl.when((i >= dense_blocks) & (i // blocks_per_class == cls))
        def _(cls=cls):
            _dsa_attn_body(*refs, width=DSA_CLS * (cls + 1), top_k=top_k, t0=i * DSA_TQ, select=True)


def _dsa_attn(q, qi, wit, ki, k, vt, batch, seq):
    nq = seq // DSA_TQ
    top_k = min(TOPK_MAX, seq // 4)
    qrow = lambda n: pl.BlockSpec((DSA_TQ, n), lambda b, i: (b * nq + i, 0))
    full = lambda n: pl.BlockSpec((1, seq, n), lambda b, i: (b, 0, 0))
    return pl.pallas_call(
        functools.partial(_dsa_attn_kernel, seq=seq, top_k=top_k),
        grid=(batch, nq),
        in_specs=[qrow(ATT_HEADS * LANES), qrow(IDX_HEADS * LANES),
                  pl.BlockSpec((8, DSA_TQ), lambda b, i: (0, b * nq + i)),
                  full(LANES), full(ATT_KV_HEADS * LANES),
                  pl.BlockSpec((1, ATT_KV_HEADS * VT_ROWS, seq), lambda b, i: (b, 0, 0))],
        out_specs=qrow(ATT_HEADS * HEAD_DIM),
        out_shape=jax.ShapeDtypeStruct((batch * seq, ATT_HEADS * HEAD_DIM), bf16),
        scratch_shapes=[pltpu.VMEM((8, DSA_TQ), jnp.int32),
                        pltpu.VMEM((seq, DSA_TQ), f32),
                        pltpu.VMEM((ATT_HEADS // ATT_KV_HEADS, seq, DSA_TQ), f32)],
        compiler_params=_cparams("parallel", "arbitrary"),
        name="dsa_attn",
    )(q, qi, wit, ki, k, vt)


RW_TM = 512
LORA_PAD = 128
GATE_PAD = 256


def _rwkv_pre_kernel(x_ref, xp_ref, g_ref, mu_ref, wr_ref, wk_ref, wv_ref,
                     w1_ref, w2_ref, a1_ref, a2_ref, g1_ref, g2_ref,
                     w0_ref, a0_ref, kk_ref, ka_ref, rk_ref, tri_ref, hred_ref, hexp_ref,
                     at_ref, bt_ref, kt_ref, rt_ref, v_ref, gc_ref, bonus_ref, gate_ref,
                     lc_scr, *, seq):
    i = pl.program_id(0)
    g = g_ref[...]
    h = _rms(x_ref[...], g)
    hp = _rms(xp_ref[...], g)[7:8, :]
    hp = jnp.where((i * RW_TM) % seq == 0, 0.0, hp)
    rowi = lax.broadcasted_iota(jnp.int32, (RW_TM, 1), 0)
    hs = jnp.where(rowi == 0, hp, pltpu.roll(h, 1, 0))
    xx = hs - h
    mu = mu_ref[...]

    def mix(c):
        return (h + xx * mu[c:c + 1, :]).astype(bf16)

    r = jnp.dot(mix(0), wr_ref[...], preferred_element_type=f32)
    k = jnp.dot(mix(1), wk_ref[...], preferred_element_type=f32)
    v = jnp.dot(mix(2), wv_ref[...], preferred_element_type=f32)

    wl = w0_ref[...] + _bdot(jnp.tanh(jnp.dot(mix(3), w1_ref[...], preferred_element_type=f32)),
                             w2_ref[...])
    nwl = -wl
    softplus = jnp.maximum(nwl, 0.0) + jnp.log1p(jnp.exp(-jnp.abs(nwl)))
    ld = -jnp.exp(-softplus - 0.5)
    a = jax.nn.sigmoid(a0_ref[...] + _bdot(jnp.dot(mix(4), a1_ref[...],
                                                   preferred_element_type=f32), a2_ref[...]))
    gate_ref[...] = _bdot(jax.nn.sigmoid(jnp.dot(mix(5), g1_ref[...],
                                                 preferred_element_type=f32)), g2_ref[...]).astype(bf16)

    hred = hred_ref[...]
    hexp = hexp_ref[...]
    z = k * kk_ref[...]
    kk = z * lax.rsqrt(jnp.maximum(_head_sum(z * z, hred, hexp), 1e-24))
    k2 = k * (1.0 + (a - 1.0) * ka_ref[...])
    bonus_ref[...] = (_head_sum(r * k2 * rk_ref[...], hred, hexp) * v).astype(bf16)

    lc = _split_dot_left(tri_ref[...], ld, 2)
    lc_scr[...] = lc
    for c in range(RW_TM // CHUNK):
        last = lc_scr[c * CHUNK + CHUNK - 1:c * CHUNK + CHUNK, :]
        gc_ref[c * 8:(c + 1) * 8, :] = jnp.broadcast_to(jnp.exp(last), (8, D_MODEL))
    einv = jnp.exp(-lc)
    at_ref[...] = -kk * jnp.exp(lc - ld)
    bt_ref[...] = kk * a * einv
    kt_ref[...] = k2 * einv
    rt_ref[...] = r * jnp.exp(lc)
    v_ref[...] = v


def _split_dot_left(a, b, parts):
    out = None
    rem = b
    for p in range(parts):
        piece = rem.astype(bf16)
        term = jnp.dot(a, piece, preferred_element_type=f32)
        out = term if out is None else out + term
        if p + 1 < parts:
            rem = rem - piece.astype(f32)
    return out


def _rwkv_pre(x2, g, mu8, wr, wk, wv, w1, w2, a1, a2, g1, g2, w0, a0, k_k, k_a, r_k, tri, hred, hexp,
              seq):
    m = x2.shape[0]
    row = pl.BlockSpec((RW_TM, D_MODEL), lambda i: (i, 0))
    prev = pl.BlockSpec((8, D_MODEL), lambda i: (jnp.maximum(i * (RW_TM // 8) - 1, 0), 0))
    vec = _const_spec((1, D_MODEL))
    gcrow = pl.BlockSpec((RW_TM // CHUNK * 8, D_MODEL), lambda i: (i, 0))
    act = jax.ShapeDtypeStruct((m, D_MODEL), f32)
    return pl.pallas_call(
        functools.partial(_rwkv_pre_kernel, seq=seq),
        grid=(m // RW_TM,),
        in_specs=[row, prev, vec, _const_spec((8, D_MODEL)),
                  _const_spec(wr.shape), _const_spec(wk.shape), _const_spec(wv.shape),
                  _const_spec(w1.shape), _const_spec(w2.shape),
                  _const_spec(a1.shape), _const_spec(a2.shape),
                  _const_spec(g1.shape), _const_spec(g2.shape),
                  vec, vec, vec, vec, vec,
                  _const_spec(tri.shape), _const_spec(hred.shape), _const_spec(hexp.shape)],
        out_specs=[row, row, row, row, row, gcrow, row, row],
        out_shape=[act, act, act, act, act,
                   jax.ShapeDtypeStruct((m // CHUNK * 8, D_MODEL), f32),
                   jax.ShapeDtypeStruct((m, D_MODEL), bf16), jax.ShapeDtypeStruct((m, D_MODEL), bf16)],
        scratch_shapes=[pltpu.VMEM((RW_TM, D_MODEL), f32)],
        compiler_params=_cparams("parallel"),
        name="rwkv_pre",
    )(x2, x2, g, mu8, wr, wk, wv, w1, w2, a1, a2, g1, g2, w0, a0, k_k, k_a, r_k, tri, hred, hexp)


RA_GROUP = 2 * CHUNK
RA_ROWS = 1024


def _rwkv_chunk_kernel(at_ref, bt_ref, kt_ref, rt_ref, v_ref, gc_ref,
                       q_ref, yl_ref, m_ref, g_ref):
    n = RA_GROUP
    ri = lax.broadcasted_iota(jnp.int32, (n, n), 0)
    ci = lax.broadcasted_iota(jnp.int32, (n, n), 1)
    same = (ri // CHUNK) == (ci // CHUNK)
    strict = same & (ci < ri)
    incl = same & (ci <= ri)
    eye = (ri == ci).astype(f32)
    lane = lax.broadcasted_iota(jnp.int32, (1, LANES), 1)
    low = lane < RWKV_HEAD
    rk = lax.broadcasted_iota(jnp.int32, (RWKV_HEAD, LANES), 0)
    diag = rk == (lax.broadcasted_iota(jnp.int32, (RWKV_HEAD, LANES), 1) % RWKV_HEAD)

    groups = []
    for gi in range(RA_ROWS // n):
        rows = slice(gi * n, (gi + 1) * n)
        at, bt, kt, rt, v = (r[rows, :] for r in (at_ref, bt_ref, kt_ref, rt_ref, v_ref))
        groups.append(dict(at=at, bt=bt, kt=kt, rt=rt, v=v, vb=v.astype(bf16),
                           ar=jnp.concatenate([at, rt], axis=0),
                           bk=jnp.concatenate([bt, kt], axis=0).astype(bf16)))
    probs = [(gr, hl) for gr in groups for hl in (low, jnp.logical_not(low))]

    gs = [lax.dot_general(jnp.where(hl, gr["ar"], 0.0).astype(bf16), gr["bk"], NT,
                          preferred_element_type=f32) for gr, hl in probs]
    aab = [jnp.where(strict, g[:n, :n], 0.0) for g in gs]
    aak = [jnp.where(strict, g[:n, n:], 0.0) for g in gs]
    arb = [jnp.where(incl, g[n:, :n], 0.0) for g in gs]
    ark = [jnp.where(incl, g[n:, n:], 0.0) for g in gs]

    ts = [eye + a for a in aab]
    ps = aab
    for _ in range((CHUNK - 1).bit_length() - 1):
        ps = [_bdot(p, p) for p in ps]
        ts = [t + _bdot(t, p) for t, p in zip(ts, ps)]

    akv = [_bdot(a, gr["vb"]) for a, (gr, _) in zip(aak, probs)]
    tw = [_bdot(t, jnp.concatenate([gr["at"], x], axis=1))
          for t, x, (gr, _) in zip(ts, akv, probs)]
    qy = [_bdot(a, w) for a, w in zip(arb, tw)]
    rkv = [_bdot(a, gr["vb"]) for a, (gr, _) in zip(ark, probs)]

    for gi, gr in enumerate(groups):
        lo, hi = 2 * gi, 2 * gi + 1
        w_hat = jnp.where(low, tw[lo][:, :LANES], tw[hi][:, :LANES])
        u_loc = jnp.where(low, tw[lo][:, LANES:], tw[hi][:, LANES:])
        base = gi * n
        q_ref[base:base + n, :] = gr["rt"] + jnp.where(low, qy[lo][:, :LANES], qy[hi][:, :LANES])
        yl_ref[base:base + n, :] = (jnp.where(low, qy[lo][:, LANES:], qy[hi][:, LANES:])
                                    + jnp.where(low, rkv[lo], rkv[hi]))
        for c in range(n // CHUNK):
            rows = slice(c * CHUNK, (c + 1) * CHUNK)
            ch = base // CHUNK + c
            gc = gc_ref[ch * 8:ch * 8 + 1, :]
            bh = gr["bt"][rows] * gc
            kh = gr["kt"][rows] * gc
            pm = _bdot(w_hat[rows], bh, TN)
            pg = _bdot(jnp.concatenate([u_loc[rows], gr["v"][rows]], axis=0),
                       jnp.concatenate([bh, kh], axis=0), TN)
            out = slice(base + c * CHUNK, base + (c + 1) * CHUNK)
            m_ref[out, :] = (jnp.where(low, pm[:RWKV_HEAD], pm[RWKV_HEAD:])
                             + jnp.where(diag, gc, 0.0))
            g_ref[out, :] = jnp.where(low, pg[:RWKV_HEAD], pg[RWKV_HEAD:])


def _rwkv_chunk(at, bt, kt, rt, v, gc):
    m = at.shape[0]
    blk = pl.BlockSpec((RA_ROWS, LANES), lambda i, p: (i, p))
    gblk = pl.BlockSpec((RA_ROWS // CHUNK * 8, LANES), lambda i, p: (i, p))
    act = jax.ShapeDtypeStruct((m, D_MODEL), f32)
    return pl.pallas_call(
        _rwkv_chunk_kernel,
        grid=(m // RA_ROWS, D_MODEL // LANES),
        in_specs=[blk, blk, blk, blk, blk, gblk],
        out_specs=[blk, blk, blk, blk],
        out_shape=[act, act, act, act],
        compiler_params=_cparams("parallel", "parallel"),
        name="rwkv_chunk",
    )(at, bt, kt, rt, v, gc)


SCAN_LANES = D_MODEL
SCAN_SPLIT = 2


def _rwkv_scan_kernel(q_ref, yl_ref, m_ref, g_ref, y_ref, state_ref, *, seq):
    pairs = SCAN_LANES // LANES
    low = lax.broadcasted_iota(jnp.int32, (1, LANES), 1) < RWKV_HEAD

    def blockdiag(x):
        return jnp.concatenate([jnp.where(low, x, 0.0), jnp.where(low, 0.0, x)], axis=0)

    def body(c, states):
        rows = pl.ds(pl.multiple_of(c * CHUNK, CHUNK), CHUNK)
        new_states = []
        for p in range(pairs):
            cols = slice(p * LANES, (p + 1) * LANES)
            s = states[p]
            mc = blockdiag(m_ref[rows, cols]).astype(bf16)
            s_hi = s.astype(bf16)
            s_lo = (s - s_hi.astype(f32)).astype(bf16)
            sm = (jnp.dot(s_hi, mc, preferred_element_type=f32)
                  + jnp.dot(s_lo, mc, preferred_element_type=f32))
            new_states.append(sm + g_ref[rows, cols])
        for p in range(pairs):
            cols = slice(p * LANES, (p + 1) * LANES)
            y_ref[rows, cols] = (yl_ref[rows, cols]
                                 + _bdot(q_ref[rows, cols], blockdiag(states[p]), NT))
        return tuple(new_states)

    @pl.when(pl.program_id(1) == 0)
    def _():
        state_ref[...] = jnp.zeros(state_ref.shape, f32)

    init = tuple(state_ref[p] for p in range(pairs))
    final = lax.fori_loop(0, seq // SCAN_SPLIT // CHUNK, body, init)
    for p in range(pairs):
        state_ref[p] = final[p]


def _rwkv_scan(q, yl, mm, gg, batch, seq):
    rows = seq // SCAN_SPLIT
    blk = pl.BlockSpec((rows, SCAN_LANES), lambda b, s: (b * SCAN_SPLIT + s, 0))
    return pl.pallas_call(
        functools.partial(_rwkv_scan_kernel, seq=seq),
        grid=(batch, SCAN_SPLIT),
        in_specs=[blk, blk, blk, blk],
        out_specs=blk,
        out_shape=jax.ShapeDtypeStruct((batch * seq, D_MODEL), f32),
        scratch_shapes=[pltpu.VMEM((SCAN_LANES // LANES, RWKV_HEAD, LANES), f32)],
        compiler_params=_cparams("parallel", "arbitrary"),
        name="rwkv_scan",
    )(q, yl, mm, gg)


def _dup_heads(w, n_heads):
    w = w.reshape(w.shape[0], n_heads, 1, HEAD_DIM)
    return jnp.broadcast_to(w, (w.shape[0], n_heads, 2, HEAD_DIM)).reshape(w.shape[0], n_heads * LANES)


def _pad_cols(w, n):
    return jnp.pad(w, ((0, 0), (0, n - w.shape[1])))


def _pad_rows(w, n):
    return jnp.pad(w, ((0, n - w.shape[0]), (0, 0)))


def _rope_tables(seq):
    inv = 1.0 / (ROPE_THETA ** (jnp.arange(0, HEAD_DIM, 2, dtype=f32) / HEAD_DIM))
    ang = jnp.arange(seq, dtype=f32)[:, None] * inv[None, :]
    cos, sin = jnp.cos(ang), jnp.sin(ang)
    return jnp.tile(jnp.concatenate([cos, cos], 1), (1, 2)), jnp.tile(jnp.concatenate([-sin, sin], 1), (1, 2))


def _dsa_layer(x2, g, w_in, w_o, batch, seq):
    o0 = ATT_HEADS * HEAD_DIM
    o1 = o0 + ATT_KV_HEADS * HEAD_DIM
    o2 = o1 + ATT_KV_HEADS * HEAD_DIM
    o3 = o2 + IDX_HEADS * HEAD_DIM
    o4 = o3 + HEAD_DIM
    w_in = w_in.astype(bf16)
    wq = w_in[:, :o0]
    wk2 = _dup_heads(w_in[:, o0:o1], ATT_KV_HEADS)
    wv = w_in[:, o1:o2].T
    wqi = w_in[:, o2:o3]
    wki2 = _dup_heads(w_in[:, o3:o4], 1)
    wwi = _pad_cols(w_in[:, o4:], LANES).T
    cos, sin = _rope_tables(seq)
    q, k, vt, qi, ki, wit = _dsa_proj(x2, g, wq, wk2, wv, wqi, wki2, wwi, cos, sin, batch, seq)
    o = _dsa_attn(q, qi, wit, ki, k, vt, batch, seq)
    return _attn_mlp_kernel, "attn_mlp", [o, x2], [w_o.astype(bf16)]


def _rwkv_layer(x2, g, mu, w_rkv, w0, w1, w2, a0, a1, a2, g1, g2, k_k, k_a, r_k, lnx_w, lnx_b, w_o,
                batch, seq):
    vec = lambda p: p.reshape(1, D_MODEL)
    w_rkv = w_rkv.astype(bf16)
    ri = jnp.arange(RW_TM)
    tri = ((ri[:, None] >= ri[None, :]) & (ri[:, None] // CHUNK == ri[None, :] // CHUNK)).astype(bf16)
    di = jnp.arange(D_MODEL)
    hred = (di[:, None] // RWKV_HEAD == jnp.arange(LANES)[None, :]).astype(bf16)
    hexp = jnp.concatenate([hred.T, hred.T], axis=0)
    at, bt, kt, rt, v, gc, bonus, gate = _rwkv_pre(
        x2, g, _pad_rows(mu, 8), w_rkv[0], w_rkv[1], w_rkv[2],
        _pad_cols(w1, LORA_PAD).astype(bf16), _pad_rows(w2, LORA_PAD).astype(bf16),
        _pad_cols(a1, LORA_PAD).astype(bf16), _pad_rows(a2, LORA_PAD).astype(bf16),
        _pad_cols(g1, GATE_PAD).astype(bf16), _pad_rows(g2, GATE_PAD).astype(bf16),
        vec(w0), vec(a0), vec(k_k), vec(k_a), vec(r_k), tri, hred, hexp, seq)
    q, yl, mm, gg = _rwkv_chunk(at, bt, kt, rt, v, gc)
    y = _rwkv_scan(q, yl, mm, gg, batch, seq)
    return (_rwkv_mlp_kernel, "rwkv_mlp", [y, bonus, gate, x2],
            [vec(lnx_w), vec(lnx_b), hred, hexp, w_o.astype(bf16)])


def kernel(x, mixer_norm, mlp_norm, mlp_w_up, mlp_w_down, final_norm, dsa_w_in, dsa_w_o, rwkv_mu, rwkv_w_rkv, rwkv_w0, rwkv_w1, rwkv_w2, rwkv_a0, rwkv_a1, rwkv_a2, rwkv_g1, rwkv_g2, rwkv_k_k, rwkv_k_a, rwkv_r_k, rwkv_lnx_w, rwkv_lnx_b, rwkv_w_o):
    batch, seq, _ = x.shape
    depth = mixer_norm.shape[0]
    x2 = x.reshape(batch * seq, D_MODEL)
    fg = final_norm.reshape(1, D_MODEL)
    w_up = mlp_w_up.astype(bf16)
    w_down = mlp_w_down.astype(bf16)
    for i in range(depth):
        g = mixer_norm[i].reshape(1, D_MODEL)
        j = i // 2
        if i % 2 == 0:
            tail = _dsa_layer(x2, g, dsa_w_in[j], dsa_w_o[j], batch, seq)
        else:
            tail = _rwkv_layer(x2, g, rwkv_mu[j], rwkv_w_rkv[j], rwkv_w0[j], rwkv_w1[j], rwkv_w2[j],
                               rwkv_a0[j], rwkv_a1[j], rwkv_a2[j], rwkv_g1[j], rwkv_g2[j],
                               rwkv_k_k[j], rwkv_k_a[j], rwkv_r_k[j], rwkv_lnx_w[j], rwkv_lnx_b[j],
                               rwkv_w_o[j], batch, seq)
        x2 = _mixer_mlp(*tail, mlp_norm[i].reshape(1, D_MODEL), w_up, w_down, i, fg,
                        final=(i == depth - 1))
    return x2.reshape(batch, seq, D_MODEL)
```
